```python
import math
import jax, jax.numpy as jnp
from jax import lax
import numpy as np

D_MODEL = 2048
BATCH = 4
SEQ = 2048
DEPTH = 1
DEC_BATCH = 128
DEC_SEQ = 1
PAST_LEN = 16384
PAGE_SIZE = 128

N_META = 16
D_RNN = D_MODEL // 2
RG_HEADS = 8
RG_BLOCK = D_RNN // RG_HEADS
CONV_W = 4
LRU_C = 8.0
S5_CH = 16
S5_GROUPS = (D_MODEL // 2) // S5_CH
D_S5 = S5_GROUPS * S5_CH
S5_N = 64
N_IN = 2 * D_RNN + D_S5 + 2 * D_MODEL
N_EXPERTS = 64
TOP_K = 8
N_EXPERT_GROUPS = 8
TOPK_GROUPS = 4
D_EXPERT = 512
D_SHARED = D_EXPERT
ROUTED_SCALE = 2.5
LN_EPS = 1e-5

kernel_name = "hawk_s5_moe_deepnorm_meta_step"

F32 = jnp.float32


def _layernorm(x, g, b):
    xf = x.astype(F32)
    mu = jnp.mean(xf, axis=-1, keepdims=True)
    xc = xf - mu
    var = jnp.mean(xc * xc, axis=-1, keepdims=True)
    y = xc * lax.rsqrt(var + LN_EPS) * g.astype(F32) + b.astype(F32)
    return y.astype(x.dtype)


def _causal_conv(u, buf, w, b):
    T = u.shape[1]
    full = jnp.concatenate([buf.astype(u.dtype), u], axis=1)
    out = b + full[:, 0:T] * w[0]
    for k in range(1, CONV_W):
        out = out + full[:, k:k + T] * w[k]
    return out, full[:, -(CONV_W - 1):]


def _rglru(xc, h0, wa, ba, wi, bi, lam):
    B, T, _ = xc.shape
    xf = xc.astype(F32)
    xh = xf.reshape(B, T, RG_HEADS, RG_BLOCK)
    r = jax.nn.sigmoid(jnp.einsum('bthi,hij->bthj', xh, wa.astype(F32)).reshape(B, T, D_RNN) + ba.astype(F32))
    i = jax.nn.sigmoid(jnp.einsum('bthi,hij->bthj', xh, wi.astype(F32)).reshape(B, T, D_RNN) + bi.astype(F32))
    log_a = -LRU_C * r * jax.nn.softplus(-lam.astype(F32))
    a = jnp.exp(log_a)
    u = jnp.sqrt(-jnp.expm1(2.0 * log_a)) * (i * xf)

    def step(h, au):
        a_t, u_t = au
        h = a_t * h + u_t
        return h, h

    hT, hs = lax.scan(step, h0.astype(F32), (jnp.swapaxes(a, 0, 1), jnp.swapaxes(u, 0, 1)))
    return jnp.swapaxes(hs, 0, 1), hT


def _cplx_combine(e1, e2):
    a1r, a1i, b1r, b1i = e1
    a2r, a2i, b2r, b2i = e2
    return (a2r * a1r - a2i * a1i,
            a2r * a1i + a2i * a1r,
            a2r * b1r - a2i * b1i + b2r,
            a2r * b1i + a2i * b1r + b2i)


def _s5(u, h0, a_re, a_im, b_re, b_im, c_re, c_im, d, log_dt):
    B, T, _ = u.shape
    uf = u.astype(F32).reshape(B, T, S5_GROUPS, S5_CH)
    a_re = a_re.astype(F32); a_im = a_im.astype(F32)
    b_re = b_re.astype(F32); b_im = b_im.astype(F32)
    dt = jnp.exp(log_dt.astype(F32))[:, None]
    mag = jnp.exp(a_re * dt)
    ab_r = mag * jnp.cos(a_im * dt)
    ab_i = mag * jnp.sin(a_im * dt)
    den = a_re * a_re + a_im * a_im
    nr = ab_r - 1.0
    cr = (nr * a_re + ab_i * a_im) / den
    ci = (ab_i * a_re - nr * a_im) / den
    bb_r = cr[..., None] * b_re - ci[..., None] * b_im
    bb_i = cr[..., None] * b_im + ci[..., None] * b_re
    bu_r = jnp.einsum('btgc,gnc->btgn', uf, bb_r)
    bu_i = jnp.einsum('btgc,gnc->btgn', uf, bb_i)
    ar = jnp.broadcast_to(ab_r, bu_r.shape)
    ai = jnp.broadcast_to(ab_i, bu_i.shape)
    acum_r, acum_i, hr, hi = lax.associative_scan(_cplx_combine, (ar, ai, bu_r, bu_i), axis=1)
    if h0 is not None:
        h0r = h0[0].astype(F32)[:, None]
        h0i = h0[1].astype(F32)[:, None]
        hr = hr + acum_r * h0r - acum_i * h0i
        hi = hi + acum_r * h0i + acum_i * h0r
    y = (jnp.einsum('gcn,btgn->btgc', c_re.astype(F32), hr)
         - jnp.einsum('gcn,btgn->btgc', c_im.astype(F32), hi)
         + d.astype(F32) * uf)
    return y.reshape(B, T, D_S5), hr[:, -1], hi[:, -1]


def _swiglu(x, w1, w3, w2):
    return (jax.nn.silu(x @ w1) * (x @ w3)) @ w2


def _moe(x, p):
    B, T, D = x.shape
    xt = x.reshape(-1, D)
    scores = jax.nn.sigmoid((xt @ p['router_w']).astype(F32))
    sel = scores + p['router_bias'].astype(F32)
    grp = sel.reshape(-1, N_EXPERT_GROUPS, N_EXPERTS // N_EXPERT_GROUPS)
    gscore = jnp.sum(lax.top_k(grp, 2)[0], axis=-1)
    _, gidx = lax.top_k(gscore, TOPK_GROUPS)
    gmask = jnp.sum(jax.nn.one_hot(gidx, N_EXPERT_GROUPS, dtype=F32), axis=-2) > 0
    masked = jnp.where(gmask[..., None], grp, -jnp.inf).reshape(-1, N_EXPERTS)
    _, eidx = lax.top_k(masked, TOP_K)
    w = jnp.take_along_axis(scores, eidx, axis=-1)
    w = w / jnp.sum(w, axis=-1, keepdims=True) * ROUTED_SCALE
    gates = jnp.sum(jax.nn.one_hot(eidx, N_EXPERTS, dtype=F32) * w[..., None], axis=-2).astype(x.dtype)
    out = _swiglu(xt, p['sh_w1'], p['sh_w3'], p['sh_w2'])
    for e in range(N_EXPERTS):
        out = out + gates[:, e:e + 1] * _swiglu(xt, p['ex_w1'][e], p['ex_w3'][e], p['ex_w2'][e])
    return out.reshape(B, T, D)


def _layer(x, rg_h0, conv0, s5_h0, p, alpha):
    dt = x.dtype
    z = x @ p['w_in'] + p['b_in']
    o1, o2, o3, o4 = D_RNN, 2 * D_RNN, 2 * D_RNN + D_S5, 2 * D_RNN + D_S5 + D_MODEL
    xa, ya, us, ga, gb = z[..., :o1], z[..., o1:o2], z[..., o2:o3], z[..., o3:o4], z[..., o4:]
    xc, conv_new = _causal_conv(xa, conv0, p['conv_w'], p['conv_b'])
    hs, rg_hT = _rglru(xc, rg_h0, p['rg_wa'], p['rg_ba'], p['rg_wi'], p['rg_bi'], p['rg_lambda'])
    branch_a = (hs * jax.nn.gelu(ya.astype(F32))).astype(dt) @ p['proj_a']
    y5, s5r, s5i = _s5(us, s5_h0, p['s5_a_re'], p['s5_a_im'], p['s5_b_re'], p['s5_b_im'],
                       p['s5_c_re'], p['s5_c_im'], p['s5_d'], p['s5_log_dt'])
    g5 = jax.nn.gelu(y5)
    glu = g5 * jax.nn.sigmoid(g5 @ p['glu_w'].astype(F32) + p['glu_b'].astype(F32))
    branch_b = glu.astype(dt) @ p['proj_b']
    merged = jax.nn.sigmoid(ga) * branch_a + jax.nn.sigmoid(gb) * branch_b
    x = _layernorm(alpha * x + merged @ p['w_o'], p['ln1_g'], p['ln1_b'])
    x = _layernorm(alpha * x + _moe(x, p), p['ln2_g'], p['ln2_b'])
    return x, (rg_hT.astype(dt), conv_new, s5r.astype(dt), s5i.astype(dt))


def setup_inputs(seed: int = 0) -> dict:
    key = jax.random.key(seed)
    ks = iter(jax.random.split(key, 64))
    nrm = lambda shape, s=1.0: jax.random.normal(next(ks), shape, F32) * s
    beta = (8.0 * DEPTH) ** -0.25
    L = DEPTH
    a0 = jax.random.uniform(next(ks), (L, D_RNN), F32, 0.9, 0.999)
    inp = {}
    inp['x_prompt'] = nrm((BATCH, SEQ, D_MODEL))
    inp['x_sample'] = nrm((DEC_BATCH, DEC_SEQ, D_MODEL))
    inp['state_rglru_h'] = nrm((L, DEC_BATCH, D_RNN), 0.5)
    inp['state_conv'] = nrm((L, DEC_BATCH, CONV_W - 1, D_RNN))
    inp['state_s5_re'] = nrm((L, DEC_BATCH, S5_GROUPS, S5_N), 0.3)
    inp['state_s5_im'] = nrm((L, DEC_BATCH, S5_GROUPS, S5_N), 0.3)
    inp['meta_tokens'] = nrm((N_META, D_MODEL))
    inp['ln_in_g'] = 1.0 + nrm((D_MODEL,), 0.02)
    inp['ln_in_b'] = nrm((D_MODEL,), 0.02)
    inp['w_in'] = nrm((L, D_MODEL, N_IN), D_MODEL ** -0.5)
    inp['b_in'] = nrm((L, N_IN), 0.02)
    inp['conv_w'] = nrm((L, CONV_W, D_RNN), CONV_W ** -0.5)
    inp['conv_b'] = nrm((L, D_RNN), 0.02)
    inp['rg_wa'] = nrm((L, RG_HEADS, RG_BLOCK, RG_BLOCK), RG_BLOCK ** -0.5)
    inp['rg_ba'] = nrm((L, D_RNN), 0.02)
    inp['rg_wi'] = nrm((L, RG_HEADS, RG_BLOCK, RG_BLOCK), RG_BLOCK ** -0.5)
    inp['rg_bi'] = nrm((L, D_RNN), 0.02)
    inp['rg_lambda'] = jnp.log(a0) - jnp.log1p(-a0)
    inp['s5_a_re'] = -0.5 + nrm((L, S5_GROUPS, S5_N), 0.01)
    inp['s5_a_im'] = jnp.pi * jnp.arange(S5_N, dtype=F32) + nrm((L, S5_GROUPS, S5_N), 0.01)
    inp['s5_b_re'] = nrm((L, S5_GROUPS, S5_N, S5_CH), (2 * S5_CH) ** -0.5)
    inp['s5_b_im'] = nrm((L, S5_GROUPS, S5_N, S5_CH), (2 * S5_CH) ** -0.5)
    inp['s5_c_re'] = nrm((L, S5_GROUPS, S5_CH, S5_N), (2 * S5_N) ** -0.5)
    inp['s5_c_im'] = nrm((L, S5_GROUPS, S5_CH, S5_N), (2 * S5_N) ** -0.5)
    inp['s5_d'] = nrm((L, S5_GROUPS, S5_CH))
    inp['s5_log_dt'] = jax.random.uniform(next(ks), (L, S5_GROUPS), F32, math.log(0.001), math.log(0.1))
    inp['glu_w'] = nrm((L, D_S5, D_S5), D_S5 ** -0.5)
    inp['glu_b'] = nrm((L, D_S5), 0.02)
    inp['proj_a'] = nrm((L, D_RNN, D_MODEL), D_RNN ** -0.5)
    inp['proj_b'] = nrm((L, D_S5, D_MODEL), D_S5 ** -0.5)
    inp['w_o'] = nrm((L, D_MODEL, D_MODEL), beta * D_MODEL ** -0.5)
    inp['ln1_g'] = 1.0 + nrm((L, D_MODEL), 0.02)
    inp['ln1_b'] = nrm((L, D_MODEL), 0.02)
    inp['router_w'] = nrm((L, D_MODEL, N_EXPERTS), D_MODEL ** -0.5)
    inp['router_bias'] = nrm((L, N_EXPERTS), 0.01)
    inp['ex_w1'] = nrm((L, N_EXPERTS, D_MODEL, D_EXPERT), D_MODEL ** -0.5)
    inp['ex_w3'] = nrm((L, N_EXPERTS, D_MODEL, D_EXPERT), D_MODEL ** -0.5)
    inp['ex_w2'] = nrm((L, N_EXPERTS, D_EXPERT, D_MODEL), beta * D_EXPERT ** -0.5)
    inp['sh_w1'] = nrm((L, D_MODEL, D_SHARED), D_MODEL ** -0.5)
    inp['sh_w3'] = nrm((L, D_MODEL, D_SHARED), D_MODEL ** -0.5)
    inp['sh_w2'] = nrm((L, D_SHARED, D_MODEL), beta * D_SHARED ** -0.5)
    inp['ln2_g'] = 1.0 + nrm((L, D_MODEL), 0.02)
    inp['ln2_b'] = nrm((L, D_MODEL), 0.02)
    return inp


def reference(x_prompt, x_sample, state_rglru_h, state_conv, state_s5_re, state_s5_im,
              meta_tokens, ln_in_g, ln_in_b, w_in, b_in, conv_w, conv_b,
              rg_wa, rg_ba, rg_wi, rg_bi, rg_lambda,
              s5_a_re, s5_a_im, s5_b_re, s5_b_im, s5_c_re, s5_c_im, s5_d, s5_log_dt,
              glu_w, glu_b, proj_a, proj_b, w_o, ln1_g, ln1_b,
              router_w, router_bias, ex_w1, ex_w3, ex_w2, sh_w1, sh_w3, sh_w2, ln2_g, ln2_b):
    alpha = (2.0 * DEPTH) ** 0.25
    bp = x_prompt.shape[0]
    meta = jnp.broadcast_to(meta_tokens.astype(x_prompt.dtype)[None], (bp, N_META, D_MODEL))
    xp = _layernorm(jnp.concatenate([meta, x_prompt], axis=1), ln_in_g, ln_in_b)
    xs = _layernorm(x_sample, ln_in_g, ln_in_b)
    p_h, p_c, p_r, p_i = [], [], [], []
    s_h, s_c, s_r, s_i = [], [], [], []
    for l in range(DEPTH):
        p = dict(w_in=w_in[l], b_in=b_in[l], conv_w=conv_w[l], conv_b=conv_b[l],
                 rg_wa=rg_wa[l], rg_ba=rg_ba[l], rg_wi=rg_wi[l], rg_bi=rg_bi[l], rg_lambda=rg_lambda[l],
                 s5_a_re=s5_a_re[l], s5_a_im=s5_a_im[l], s5_b_re=s5_b_re[l], s5_b_im=s5_b_im[l],
                 s5_c_re=s5_c_re[l], s5_c_im=s5_c_im[l], s5_d=s5_d[l], s5_log_dt=s5_log_dt[l],
                 glu_w=glu_w[l], glu_b=glu_b[l], proj_a=proj_a[l], proj_b=proj_b[l], w_o=w_o[l],
                 ln1_g=ln1_g[l], ln1_b=ln1_b[l], router_w=router_w[l], router_bias=router_bias[l],
                 ex_w1=ex_w1[l], ex_w3=ex_w3[l], ex_w2=ex_w2[l],
                 sh_w1=sh_w1[l], sh_w3=sh_w3[l], sh_w2=sh_w2[l], ln2_g=ln2_g[l], ln2_b=ln2_b[l])
        h0p = jnp.zeros((bp, D_RNN), xp.dtype)
        c0p = jnp.zeros((bp, CONV_W - 1, D_RNN), xp.dtype)
        xp, (ph, pc, pr, pi) = _layer(xp, h0p, c0p, None, p, alpha)
        xs, (sh, sc, sr, si) = _layer(xs, state_rglru_h[l], state_conv[l],
                                      (state_s5_re[l], state_s5_im[l]), p, alpha)
        p_h.append(ph); p_c.append(pc); p_r.append(pr); p_i.append(pi)
        s_h.append(sh); s_c.append(sc); s_r.append(sr); s_i.append(si)
    y_prompt = xp[:, N_META:]
    y_sample = xs
    return (y_prompt, y_sample,
            jnp.stack(p_h), jnp.stack(p_c), jnp.stack(p_r), jnp.stack(p_i),
            jnp.stack(s_h), jnp.stack(s_c), jnp.stack(s_r), jnp.stack(s_i))
```

```python
import functools
import math

import jax
import jax.numpy as jnp
from jax import lax
from jax.experimental import pallas as pl
from jax.experimental.pallas import tpu as pltpu

F32 = jnp.float32
BF16 = jnp.bfloat16
I32 = jnp.int32
U32 = jnp.uint32

D_MODEL = 2048
D_RNN = D_MODEL // 2
D_S5 = D_MODEL // 2
N_IN = 2 * D_RNN + D_S5 + 2 * D_MODEL
RG_HEADS = 8
CONV_W = 4
LRU_C = 8.0
S5_CH = 16
S5_GROUPS = D_S5 // S5_CH
S5_N = 64
S5_STATE = S5_GROUPS * S5_N
N_EXPERTS = 64
TOP_K = 8
N_GROUPS = 8
GROUP_SIZE = N_EXPERTS // N_GROUPS
TOPK_GROUPS = 4
D_EXPERT = 512
ROUTED_SCALE = 2.5
LN_EPS = 1e-5
N_META = 16
DEPTH = 1
ALPHA = (2.0 * DEPTH) ** 0.25

SUBLANES = 8
LANES = 128
MXU_DIM = 256
VMEM_LIMIT = 56 * 1024 * 1024

NSEQ = 4
S5_BLOCK_GROUPS = MXU_DIM // S5_CH
S5_BLOCKS = S5_GROUPS // S5_BLOCK_GROUPS
S5_BLOCK_STATE = S5_BLOCK_GROUPS * S5_N
IN_TILE_N = 512
POST_TILE = 320
TOKEN_TILE = 640
MOE_TILE = 256
FIN_TILE = 128


def _params(sem, vmem=VMEM_LIMIT):
    return pltpu.CompilerParams(dimension_semantics=sem, vmem_limit_bytes=vmem)


def _dot(a, b):
    return jnp.dot(a, b, preferred_element_type=F32)


def _layernorm(x, g, b):
    mu = jnp.mean(x, axis=-1, keepdims=True)
    xc = x - mu
    var = jnp.mean(xc * xc, axis=-1, keepdims=True)
    return xc * lax.rsqrt(var + LN_EPS) * g + b


def _sigmoid(x):
    return 1.0 / (1.0 + jnp.exp(-x))


def _gelu(x):
    c = math.sqrt(2.0 / math.pi)
    return 0.5 * x * (1.0 + jnp.tanh(c * (x + 0.044715 * (x * x * x))))


def _silu(x):
    return x * _sigmoid(x)


def _softplus(x):
    return jnp.maximum(x, 0.0) + jnp.log1p(jnp.exp(-jnp.abs(x)))


def _neg_expm1(x):
    poly = x * (1.0 + x * (1.0 / 2) * (1.0 + x * (1.0 / 3) * (1.0 + x * (1.0 / 4) * (
        1.0 + x * (1.0 / 5) * (1.0 + x * (1.0 / 6) * (1.0 + x * (1.0 / 7)))))))
    return -jnp.where(x > -0.25, poly, jnp.exp(x) - 1.0)


def _pack_bf16_pair(lo, hi):
    lo_bits = pltpu.bitcast(lo.astype(BF16).astype(F32), U32) >> 16
    hi_bits = pltpu.bitcast(hi.astype(BF16).astype(F32), U32) & jnp.uint32(0xFFFF0000)
    return hi_bits | lo_bits


def _unpack_bf16_pair(w):
    lo = pltpu.bitcast(w << 16, F32)
    hi = pltpu.bitcast(w & jnp.uint32(0xFFFF0000), F32)
    return lo, hi


def _s5_prep_body(are_ref, aim_ref, ldt_ref, bre_ref, bim_ref,
                  abr_ref, abi_ref, bbr_ref, bbi_ref):
    a_re = are_ref[...]
    a_im = aim_ref[...]
    dt = jnp.exp(ldt_ref[...])
    mag = jnp.exp(a_re * dt)
    ab_r = mag * jnp.cos(a_im * dt)
    ab_i = mag * jnp.sin(a_im * dt)
    den = a_re * a_re + a_im * a_im
    nr = ab_r - 1.0
    cr = (nr * a_re + ab_i * a_im) / den
    ci = (ab_i * a_re - nr * a_im) / den
    b_re = bre_ref[...]
    b_im = bim_ref[...]
    abr_ref[...] = ab_r
    abi_ref[...] = ab_i
    bbr_ref[...] = cr * b_re - ci * b_im
    bbi_ref[...] = cr * b_im + ci * b_re


def _s5_prep(a_re, a_im, log_dt, b_re, b_im):
    g, n, c = b_re.shape
    wide = c * n
    bc = lambda v: jnp.broadcast_to(v[:, None, :], (g, c, n)).reshape(g, wide)
    are_x = bc(a_re)
    aim_x = bc(a_im)
    ldt_x = jnp.broadcast_to(log_dt[:, None], (g, wide))
    bre_x = jnp.transpose(b_re, (0, 2, 1)).reshape(g, wide)
    bim_x = jnp.transpose(b_im, (0, 2, 1)).reshape(g, wide)
    shp = jax.ShapeDtypeStruct((g, wide), F32)
    abr, abi, bbr, bbi = pl.pallas_call(
        _s5_prep_body, out_shape=(shp, shp, shp, shp), name="s5_prep",
    )(are_x, aim_x, ldt_x, bre_x, bim_x)
    a_r = abr[:, :n].reshape(1, g * n)
    a_i = abi[:, :n].reshape(1, g * n)
    return a_r, a_i, bbr.reshape(g, c, n), bbi.reshape(g, c, n)


def _block_diag(x):
    k, g, a, b = x.shape
    eye = jnp.eye(g, dtype=x.dtype)
    return jnp.einsum("kgab,gh->kgahb", x, eye).reshape(k, g * a, g * b)


def _inproj_body(x_ref, g_ref, b_ref, w_ref, bias_ref, xa_ref, z_ref, xn_scr):
    j = pl.program_id(1)

    @pl.when(j == 0)
    def _():
        xn_scr[...] = _layernorm(x_ref[...], g_ref[...], b_ref[...]).astype(BF16)

    z = _dot(xn_scr[...], w_ref[...]) + bias_ref[...]
    n_xa = D_RNN // IN_TILE_N

    @pl.when(j < n_xa)
    def _():
        xa_ref[...] = z

    @pl.when(j >= n_xa)
    def _():
        z_ref[...] = z.astype(BF16)


def _inproj(xflat, ln_g, ln_b, w_in_bf, b_in, tile):
    n1 = xflat.shape[0]
    n_xa = D_RNN // IN_TILE_N
    grid = (n1 // tile, N_IN // IN_TILE_N)
    return pl.pallas_call(
        _inproj_body,
        grid=grid,
        in_specs=[
            pl.BlockSpec((tile, D_MODEL), lambda i, j: (i, 0)),
            pl.BlockSpec((1, D_MODEL), lambda i, j: (0, 0)),
            pl.BlockSpec((1, D_MODEL), lambda i, j: (0, 0)),
            pl.BlockSpec((D_MODEL, IN_TILE_N), lambda i, j: (0, j)),
            pl.BlockSpec((1, IN_TILE_N), lambda i, j: (0, j)),
        ],
        out_specs=[
            pl.BlockSpec((tile, IN_TILE_N), lambda i, j: (i, jnp.minimum(j, n_xa - 1))),
            pl.BlockSpec((tile, IN_TILE_N), lambda i, j: (i, jnp.maximum(j - n_xa, 0))),
        ],
        out_shape=[
            jax.ShapeDtypeStruct((n1, D_RNN), F32),
            jax.ShapeDtypeStruct((n1, N_IN - D_RNN), BF16),
        ],
        scratch_shapes=[pltpu.VMEM((tile, D_MODEL), BF16)],
        compiler_params=_params(("arbitrary", "arbitrary")),
        name="in_proj",
    )(xflat, ln_g, ln_b, w_in_bf, b_in)


def _rg_gates(xc, wa_ref, wi_ref, ba, bi, sp):
    xcb = xc.astype(BF16)
    nblk = D_RNN // MXU_DIM
    r_pre = jnp.concatenate(
        [_dot(xcb[:, k * MXU_DIM:(k + 1) * MXU_DIM], wa_ref[k]) for k in range(nblk)], axis=1)
    i_pre = jnp.concatenate(
        [_dot(xcb[:, k * MXU_DIM:(k + 1) * MXU_DIM], wi_ref[k]) for k in range(nblk)], axis=1)
    r = _sigmoid(r_pre + ba)
    i = _sigmoid(i_pre + bi)
    log_a = (-LRU_C * r) * sp
    a = jnp.exp(log_a)
    u = jnp.sqrt(_neg_expm1(2.0 * log_a)) * (i * xc)
    return a, u


def _odd_rows(width):
    return lax.broadcasted_iota(I32, (SUBLANES, width), 0) >= NSEQ


def _mixer_chunk(rows, xa_ref, ya_ref, us_ref, ua_ref, gl_ref, w, s):
    (cw_ref, cb_ref, wa_ref, wi_ref, ba_ref, bi_ref, lam_ref, bdb_ref, cre_ref, cim_ref,
     d_ref, gluw_ref, glub_ref) = w
    (ext_scr, a_scr, u_scr, hs_scr, bu_scr, hst_scr, y_scr, tail_scr, hcar_scr, s5car_scr,
     cst_scr) = s
    emit = ua_ref is not None
    halo = NSEQ * CONV_W
    ngroups = rows // SUBLANES

    ext_scr[pl.ds(0, halo), :] = tail_scr[...]
    ext_scr[pl.ds(halo, rows), :] = xa_ref[...]
    xc = cb_ref[...] + cw_ref[pl.ds(CONV_W - 1, 1), :] * ext_scr[pl.ds(halo, rows), :]
    for j in range(1, CONV_W):
        xc = xc + cw_ref[pl.ds(CONV_W - 1 - j, 1), :] * ext_scr[pl.ds(halo - NSEQ * j, rows), :]
    tail_scr[...] = ext_scr[pl.ds(rows, halo), :]

    sp = _softplus(-lam_ref[...])
    a, u = _rg_gates(xc, wa_ref, wi_ref, ba_ref[...], bi_ref[...], sp)
    a_scr[pl.ds(0, rows), :] = a
    u_scr[pl.ds(0, rows), :] = u
    odd = _odd_rows(D_RNN)

    def rg_body(g, c):
        row = pl.multiple_of(g * SUBLANES, SUBLANES)
        a_v = a_scr[pl.ds(row, SUBLANES), :]
        u_v = u_scr[pl.ds(row, SUBLANES), :]
        hl = u_v + jnp.where(odd, a_v * pltpu.roll(u_v, NSEQ, 0), 0.0)
        p = jnp.where(odd, a_v * pltpu.roll(a_v, NSEQ, 0), a_v)
        hs_scr[pl.ds(row, SUBLANES), :] = hl + p * c
        q = jnp.where(odd, hl, pltpu.roll(hl, NSEQ, 0))
        pp = jnp.where(odd, p, pltpu.roll(p, NSEQ, 0))
        return q + pp * c

    hcar_scr[...] = lax.fori_loop(0, ngroups, rg_body, hcar_scr[...])
    if emit:
        ua_ref[...] = (hs_scr[pl.ds(0, rows), :] * _gelu(ya_ref[...].astype(F32))).astype(BF16)

    odd_s = _odd_rows(S5_BLOCK_STATE)
    for kb in range(S5_BLOCKS):
        lo = kb * S5_BLOCK_STATE
        ub = us_ref[:, kb * MXU_DIM:(kb + 1) * MXU_DIM]
        bu_scr[pl.ds(0, rows), :] = _dot(ub, bdb_ref[kb])
        aor, aoi, pr, pi, a2r, a2i = [cst_scr[i, :, lo:lo + S5_BLOCK_STATE] for i in range(6)]

        def s5_body(g, c, aor=aor, aoi=aoi, pr=pr, pi=pi, a2r=a2r, a2i=a2i):
            cr, ci = c
            row = pl.multiple_of(g * SUBLANES, SUBLANES)
            bur = bu_scr[pl.ds(row, SUBLANES), 0:S5_BLOCK_STATE]
            bui = bu_scr[pl.ds(row, SUBLANES), S5_BLOCK_STATE:2 * S5_BLOCK_STATE]
            sr = pltpu.roll(bur, NSEQ, 0)
            si = pltpu.roll(bui, NSEQ, 0)
            hlr = bur + aor * sr - aoi * si
            hli = bui + aor * si + aoi * sr
            if emit:
                hst_scr[pl.ds(row, SUBLANES), 0:S5_BLOCK_STATE] = hlr + pr * cr - pi * ci
                hst_scr[pl.ds(row, SUBLANES), S5_BLOCK_STATE:2 * S5_BLOCK_STATE] = (
                    hli + pr * ci + pi * cr)
            qr = jnp.where(odd_s, hlr, pltpu.roll(hlr, NSEQ, 0))
            qi = jnp.where(odd_s, hli, pltpu.roll(hli, NSEQ, 0))
            return qr + a2r * cr - a2i * ci, qi + a2r * ci + a2i * cr

        cr, ci = lax.fori_loop(
            0, ngroups, s5_body,
            (s5car_scr[0, :, lo:lo + S5_BLOCK_STATE], s5car_scr[1, :, lo:lo + S5_BLOCK_STATE]))
        s5car_scr[0, :, lo:lo + S5_BLOCK_STATE] = cr
        s5car_scr[1, :, lo:lo + S5_BLOCK_STATE] = ci
        if emit:
            hre = hst_scr[pl.ds(0, rows), 0:S5_BLOCK_STATE].astype(BF16)
            him = hst_scr[pl.ds(0, rows), S5_BLOCK_STATE:2 * S5_BLOCK_STATE].astype(BF16)
            y = _dot(hre, cre_ref[kb]) - _dot(him, cim_ref[kb])
            y = y + d_ref[:, kb * MXU_DIM:(kb + 1) * MXU_DIM] * ub.astype(F32)
            y_scr[pl.ds(0, rows), kb * MXU_DIM:(kb + 1) * MXU_DIM] = y

    if emit:
        g5 = _gelu(y_scr[pl.ds(0, rows), :])
        gate = _sigmoid(_dot(g5.astype(BF16), gluw_ref[...]) + glub_ref[...])
        gl_ref[...] = (g5 * gate).astype(BF16)


def _mixer_body(meta_rows, n_s, xa_ref, ya_ref, us_ref, xam_ref, usm_ref, ar_ref, ai_ref,
                h0_ref, cbuf_ref, s5r0_ref, s5i0_ref, *rest):
    w = rest[:13]
    (ua_ref, gl_ref, hout_ref, convout_ref, s5r_ref, s5i_ref,
     h1_ref, cnew_ref, s5r1_ref, s5i1_ref) = rest[13:23]
    s = rest[23:]
    y_scr, tail_scr, hcar_scr, s5car_scr, cst_scr = s[6], s[7], s[8], s[9], s[10]
    c = pl.program_id(0)
    rows = xa_ref.shape[0]
    nchunks = pl.num_programs(0) - 1

    @pl.when(c == nchunks)
    def _():
        head = lambda r: r.at[pl.ds(0, n_s)]
        _sample_step(head(xa_ref), head(ya_ref), head(us_ref), h0_ref, cbuf_ref, s5r0_ref,
                     s5i0_ref, ar_ref, ai_ref, w, head(ua_ref), head(gl_ref), h1_ref, cnew_ref,
                     s5r1_ref, s5i1_ref, head(y_scr))
        ua_ref[pl.ds(n_s, rows - n_s), :] = jnp.zeros((rows - n_s, D_RNN), BF16)
        gl_ref[pl.ds(n_s, rows - n_s), :] = jnp.zeros((rows - n_s, D_S5), BF16)

    @pl.when(c < nchunks)
    def _():
        _prompt_step(meta_rows, nchunks, xa_ref, ya_ref, us_ref, xam_ref, usm_ref, ar_ref, ai_ref,
                     w, ua_ref, gl_ref, hout_ref, convout_ref, s5r_ref, s5i_ref, s)


def _prompt_step(meta_rows, nchunks, xa_ref, ya_ref, us_ref, xam_ref, usm_ref, ar_ref, ai_ref,
                 w, ua_ref, gl_ref, hout_ref, convout_ref, s5r_ref, s5i_ref, s):
    tail_scr, hcar_scr, s5car_scr, cst_scr = s[7], s[8], s[9], s[10]
    c = pl.program_id(0)
    rows = xa_ref.shape[0]

    @pl.when(c == 0)
    def _():
        odd = _odd_rows(S5_STATE)
        ar = jnp.broadcast_to(ar_ref[...], (SUBLANES, S5_STATE))
        ai = jnp.broadcast_to(ai_ref[...], (SUBLANES, S5_STATE))
        a2r = ar * ar - ai * ai
        a2i = 2.0 * (ar * ai)
        cst_scr[0] = jnp.where(odd, ar, 0.0)
        cst_scr[1] = jnp.where(odd, ai, 0.0)
        cst_scr[2] = jnp.where(odd, a2r, ar)
        cst_scr[3] = jnp.where(odd, a2i, ai)
        cst_scr[4] = a2r
        cst_scr[5] = a2i
        tail_scr[...] = jnp.zeros_like(tail_scr)
        hcar_scr[...] = jnp.zeros_like(hcar_scr)
        s5car_scr[...] = jnp.zeros_like(s5car_scr)
        _mixer_chunk(meta_rows, xam_ref, None, usm_ref, None, None, w, s)

    _mixer_chunk(rows, xa_ref, ya_ref, us_ref, ua_ref, gl_ref, w, s)

    @pl.when(c == nchunks - 1)
    def _():
        hout_ref[...] = hcar_scr[pl.ds(NSEQ, NSEQ), :]
        convout_ref[...] = tail_scr[pl.ds(NSEQ, NSEQ * (CONV_W - 1)), :]
        s5r_ref[...] = s5car_scr[0, pl.ds(NSEQ, NSEQ), :]
        s5i_ref[...] = s5car_scr[1, pl.ds(NSEQ, NSEQ), :]


def _mixer_weight_specs(nidx):
    z = (0,) * nidx if nidx else ()
    c2 = lambda *_: (0, 0)
    c3 = lambda *_: (0, 0, 0)
    nblk = D_RNN // MXU_DIM
    return [
        pl.BlockSpec((CONV_W, D_RNN), c2),
        pl.BlockSpec((1, D_RNN), c2),
        pl.BlockSpec((nblk, MXU_DIM, MXU_DIM), c3),
        pl.BlockSpec((nblk, MXU_DIM, MXU_DIM), c3),
        pl.BlockSpec((1, D_RNN), c2),
        pl.BlockSpec((1, D_RNN), c2),
        pl.BlockSpec((1, D_RNN), c2),
        pl.BlockSpec((S5_BLOCKS, MXU_DIM, 2 * S5_BLOCK_STATE), c3),
        pl.BlockSpec((S5_BLOCKS, S5_BLOCK_STATE, MXU_DIM), c3),
        pl.BlockSpec((S5_BLOCKS, S5_BLOCK_STATE, MXU_DIM), c3),
        pl.BlockSpec((1, D_S5), c2),
        pl.BlockSpec((D_S5, D_S5), c2),
        pl.BlockSpec((1, D_S5), c2),
    ]


def _mixer(xa, z16, a_r, a_i, h0, cbuf, s5r0, s5i0, weights, n_prompt, n_s, meta_row0, meta_rows,
           chunk):
    nchunks = n_prompt // chunk
    n1 = xa.shape[0]
    assert n1 == n_prompt + chunk and n_s <= chunk
    meta_blk = meta_row0 // meta_rows
    halo = NSEQ * CONV_W
    c2 = lambda c: (0, 0)
    in_specs = [
        pl.BlockSpec((chunk, D_RNN), lambda c: (c, 0)),
        pl.BlockSpec((chunk, D_RNN), lambda c: (c, 0)),
        pl.BlockSpec((chunk, D_S5), lambda c: (c, 1)),
        pl.BlockSpec((meta_rows, D_RNN), lambda c: (meta_blk, 0)),
        pl.BlockSpec((meta_rows, D_S5), lambda c: (meta_blk, 1)),
        pl.BlockSpec((1, S5_STATE), c2),
        pl.BlockSpec((1, S5_STATE), c2),
        pl.BlockSpec((n_s, D_RNN), c2),
        pl.BlockSpec((n_s, (CONV_W - 1) * D_RNN), c2),
        pl.BlockSpec((n_s, S5_STATE), c2),
        pl.BlockSpec((n_s, S5_STATE), c2),
    ] + _mixer_weight_specs(1)
    out_specs = [
        pl.BlockSpec((chunk, D_RNN), lambda c: (c, 0)),
        pl.BlockSpec((chunk, D_S5), lambda c: (c, 0)),
        pl.BlockSpec((NSEQ, D_RNN), c2),
        pl.BlockSpec((NSEQ * (CONV_W - 1), D_RNN), c2),
        pl.BlockSpec((NSEQ, S5_STATE), c2),
        pl.BlockSpec((NSEQ, S5_STATE), c2),
        pl.BlockSpec((n_s, D_RNN), c2),
        pl.BlockSpec((n_s, (CONV_W - 1) * D_RNN), c2),
        pl.BlockSpec((n_s, S5_STATE), c2),
        pl.BlockSpec((n_s, S5_STATE), c2),
    ]
    out_shape = [
        jax.ShapeDtypeStruct((n1, D_RNN), BF16),
        jax.ShapeDtypeStruct((n1, D_S5), BF16),
        jax.ShapeDtypeStruct((NSEQ, D_RNN), F32),
        jax.ShapeDtypeStruct((NSEQ * (CONV_W - 1), D_RNN), F32),
        jax.ShapeDtypeStruct((NSEQ, S5_STATE), F32),
        jax.ShapeDtypeStruct((NSEQ, S5_STATE), F32),
        jax.ShapeDtypeStruct((n_s, D_RNN), F32),
        jax.ShapeDtypeStruct((n_s, (CONV_W - 1) * D_RNN), F32),
        jax.ShapeDtypeStruct((n_s, S5_STATE), F32),
        jax.ShapeDtypeStruct((n_s, S5_STATE), F32),
    ]
    scratch = [
        pltpu.VMEM((chunk + halo, D_RNN), F32),
        pltpu.VMEM((chunk, D_RNN), F32),
        pltpu.VMEM((chunk, D_RNN), F32),
        pltpu.VMEM((chunk, D_RNN), F32),
        pltpu.VMEM((chunk, 2 * S5_BLOCK_STATE), F32),
        pltpu.VMEM((chunk, 2 * S5_BLOCK_STATE), F32),
        pltpu.VMEM((chunk, D_S5), F32),
        pltpu.VMEM((halo, D_RNN), F32),
        pltpu.VMEM((SUBLANES, D_RNN), F32),
        pltpu.VMEM((2, SUBLANES, S5_STATE), F32),
        pltpu.VMEM((6, SUBLANES, S5_STATE), F32),
    ]
    return pl.pallas_call(
        functools.partial(_mixer_body, meta_rows, n_s),
        grid=(nchunks + 1,),
        in_specs=in_specs,
        out_specs=out_specs,
        out_shape=out_shape,
        scratch_shapes=scratch,
        compiler_params=_params(("arbitrary",)),
        name="mixer",
    )(xa, z16, z16, xa, z16, a_r, a_i, h0, cbuf, s5r0, s5i0, *weights)


def _sample_step(xa_ref, ya_ref, us_ref, h0_ref, cbuf_ref, s5r0_ref, s5i0_ref, ar_ref, ai_ref,
                 w, ua_ref, gl_ref, h1_ref, cnew_ref, s5r1_ref, s5i1_ref, y_scr):
    (cw_ref, cb_ref, wa_ref, wi_ref, ba_ref, bi_ref, lam_ref, bdb_ref, cre_ref, cim_ref,
     d_ref, gluw_ref, glub_ref) = w
    xa = xa_ref[...]
    xc = cb_ref[...] + cw_ref[pl.ds(CONV_W - 1, 1), :] * xa
    for k in range(CONV_W - 1):
        xc = xc + cw_ref[pl.ds(k, 1), :] * cbuf_ref[:, k * D_RNN:(k + 1) * D_RNN]
    for k in range(CONV_W - 2):
        cnew_ref[:, k * D_RNN:(k + 1) * D_RNN] = cbuf_ref[:, (k + 1) * D_RNN:(k + 2) * D_RNN]
    cnew_ref[:, (CONV_W - 2) * D_RNN:(CONV_W - 1) * D_RNN] = xa

    sp = _softplus(-lam_ref[...])
    a, u = _rg_gates(xc, wa_ref, wi_ref, ba_ref[...], bi_ref[...], sp)
    h1 = a * h0_ref[...] + u
    h1_ref[...] = h1
    ua_ref[...] = (h1 * _gelu(ya_ref[...].astype(F32))).astype(BF16)

    for kb in range(S5_BLOCKS):
        lo = kb * S5_BLOCK_STATE
        ub = us_ref[:, kb * MXU_DIM:(kb + 1) * MXU_DIM]
        bu = _dot(ub, bdb_ref[kb])
        ar = ar_ref[:, lo:lo + S5_BLOCK_STATE]
        ai = ai_ref[:, lo:lo + S5_BLOCK_STATE]
        h0r = s5r0_ref[:, lo:lo + S5_BLOCK_STATE]
        h0i = s5i0_ref[:, lo:lo + S5_BLOCK_STATE]
        hr = bu[:, 0:S5_BLOCK_STATE] + ar * h0r - ai * h0i
        hi = bu[:, S5_BLOCK_STATE:2 * S5_BLOCK_STATE] + ar * h0i + ai * h0r
        s5r1_ref[:, lo:lo + S5_BLOCK_STATE] = hr
        s5i1_ref[:, lo:lo + S5_BLOCK_STATE] = hi
        y = _dot(hr.astype(BF16), cre_ref[kb]) - _dot(hi.astype(BF16), cim_ref[kb])
        y_scr[:, kb * MXU_DIM:(kb + 1) * MXU_DIM] = (
            y + d_ref[:, kb * MXU_DIM:(kb + 1) * MXU_DIM] * ub.astype(F32))

    g5 = _gelu(y_scr[...])
    gate = _sigmoid(_dot(g5.astype(BF16), gluw_ref[...]) + glub_ref[...])
    gl_ref[...] = (g5 * gate).astype(BF16)


def _col_min(x):
    return jnp.min(x, axis=0, keepdims=True)


def _col_max(x):
    return jnp.max(x, axis=0, keepdims=True)


def _col_sum(x):
    return jnp.sum(x, axis=0, keepdims=True)


def _route(sel, scores, tm):
    neg = -jnp.inf
    iota = lax.broadcasted_iota(I32, (GROUP_SIZE, tm), 0)
    sel_b = [sel[g * GROUP_SIZE:(g + 1) * GROUP_SIZE, :] for g in range(N_GROUPS)]
    sc_b = [scores[g * GROUP_SIZE:(g + 1) * GROUP_SIZE, :] for g in range(N_GROUPS)]

    iota_g = lax.broadcasted_iota(I32, (N_GROUPS, tm), 0)
    gs = jnp.zeros((N_GROUPS, tm), F32)
    for g in range(N_GROUPS):
        b = sel_b[g]
        m1 = _col_max(b)
        i1 = _col_min(jnp.where(b == m1, iota, GROUP_SIZE))
        m2 = _col_max(jnp.where(iota == i1, neg, b))
        gs = jnp.where(iota_g == g, m1 + m2, gs)

    keep = jnp.zeros((N_GROUPS, tm), I32)
    work = gs
    for _ in range(TOPK_GROUPS):
        m = _col_max(work)
        idx = _col_min(jnp.where(work == m, iota_g, N_GROUPS))
        hit = iota_g == idx
        keep = jnp.where(hit, 1, keep)
        work = jnp.where(hit, neg, work)

    cand = [jnp.where(keep[g:g + 1, :] > 0, sel_b[g], neg) for g in range(N_GROUPS)]
    ids, vals = [], []
    for _ in range(TOP_K):
        m = _col_max(cand[0])
        for g in range(1, N_GROUPS):
            m = jnp.maximum(m, _col_max(cand[g]))
        idx = _col_min(jnp.where(cand[0] == m, iota, N_EXPERTS))
        for g in range(1, N_GROUPS):
            idx = jnp.minimum(
                idx, _col_min(jnp.where(cand[g] == m, iota + g * GROUP_SIZE, N_EXPERTS)))
        val = jnp.zeros((1, tm), F32)
        for g in range(N_GROUPS):
            hit = (iota + g * GROUP_SIZE) == idx
            val = val + _col_sum(jnp.where(hit, sc_b[g], 0.0))
            cand[g] = jnp.where(hit, neg, cand[g])
        ids.append(idx)
        vals.append(val)
    return ids, vals


def _post_body(ua_ref, gl_ref, ga_ref, gb_ref, x_ref, ling_ref, linb_ref, pa_ref, pb_ref,
               wo_ref, l1g_ref, l1b_ref, x1_ref):
    branch_a = _dot(ua_ref[...], pa_ref[...])
    branch_b = _dot(gl_ref[...], pb_ref[...])
    merged = (_sigmoid(ga_ref[...].astype(F32)) * branch_a
              + _sigmoid(gb_ref[...].astype(F32)) * branch_b)
    o = _dot(merged.astype(BF16), wo_ref[...])
    xn = _layernorm(x_ref[...], ling_ref[...], linb_ref[...])
    x1_ref[...] = _layernorm(ALPHA * xn + o, l1g_ref[...], l1b_ref[...])


def _post_mixer(ua, gl, z16, xflat, ln_in_g, ln_in_b, proj_a, proj_b, w_o, ln1_g, ln1_b,
                n_tok, tile):
    c2 = lambda i: (0, 0)
    in_specs = [
        pl.BlockSpec((tile, D_RNN), lambda i: (i, 0)),
        pl.BlockSpec((tile, D_S5), lambda i: (i, 0)),
        pl.BlockSpec((tile, D_MODEL), lambda i: (i, 1)),
        pl.BlockSpec((tile, D_MODEL), lambda i: (i, 2)),
        pl.BlockSpec((tile, D_MODEL), lambda i: (i, 0)),
        pl.BlockSpec((1, D_MODEL), c2),
        pl.BlockSpec((1, D_MODEL), c2),
        pl.BlockSpec((D_RNN, D_MODEL), c2),
        pl.BlockSpec((D_S5, D_MODEL), c2),
        pl.BlockSpec((D_MODEL, D_MODEL), c2),
        pl.BlockSpec((1, D_MODEL), c2),
        pl.BlockSpec((1, D_MODEL), c2),
    ]
    return pl.pallas_call(
        _post_body,
        grid=(n_tok // tile,),
        in_specs=in_specs,
        out_specs=pl.BlockSpec((tile, D_MODEL), lambda i: (i, 0)),
        out_shape=jax.ShapeDtypeStruct((n_tok, D_MODEL), F32),
        compiler_params=_params(("arbitrary",)),
        name="post_mixer",
    )(ua, gl, z16, z16, xflat, ln_in_g, ln_in_b, proj_a, proj_b, w_o, ln1_g, ln1_b)


def _router_body(x1_ref, wrh_ref, wrl_ref, rb_ref,
                 x1p_ref, eidx_ref, rank_ref, gatet_ref, cnt_ref, cnt_scr):
    i = pl.program_id(0)
    tm = x1_ref.shape[0]

    @pl.when(i == 0)
    def _():
        cnt_scr[...] = jnp.zeros_like(cnt_scr)

    x1 = x1_ref[...]
    half = D_MODEL // 2
    x1p_ref[...] = _pack_bf16_pair(x1[:, :half], x1[:, half:])

    x_hi = x1.astype(BF16)
    x_lo = (x1 - x_hi.astype(F32)).astype(BF16)
    nt = (((1,), (1,)), ((), ()))
    dg = lambda a, b: lax.dot_general(a, b, nt, preferred_element_type=F32)
    logits = dg(wrh_ref[...], x_hi) + dg(wrh_ref[...], x_lo) + dg(wrl_ref[...], x_hi)
    scores = _sigmoid(logits)
    sel = scores + rb_ref[...]
    ids, vals = _route(sel, scores, tm)

    total = vals[0]
    for v in vals[1:]:
        total = total + v
    iota_k = lax.broadcasted_iota(I32, (TOP_K, tm), 0)
    iota_e = lax.broadcasted_iota(I32, (N_EXPERTS, tm), 0)
    eidx = jnp.zeros((TOP_K, tm), I32)
    gates = jnp.zeros((TOP_K, tm), F32)
    selm = jnp.zeros((N_EXPERTS, tm), F32)
    for k in range(TOP_K):
        eidx = jnp.where(iota_k == k, ids[k], eidx)
        gates = jnp.where(iota_k == k, vals[k] / total * ROUTED_SCALE, gates)
        selm = jnp.where(iota_e == ids[k], 1.0, selm)
    eidx_ref[...] = eidx

    r_i = lax.broadcasted_iota(I32, (tm, tm), 0)
    c_i = lax.broadcasted_iota(I32, (tm, tm), 1)
    upper = jnp.where(r_i < c_i, 1.0, 0.0).astype(BF16)
    rank_all = _dot(selm.astype(BF16), upper) + cnt_scr[...]
    rank = jnp.zeros((TOP_K, tm), F32)
    for k in range(TOP_K):
        rk = _col_sum(jnp.where(iota_e == ids[k], rank_all, 0.0))
        rank = jnp.where(iota_k == k, rk, rank)
    rank_ref[...] = rank.astype(I32)
    cnt_scr[...] = cnt_scr[...] + jnp.sum(selm, axis=1, keepdims=True)
    cnt_ref[...] = cnt_scr[...]

    gpad = jnp.concatenate([gates, jnp.zeros((LANES - TOP_K, tm), F32)], axis=0)
    gatet_ref[...] = gpad.T


def _router(x1, wr_hi, wr_lo, rbias, tile):
    n_tok = x1.shape[0]
    c2 = lambda i: (0, 0)
    half = D_MODEL // 2
    return pl.pallas_call(
        _router_body,
        grid=(n_tok // tile,),
        in_specs=[
            pl.BlockSpec((tile, D_MODEL), lambda i: (i, 0)),
            pl.BlockSpec((N_EXPERTS, D_MODEL), c2),
            pl.BlockSpec((N_EXPERTS, D_MODEL), c2),
            pl.BlockSpec((N_EXPERTS, 1), c2),
        ],
        out_specs=[
            pl.BlockSpec((tile, half), lambda i: (i, 0)),
            pl.BlockSpec((TOP_K, tile), lambda i: (0, i)),
            pl.BlockSpec((TOP_K, tile), lambda i: (0, i)),
            pl.BlockSpec((tile, LANES), lambda i: (i, 0)),
            pl.BlockSpec((N_EXPERTS, 1), c2),
        ],
        out_shape=[
            jax.ShapeDtypeStruct((n_tok, half), U32),
            jax.ShapeDtypeStruct((TOP_K, n_tok), I32),
            jax.ShapeDtypeStruct((TOP_K, n_tok), I32),
            jax.ShapeDtypeStruct((n_tok, LANES), F32),
            jax.ShapeDtypeStruct((N_EXPERTS, 1), F32),
        ],
        scratch_shapes=[pltpu.VMEM((N_EXPERTS, 1), F32)],
        compiler_params=_params(("arbitrary",)),
        name="router",
    )(x1, wr_hi, wr_lo, rbias)


def _positions_body(eidx_ref, rank_ref, cnt_ref, pos_ref, texp_ref, info_ref):
    cnt = cnt_ref[...]
    padded = jnp.floor((cnt + (MOE_TILE - 1)) * (1.0 / MOE_TILE)) * MOE_TILE
    r_i = lax.broadcasted_iota(I32, (N_EXPERTS, N_EXPERTS), 0)
    c_i = lax.broadcasted_iota(I32, (N_EXPERTS, N_EXPERTS), 1)
    eye = r_i == c_i
    as_row = lambda col: jnp.sum(jnp.where(eye, col, 0.0), axis=0, keepdims=True)
    padded_row = as_row(padded)
    base = jnp.sum(jnp.where(c_i < r_i, padded_row, 0.0), axis=1, keepdims=True)
    end = base + padded

    eidx = eidx_ref[...]
    pos = rank_ref[...]
    base_i = base.astype(I32)
    for e in range(N_EXPERTS):
        pos = pos + jnp.where(eidx == e, base_i[e:e + 1, :], 0)
    pos_ref[...] = pos

    ntp = texp_ref.shape[1]
    t_row = lax.broadcasted_iota(I32, (1, ntp), 1).astype(F32) * MOE_TILE
    texp = jnp.sum(jnp.where(end <= t_row, 1, 0), axis=0, keepdims=True)
    texp_ref[...] = jnp.minimum(texp, N_EXPERTS - 1).astype(I32)

    row = lax.broadcasted_iota(I32, (SUBLANES, LANES), 0)
    lane = lax.broadcasted_iota(I32, (SUBLANES, LANES), 1)
    pad_lanes = lambda r: jnp.concatenate(
        [r, jnp.zeros((1, LANES - N_EXPERTS), F32)], axis=1)
    ntiles = jnp.sum(padded, axis=0, keepdims=True) * (1.0 / MOE_TILE)
    info = jnp.where(row == 0, pad_lanes(as_row(cnt)), 0.0)
    info = jnp.where(row == 1, pad_lanes(as_row(base)), info)
    info = jnp.where(row == 2, pad_lanes(as_row(end)), info)
    info = jnp.where((row == 3) & (lane == 0), ntiles, info)
    info_ref[...] = info.astype(I32)


def _positions(eidx, rank, cnt, n_tiles_max):
    n_tok = eidx.shape[1]
    ntp = -(-n_tiles_max // LANES) * LANES
    return pl.pallas_call(
        _positions_body,
        out_shape=[
            jax.ShapeDtypeStruct((TOP_K, n_tok), I32),
            jax.ShapeDtypeStruct((1, ntp), I32),
            jax.ShapeDtypeStruct((SUBLANES, LANES), I32),
        ],
        compiler_params=pltpu.CompilerParams(vmem_limit_bytes=VMEM_LIMIT),
        name="positions",
    )(eidx, rank, cnt)


def _row_copy(src_hbm, src_row, dst_hbm, dst_row, sem):
    return pltpu.make_async_copy(
        src_hbm.at[pl.ds(src_row, 1)], dst_hbm.at[pl.ds(dst_row, 1)], sem)


def _swiglu_packed(xp, w1_ref, w3_ref, w2_ref):
    half = D_MODEL // 2
    lo, hi = _unpack_bf16_pair(xp)
    lo = lo.astype(BF16)
    hi = hi.astype(BF16)
    h1 = _dot(lo, w1_ref[0:half, :]) + _dot(hi, w1_ref[half:D_MODEL, :])
    h3 = _dot(lo, w3_ref[0:half, :]) + _dot(hi, w3_ref[half:D_MODEL, :])
    h = (_silu(h1) * h3).astype(BF16)
    return _dot(h, w2_ref[...])


def _dispatch_body(info_ref, pos_ref, xp_hbm, zero_hbm, xp_ref, w1_ref, w3_ref, w2_ref,
                   xs_hbm, sh_ref, sem):
    i = pl.program_id(0)
    tm = xp_ref.shape[0]

    def issue(t, carry):
        for k in range(TOP_K):
            _row_copy(xp_hbm, i * tm + t, xs_hbm, pos_ref[k, t], sem).start()
        return carry

    lax.fori_loop(0, tm, issue, 0)
    sh_ref[...] = _swiglu_packed(xp_ref[...], w1_ref, w3_ref, w2_ref)

    def drain(t, carry):
        for k in range(TOP_K):
            _row_copy(xp_hbm, 0, xs_hbm, 0, sem).wait()
        return carry

    lax.fori_loop(0, tm, drain, 0)

    @pl.when(i == pl.num_programs(0) - 1)
    def _():
        def per_expert(e, carry):
            start = info_ref[1, e] + info_ref[0, e]
            stop = info_ref[2, e]

            def fill(r, c):
                _row_copy(zero_hbm, 0, xs_hbm, r, sem).start()
                return c

            def fill_wait(r, c):
                _row_copy(zero_hbm, 0, xs_hbm, 0, sem).wait()
                return c

            lax.fori_loop(start, stop, fill, 0)
            lax.fori_loop(start, stop, fill_wait, 0)
            return carry

        lax.fori_loop(0, N_EXPERTS, per_expert, 0)


def _dispatch(info, pos, x1p, sh_w1, sh_w3, sh_w2, rows_sorted, tile):
    n_tok, half = x1p.shape
    zero_rows = jnp.zeros((SUBLANES, half), U32)
    c2 = lambda i, info: (0, 0)
    grid_spec = pltpu.PrefetchScalarGridSpec(
        num_scalar_prefetch=1,
        grid=(n_tok // tile,),
        in_specs=[
            pl.BlockSpec((TOP_K, tile), lambda i, info: (0, i), memory_space=pltpu.SMEM),
            pl.BlockSpec(memory_space=pl.ANY),
            pl.BlockSpec(memory_space=pl.ANY),
            pl.BlockSpec((tile, half), lambda i, info: (i, 0)),
            pl.BlockSpec((D_MODEL, D_EXPERT), c2),
            pl.BlockSpec((D_MODEL, D_EXPERT), c2),
            pl.BlockSpec((D_EXPERT, D_MODEL), c2),
        ],
        out_specs=[
            pl.BlockSpec(memory_space=pl.ANY),
            pl.BlockSpec((tile, D_MODEL), lambda i, info: (i, 0)),
        ],
        scratch_shapes=[pltpu.SemaphoreType.DMA],
    )
    return pl.pallas_call(
        _dispatch_body,
        grid_spec=grid_spec,
        out_shape=[
            jax.ShapeDtypeStruct((rows_sorted, half), U32),
            jax.ShapeDtypeStruct((n_tok, D_MODEL), F32),
        ],
        compiler_params=_params(("arbitrary",)),
        name="dispatch",
    )(info, pos, x1p, zero_rows, x1p, sh_w1, sh_w3, sh_w2)


def _moe_body(texp_ref, info_ref, xs_ref, w1_ref, w3_ref, w2_ref, ys_ref,
              w1b_scr, w3b_scr, w2b_scr):
    t = pl.program_id(0)
    ntiles = info_ref[3, 0]
    tt = jnp.minimum(t, ntiles - 1)
    prev = jnp.maximum(tt - 1, 0)
    fresh = (t == 0) | (texp_ref[0, tt] != texp_ref[0, prev])
    valid = t < ntiles

    @pl.when(valid & fresh)
    def _():
        w1b_scr[...] = w1_ref[0].astype(BF16)
        w3b_scr[...] = w3_ref[0].astype(BF16)
        w2b_scr[...] = w2_ref[0].astype(BF16)

    @pl.when(valid)
    def _():
        y = _swiglu_packed(xs_ref[...], w1b_scr, w3b_scr, w2b_scr)
        half = D_MODEL // 2
        ys_ref[...] = _pack_bf16_pair(y[:, :half], y[:, half:])


def _moe(texp, info, xs, ex_w1, ex_w3, ex_w2, n_tiles_max):
    half = D_MODEL // 2

    def tile_idx(t, texp, info):
        return jnp.minimum(t, info[3, 0] - 1)

    def w_idx(t, texp, info):
        return (texp[0, tile_idx(t, texp, info)], 0, 0)

    grid_spec = pltpu.PrefetchScalarGridSpec(
        num_scalar_prefetch=2,
        grid=(n_tiles_max,),
        in_specs=[
            pl.BlockSpec((MOE_TILE, half), lambda t, texp, info: (tile_idx(t, texp, info), 0)),
            pl.BlockSpec((1, D_MODEL, D_EXPERT), w_idx),
            pl.BlockSpec((1, D_MODEL, D_EXPERT), w_idx),
            pl.BlockSpec((1, D_EXPERT, D_MODEL), w_idx),
        ],
        out_specs=pl.BlockSpec(
            (MOE_TILE, half), lambda t, texp, info: (tile_idx(t, texp, info), 0)),
        scratch_shapes=[
            pltpu.VMEM((D_MODEL, D_EXPERT), BF16),
            pltpu.VMEM((D_MODEL, D_EXPERT), BF16),
            pltpu.VMEM((D_EXPERT, D_MODEL), BF16),
        ],
    )
    return pl.pallas_call(
        _moe_body,
        grid_spec=grid_spec,
        out_shape=jax.ShapeDtypeStruct(xs.shape, U32),
        compiler_params=_params(("arbitrary",)),
        name="moe_experts",
    )(texp, info, xs, ex_w1, ex_w3, ex_w2)


def _combine_body(pos_ref, ys_hbm, gate_ref, x1_ref, sh_ref, g_ref, b_ref, out_ref,
                  buf_scr, sem):
    tm = x1_ref.shape[0]

    def issue(t, carry):
        for k in range(TOP_K):
            pltpu.make_async_copy(
                ys_hbm.at[pl.ds(pos_ref[k, t], 1)], buf_scr.at[k, pl.ds(t, 1)], sem).start()
        return carry

    lax.fori_loop(0, tm, issue, 0)

    def drain(t, carry):
        for k in range(TOP_K):
            pltpu.make_async_copy(
                ys_hbm.at[pl.ds(0, 1)], buf_scr.at[k, pl.ds(0, 1)], sem).wait()
        return carry

    lax.fori_loop(0, tm, drain, 0)

    gates = gate_ref[...]
    half = D_MODEL // 2
    acc = sh_ref[...]
    for k in range(TOP_K):
        lo, hi = _unpack_bf16_pair(buf_scr[k])
        acc = acc + gates[:, k:k + 1] * jnp.concatenate([lo, hi], axis=1)
    out_ref[...] = _layernorm(ALPHA * x1_ref[...] + acc, g_ref[...], b_ref[...])


def _combine(pos, ys, gate_t, x1, sh_out, ln2_g, ln2_b, tile):
    n_tok = x1.shape[0]
    half = D_MODEL // 2
    c2 = lambda i: (0, 0)
    return pl.pallas_call(
        _combine_body,
        grid=(n_tok // tile,),
        in_specs=[
            pl.BlockSpec((TOP_K, tile), lambda i: (0, i), memory_space=pltpu.SMEM),
            pl.BlockSpec(memory_space=pl.ANY),
            pl.BlockSpec((tile, LANES), lambda i: (i, 0)),
            pl.BlockSpec((tile, D_MODEL), lambda i: (i, 0)),
            pl.BlockSpec((tile, D_MODEL), lambda i: (i, 0)),
            pl.BlockSpec((1, D_MODEL), c2),
            pl.BlockSpec((1, D_MODEL), c2),
        ],
        out_specs=pl.BlockSpec((tile, D_MODEL), lambda i: (i, 0)),
        out_shape=jax.ShapeDtypeStruct((n_tok, D_MODEL), F32),
        scratch_shapes=[
            pltpu.VMEM((TOP_K, tile, half), U32),
            pltpu.SemaphoreType.DMA,
        ],
        compiler_params=_params(("arbitrary",)),
        name="combine",
    )(pos, ys, gate_t, x1, sh_out, ln2_g, ln2_b)


def _pick_tile(n, cap, mult):
    best = mult
    for t in range(mult, cap + 1, mult):
        if n % t == 0:
            best = t
    assert n % best == 0
    return best


def kernel(x_prompt, x_sample, state_rglru_h, state_conv, state_s5_re, state_s5_im, meta_tokens, ln_in_g, ln_in_b, w_in, b_in, conv_w, conv_b, rg_wa, rg_ba, rg_wi, rg_bi, rg_lambda, s5_a_re, s5_a_im, s5_b_re, s5_b_im, s5_c_re, s5_c_im, s5_d, s5_log_dt, glu_w, glu_b, proj_a, proj_b, w_o, ln1_g, ln1_b, router_w, router_bias, ex_w1, ex_w3, ex_w2, sh_w1, sh_w3, sh_w2, ln2_g, ln2_b):
    bp, seq, d = x_prompt.shape
    n_s = x_sample.shape[0]
    assert bp == NSEQ and d == D_MODEL and x_sample.shape[1] == 1
    assert w_in.shape[0] == DEPTH
    n_prompt = bp * seq
    n_tok = n_prompt + n_s
    meta_rows = NSEQ * N_META
    n1 = n_tok + 2 * meta_rows
    row = lambda v: v.reshape(1, -1)

    xflat = jnp.concatenate([
        jnp.transpose(x_prompt, (1, 0, 2)).reshape(n_prompt, d),
        x_sample.reshape(n_s, d),
        jnp.repeat(meta_tokens, NSEQ, axis=0),
        jnp.zeros((meta_rows, d), F32),
    ], axis=0)

    tile1 = _pick_tile(n1, 1056, 2 * SUBLANES)
    xa, z16 = _inproj(xflat, row(ln_in_g), row(ln_in_b), w_in[0].astype(BF16), b_in, tile1)

    a_r, a_i, bb_r, bb_i = _s5_prep(s5_a_re[0], s5_a_im[0], s5_log_dt[0], s5_b_re[0], s5_b_im[0])
    bg = S5_BLOCK_GROUPS
    bd_b = jnp.concatenate([
        _block_diag(bb_r.reshape(S5_BLOCKS, bg, S5_CH, S5_N)),
        _block_diag(bb_i.reshape(S5_BLOCKS, bg, S5_CH, S5_N)),
    ], axis=2).astype(BF16)
    c_t = lambda c: jnp.transpose(c[0], (0, 2, 1)).reshape(S5_BLOCKS, bg, S5_N, S5_CH)
    bd_cre = _block_diag(c_t(s5_c_re)).astype(BF16)
    bd_cim = _block_diag(c_t(s5_c_im)).astype(BF16)
    heads_per_blk = RG_HEADS // (D_RNN // MXU_DIM)
    rg_blk = lambda w: _block_diag(
        w[0].reshape(D_RNN // MXU_DIM, heads_per_blk, D_RNN // RG_HEADS, D_RNN // RG_HEADS)
    ).astype(BF16)
    weights = (conv_w[0], conv_b, rg_blk(rg_wa), rg_blk(rg_wi), rg_ba, rg_bi, rg_lambda,
               bd_b, bd_cre, bd_cim, row(s5_d[0]), glu_w[0].astype(BF16), glu_b)

    chunk = n_s + 2 * meta_rows
    assert n_prompt % chunk == 0
    ua, gl, p_h, p_conv, p_s5r, p_s5i, s_h, s_conv, s_s5r, s_s5i = _mixer(
        xa, z16, a_r, a_i, state_rglru_h[0],
        state_conv[0].reshape(n_s, (CONV_W - 1) * D_RNN),
        state_s5_re[0].reshape(n_s, S5_STATE), state_s5_im[0].reshape(n_s, S5_STATE),
        weights, n_prompt, n_s, n_tok, meta_rows, chunk)

    wr_t = jnp.transpose(router_w[0])
    wr_hi = wr_t.astype(BF16)
    wr_lo = (wr_t - wr_hi.astype(F32)).astype(BF16)
    tile3 = _pick_tile(n_tok, POST_TILE, 2 * SUBLANES)
    x1 = _post_mixer(
        ua, gl, z16, xflat, row(ln_in_g), row(ln_in_b), proj_a[0].astype(BF16),
        proj_b[0].astype(BF16), w_o[0].astype(BF16), ln1_g, ln1_b, n_tok, tile3)
    tile4 = _pick_tile(n_tok, TOKEN_TILE, LANES)
    x1p, eidx, rank, gate_t, cnt = _router(
        x1, wr_hi, wr_lo, router_bias[0].reshape(N_EXPERTS, 1), tile4)

    rows_max = n_tok * TOP_K + N_EXPERTS * (MOE_TILE - 1)
    n_tiles_max = -(-rows_max // MOE_TILE)
    pos, texp, info = _positions(eidx, rank, cnt, n_tiles_max)
    xs, sh_out = _dispatch(info, pos, x1p, sh_w1[0].astype(BF16), sh_w3[0].astype(BF16),
                           sh_w2[0].astype(BF16), n_tiles_max * MOE_TILE, tile4)
    ys = _moe(texp, info, xs, ex_w1[0], ex_w3[0], ex_w2[0], n_tiles_max)
    y = _combine(pos, ys, gate_t, x1, sh_out, ln2_g, ln2_b, _pick_tile(n_tok, FIN_TILE, LANES))

    dt = x_prompt.dtype
    y_prompt = jnp.transpose(y[:n_prompt].reshape(seq, bp, d), (1, 0, 2))
    y_sample = y[n_prompt:].reshape(n_s, 1, d)
    conv_p = jnp.transpose(p_conv.reshape(CONV_W - 1, bp, D_RNN), (1, 0, 2))
    s5_shape = (S5_GROUPS, S5_N)
    return (y_prompt.astype(dt), y_sample.astype(dt),
            p_h[None], conv_p[None],
            p_s5r.reshape(1, bp, *s5_shape), p_s5i.reshape(1, bp, *s5_shape),
            s_h[None], s_conv.reshape(1, n_s, CONV_W - 1, D_RNN),
            s_s5r.reshape(1, n_s, *s5_shape), s_s5i.reshape(1, n_s, *s5_shape))
```

```python
import functools
import math

import jax
import jax.numpy as jnp
from jax import lax
from jax.experimental import pallas as pl
from jax.experimental.pallas import tpu as pltpu

F32 = jnp.float32
BF16 = jnp.bfloat16
I32 = jnp.int32
U32 = jnp.uint32

D_MODEL = 2048
D_RNN = D_MODEL // 2
D_S5 = D_MODEL // 2
N_IN = 2 * D_RNN + D_S5 + 2 * D_MODEL
RG_HEADS = 8
CONV_W = 4
LRU_C = 8.0
S5_CH = 16
S5_GROUPS = D_S5 // S5_CH
S5_N = 64
S5_STATE = S5_GROUPS * S5_N
N_EXPERTS = 64
TOP_K = 8
N_GROUPS = 8
GROUP_SIZE = N_EXPERTS // N_GROUPS
TOPK_GROUPS = 4
D_EXPERT = 512
ROUTED_SCALE = 2.5
LN_EPS = 1e-5
N_META = 16
DEPTH = 1
ALPHA = (2.0 * DEPTH) ** 0.25

SUBLANES = 8
LANES = 128
MXU_DIM = 256
VMEM_LIMIT = 56 * 1024 * 1024

NSEQ = 4
S5_BLOCK_GROUPS = MXU_DIM // S5_CH
S5_BLOCKS = S5_GROUPS // S5_BLOCK_GROUPS
S5_BLOCK_STATE = S5_BLOCK_GROUPS * S5_N
IN_TILE_N = 512
POST_TILE = 320
TOKEN_TILE = 640
MOE_TILE = 256
FIN_TILE = 128


def _params(sem, vmem=VMEM_LIMIT):
    return pltpu.CompilerParams(dimension_semantics=sem, vmem_limit_bytes=vmem)


def _dot(a, b):
    return jnp.dot(a, b, preferred_element_type=F32)


def _layernorm(x, g, b):
    mu = jnp.mean(x, axis=-1, keepdims=True)
    xc = x - mu
    var = jnp.mean(xc * xc, axis=-1, keepdims=True)
    return xc * lax.rsqrt(var + LN_EPS) * g + b


def _sigmoid(x):
    return 1.0 / (1.0 + jnp.exp(-x))


def _gelu(x):
    c = math.sqrt(2.0 / math.pi)
    return 0.5 * x * (1.0 + jnp.tanh(c * (x + 0.044715 * (x * x * x))))


def _silu(x):
    return x * _sigmoid(x)


def _softplus(x):
    return jnp.maximum(x, 0.0) + jnp.log1p(jnp.exp(-jnp.abs(x)))


def _neg_expm1(x):
    poly = x * (1.0 + x * (1.0 / 2) * (1.0 + x * (1.0 / 3) * (1.0 + x * (1.0 / 4) * (
        1.0 + x * (1.0 / 5) * (1.0 + x * (1.0 / 6) * (1.0 + x * (1.0 / 7)))))))
    return -jnp.where(x > -0.25, poly, jnp.exp(x) - 1.0)


def _pack_bf16_pair(lo, hi):
    lo_bits = pltpu.bitcast(lo.astype(BF16).astype(F32), U32) >> 16
    hi_bits = pltpu.bitcast(hi.astype(BF16).astype(F32), U32) & jnp.uint32(0xFFFF0000)
    return hi_bits | lo_bits


def _unpack_bf16_pair(w):
    lo = pltpu.bitcast(w << 16, F32)
    hi = pltpu.bitcast(w & jnp.uint32(0xFFFF0000), F32)
    return lo, hi


def _s5_prep_body(are_ref, aim_ref, ldt_ref, bre_ref, bim_ref,
                  abr_ref, abi_ref, bbr_ref, bbi_ref):
    a_re = are_ref[...]
    a_im = aim_ref[...]
    dt = jnp.exp(ldt_ref[...])
    mag = jnp.exp(a_re * dt)
    ab_r = mag * jnp.cos(a_im * dt)
    ab_i = mag * jnp.sin(a_im * dt)
    den = a_re * a_re + a_im * a_im
    nr = ab_r - 1.0
    cr = (nr * a_re + ab_i * a_im) / den
    ci = (ab_i * a_re - nr * a_im) / den
    b_re = bre_ref[...]
    b_im = bim_ref[...]
    abr_ref[...] = ab_r
    abi_ref[...] = ab_i
    bbr_ref[...] = cr * b_re - ci * b_im
    bbi_ref[...] = cr * b_im + ci * b_re


def _s5_prep(a_re, a_im, log_dt, b_re, b_im):
    g, n, c = b_re.shape
    wide = c * n
    bc = lambda v: jnp.broadcast_to(v[:, None, :], (g, c, n)).reshape(g, wide)
    are_x = bc(a_re)
    aim_x = bc(a_im)
    ldt_x = jnp.broadcast_to(log_dt[:, None], (g, wide))
    bre_x = jnp.transpose(b_re, (0, 2, 1)).reshape(g, wide)
    bim_x = jnp.transpose(b_im, (0, 2, 1)).reshape(g, wide)
    shp = jax.ShapeDtypeStruct((g, wide), F32)
    abr, abi, bbr, bbi = pl.pallas_call(
        _s5_prep_body, out_shape=(shp, shp, shp, shp), name="s5_prep",
    )(are_x, aim_x, ldt_x, bre_x, bim_x)
    a_r = abr[:, :n].reshape(1, g * n)
    a_i = abi[:, :n].reshape(1, g * n)
    return a_r, a_i, bbr.reshape(g, c, n), bbi.reshape(g, c, n)


def _block_diag(x):
    k, g, a, b = x.shape
    eye = jnp.eye(g, dtype=x.dtype)
    return jnp.einsum("kgab,gh->kgahb", x, eye).reshape(k, g * a, g * b)


def _inproj_body(x_ref, g_ref, b_ref, w_ref, bias_ref, xa_ref, z_ref, xn_scr):
    j = pl.program_id(1)

    @pl.when(j == 0)
    def _():
        xn_scr[...] = _layernorm(x_ref[...], g_ref[...], b_ref[...]).astype(BF16)

    z = _dot(xn_scr[...], w_ref[...]) + bias_ref[...]
    n_xa = D_RNN // IN_TILE_N

    @pl.when(j < n_xa)
    def _():
        xa_ref[...] = z

    @pl.when(j >= n_xa)
    def _():
        z_ref[...] = z.astype(BF16)


def _inproj(xflat, ln_g, ln_b, w_in_bf, b_in, tile):
    n1 = xflat.shape[0]
    n_xa = D_RNN // IN_TILE_N
    grid = (n1 // tile, N_IN // IN_TILE_N)
    return pl.pallas_call(
        _inproj_body,
        grid=grid,
        in_specs=[
            pl.BlockSpec((tile, D_MODEL), lambda i, j: (i, 0)),
            pl.BlockSpec((1, D_MODEL), lambda i, j: (0, 0)),
            pl.BlockSpec((1, D_MODEL), lambda i, j: (0, 0)),
            pl.BlockSpec((D_MODEL, IN_TILE_N), lambda i, j: (0, j)),
            pl.BlockSpec((1, IN_TILE_N), lambda i, j: (0, j)),
        ],
        out_specs=[
            pl.BlockSpec((tile, IN_TILE_N), lambda i, j: (i, jnp.minimum(j, n_xa - 1))),
            pl.BlockSpec((tile, IN_TILE_N), lambda i, j: (i, jnp.maximum(j - n_xa, 0))),
        ],
        out_shape=[
            jax.ShapeDtypeStruct((n1, D_RNN), F32),
            jax.ShapeDtypeStruct((n1, N_IN - D_RNN), BF16),
        ],
        scratch_shapes=[pltpu.VMEM((tile, D_MODEL), BF16)],
        compiler_params=_params(("arbitrary", "arbitrary")),
        name="in_proj",
    )(xflat, ln_g, ln_b, w_in_bf, b_in)


def _rg_gates(xc, wa_ref, wi_ref, ba, bi, sp):
    xcb = xc.astype(BF16)
    nblk = D_RNN // MXU_DIM
    r_pre = jnp.concatenate(
        [_dot(xcb[:, k * MXU_DIM:(k + 1) * MXU_DIM], wa_ref[k]) for k in range(nblk)], axis=1)
    i_pre = jnp.concatenate(
        [_dot(xcb[:, k * MXU_DIM:(k + 1) * MXU_DIM], wi_ref[k]) for k in range(nblk)], axis=1)
    r = _sigmoid(r_pre + ba)
    i = _sigmoid(i_pre + bi)
    log_a = (-LRU_C * r) * sp
    a = jnp.exp(log_a)
    u = jnp.sqrt(_neg_expm1(2.0 * log_a)) * (i * xc)
    return a, u


def _odd_rows(width):
    return lax.broadcasted_iota(I32, (SUBLANES, width), 0) >= NSEQ


def _mixer_chunk(rows, xa_ref, ya_ref, us_ref, ua_ref, gl_ref, w, s):
    (cw_ref, cb_ref, wa_ref, wi_ref, ba_ref, bi_ref, lam_ref, bdb_ref, cre_ref, cim_ref,
     d_ref, gluw_ref, glub_ref) = w
    (ext_scr, a_scr, u_scr, hs_scr, bu_scr, hst_scr, y_scr, tail_scr, hcar_scr, s5car_scr,
     cst_scr) = s
    emit = ua_ref is not None
    halo = NSEQ * CONV_W
    ngroups = rows // SUBLANES

    ext_scr[pl.ds(0, halo), :] = tail_scr[...]
    ext_scr[pl.ds(halo, rows), :] = xa_ref[...]
    xc = cb_ref[...] + cw_ref[pl.ds(CONV_W - 1, 1), :] * ext_scr[pl.ds(halo, rows), :]
    for j in range(1, CONV_W):
        xc = xc + cw_ref[pl.ds(CONV_W - 1 - j, 1), :] * ext_scr[pl.ds(halo - NSEQ * j, rows), :]
    tail_scr[...] = ext_scr[pl.ds(rows, halo), :]

    sp = _softplus(-lam_ref[...])
    a, u = _rg_gates(xc, wa_ref, wi_ref, ba_ref[...], bi_ref[...], sp)
    a_scr[pl.ds(0, rows), :] = a
    u_scr[pl.ds(0, rows), :] = u
    odd = _odd_rows(D_RNN)

    def rg_body(g, c):
        row = pl.multiple_of(g * SUBLANES, SUBLANES)
        a_v = a_scr[pl.ds(row, SUBLANES), :]
        u_v = u_scr[pl.ds(row, SUBLANES), :]
        hl = u_v + jnp.where(odd, a_v * pltpu.roll(u_v, NSEQ, 0), 0.0)
        p = jnp.where(odd, a_v * pltpu.roll(a_v, NSEQ, 0), a_v)
        hs_scr[pl.ds(row, SUBLANES), :] = hl + p * c
        q = jnp.where(odd, hl, pltpu.roll(hl, NSEQ, 0))
        pp = jnp.where(odd, p, pltpu.roll(p, NSEQ, 0))
        return q + pp * c

    hcar_scr[...] = lax.fori_loop(0, ngroups, rg_body, hcar_scr[...])
    if emit:
        ua_ref[...] = (hs_scr[pl.ds(0, rows), :] * _gelu(ya_ref[...].astype(F32))).astype(BF16)

    odd_s = _odd_rows(S5_BLOCK_STATE)
    for kb in range(S5_BLOCKS):
        lo = kb * S5_BLOCK_STATE
        ub = us_ref[:, kb * MXU_DIM:(kb + 1) * MXU_DIM]
        bu_scr[pl.ds(0, rows), :] = _dot(ub, bdb_ref[kb])
        aor, aoi, pr, pi, a2r, a2i = [cst_scr[i, :, lo:lo + S5_BLOCK_STATE] for i in range(6)]

        def s5_body(g, c, aor=aor, aoi=aoi, pr=pr, pi=pi, a2r=a2r, a2i=a2i):
            cr, ci = c
            row = pl.multiple_of(g * SUBLANES, SUBLANES)
            bur = bu_scr[pl.ds(row, SUBLANES), 0:S5_BLOCK_STATE]
            bui = bu_scr[pl.ds(row, SUBLANES), S5_BLOCK_STATE:2 * S5_BLOCK_STATE]
            sr = pltpu.roll(bur, NSEQ, 0)
            si = pltpu.roll(bui, NSEQ, 0)
            hlr = bur + aor * sr - aoi * si
            hli = bui + aor * si + aoi * sr
            if emit:
                hst_scr[pl.ds(row, SUBLANES), 0:S5_BLOCK_STATE] = hlr + pr * cr - pi * ci
                hst_scr[pl.ds(row, SUBLANES), S5_BLOCK_STATE:2 * S5_BLOCK_STATE] = (
                    hli + pr * ci + pi * cr)
            qr = jnp.where(odd_s, hlr, pltpu.roll(hlr, NSEQ, 0))
            qi = jnp.where(odd_s, hli, pltpu.roll(hli, NSEQ, 0))
            return qr + a2r * cr - a2i * ci, qi + a2r * ci + a2i * cr

        cr, ci = lax.fori_loop(
            0, ngroups, s5_body,
            (s5car_scr[0, :, lo:lo + S5_BLOCK_STATE], s5car_scr[1, :, lo:lo + S5_BLOCK_STATE]))
        s5car_scr[0, :, lo:lo + S5_BLOCK_STATE] = cr
        s5car_scr[1, :, lo:lo + S5_BLOCK_STATE] = ci
        if emit:
            hre = hst_scr[pl.ds(0, rows), 0:S5_BLOCK_STATE].astype(BF16)
            him = hst_scr[pl.ds(0, rows), S5_BLOCK_STATE:2 * S5_BLOCK_STATE].astype(BF16)
            y = _dot(hre, cre_ref[kb]) - _dot(him, cim_ref[kb])
            y = y + d_ref[:, kb * MXU_DIM:(kb + 1) * MXU_DIM] * ub.astype(F32)
            y_scr[pl.ds(0, rows), kb * MXU_DIM:(kb + 1) * MXU_DIM] = y

    if emit:
        g5 = _gelu(y_scr[pl.ds(0, rows), :])
        gate = _sigmoid(_dot(g5.astype(BF16), gluw_ref[...]) + glub_ref[...])
        gl_ref[...] = (g5 * gate).astype(BF16)


def _mixer_body(meta_rows, n_s, xa_ref, ya_ref, us_ref, xam_ref, usm_ref, ar_ref, ai_ref,
                h0_ref, cbuf_ref, s5r0_ref, s5i0_ref, *rest):
    w = rest[:13]
    (ua_ref, gl_ref, hout_ref, convout_ref, s5r_ref, s5i_ref,
     h1_ref, cnew_ref, s5r1_ref, s5i1_ref) = rest[13:23]
    s = rest[23:]
    y_scr, tail_scr, hcar_scr, s5car_scr, cst_scr = s[6], s[7], s[8], s[9], s[10]
    c = pl.program_id(0)
    rows = xa_ref.shape[0]
    nchunks = pl.num_programs(0) - 1

    @pl.when(c == nchunks)
    def _():
        head = lambda r: r.at[pl.ds(0, n_s)]
        _sample_step(head(xa_ref), head(ya_ref), head(us_ref), h0_ref, cbuf_ref, s5r0_ref,
                     s5i0_ref, ar_ref, ai_ref, w, head(ua_ref), head(gl_ref), h1_ref, cnew_ref,
                     s5r1_ref, s5i1_ref, head(y_scr))
        ua_ref[pl.ds(n_s, rows - n_s), :] = jnp.zeros((rows - n_s, D_RNN), BF16)
        gl_ref[pl.ds(n_s, rows - n_s), :] = jnp.zeros((rows - n_s, D_S5), BF16)

    @pl.when(c < nchunks)
    def _():
        _prompt_step(meta_rows, nchunks, xa_ref, ya_ref, us_ref, xam_ref, usm_ref, ar_ref, ai_ref,
                     w, ua_ref, gl_ref, hout_ref, convout_ref, s5r_ref, s5i_ref, s)


def _prompt_step(meta_rows, nchunks, xa_ref, ya_ref, us_ref, xam_ref, usm_ref, ar_ref, ai_ref,
                 w, ua_ref, gl_ref, hout_ref, convout_ref, s5r_ref, s5i_ref, s):
    tail_scr, hcar_scr, s5car_scr, cst_scr = s[7], s[8], s[9], s[10]
    c = pl.program_id(0)
    rows = xa_ref.shape[0]

    @pl.when(c == 0)
    def _():
        odd = _odd_rows(S5_STATE)
        ar = jnp.broadcast_to(ar_ref[...], (SUBLANES, S5_STATE))
        ai = jnp.broadcast_to(ai_ref[...], (SUBLANES, S5_STATE))
        a2r = ar * ar - ai * ai
        a2i = 2.0 * (ar * ai)
        cst_scr[0] = jnp.where(odd, ar, 0.0)
        cst_scr[1] = jnp.where(odd, ai, 0.0)
        cst_scr[2] = jnp.where(odd, a2r, ar)
        cst_scr[3] = jnp.where(odd, a2i, ai)
        cst_scr[4] = a2r
        cst_scr[5] = a2i
        tail_scr[...] = jnp.zeros_like(tail_scr)
        hcar_scr[...] = jnp.zeros_like(hcar_scr)
        s5car_scr[...] = jnp.zeros_like(s5car_scr)
        _mixer_chunk(meta_rows, xam_ref, None, usm_ref, None, None, w, s)

    _mixer_chunk(rows, xa_ref, ya_ref, us_ref, ua_ref, gl_ref, w, s)

    @pl.when(c == nchunks - 1)
    def _():
        hout_ref[...] = hcar_scr[pl.ds(NSEQ, NSEQ), :]
        convout_ref[...] = tail_scr[pl.ds(NSEQ, NSEQ * (CONV_W - 1)), :]
        s5r_ref[...] = s5car_scr[0, pl.ds(NSEQ, NSEQ), :]
        s5i_ref[...] = s5car_scr[1, pl.ds(NSEQ, NSEQ), :]


def _mixer_weight_specs(nidx):
    z = (0,) * nidx if nidx else ()
    c2 = lambda *_: (0, 0)
    c3 = lambda *_: (0, 0, 0)
    nblk = D_RNN // MXU_DIM
    return [
        pl.BlockSpec((CONV_W, D_RNN), c2),
        pl.BlockSpec((1, D_RNN), c2),
        pl.BlockSpec((nblk, MXU_DIM, MXU_DIM), c3),
        pl.BlockSpec((nblk, MXU_DIM, MXU_DIM), c3),
        pl.BlockSpec((1, D_RNN), c2),
        pl.BlockSpec((1, D_RNN), c2),
        pl.BlockSpec((1, D_RNN), c2),
        pl.BlockSpec((S5_BLOCKS, MXU_DIM, 2 * S5_BLOCK_STATE), c3),
        pl.BlockSpec((S5_BLOCKS, S5_BLOCK_STATE, MXU_DIM), c3),
        pl.BlockSpec((S5_BLOCKS, S5_BLOCK_STATE, MXU_DIM), c3),
        pl.BlockSpec((1, D_S5), c2),
        pl.BlockSpec((D_S5, D_S5), c2),
        pl.BlockSpec((1, D_S5), c2),
    ]


def _mixer(xa, z16, a_r, a_i, h0, cbuf, s5r0, s5i0, weights, n_prompt, n_s, meta_row0, meta_rows,
           chunk):
    nchunks = n_prompt // chunk
    n1 = xa.shape[0]
    assert n1 == n_prompt + chunk and n_s <= chunk
    meta_blk = meta_row0 // meta_rows
    halo = NSEQ * CONV_W
    c2 = lambda c: (0, 0)
    in_specs = [
        pl.BlockSpec((chunk, D_RNN), lambda c: (c, 0)),
        pl.BlockSpec((chunk, D_RNN), lambda c: (c, 0)),
        pl.BlockSpec((chunk, D_S5), lambda c: (c, 1)),
        pl.BlockSpec((meta_rows, D_RNN), lambda c: (meta_blk, 0)),
        pl.BlockSpec((meta_rows, D_S5), lambda c: (meta_blk, 1)),
        pl.BlockSpec((1, S5_STATE), c2),
        pl.BlockSpec((1, S5_STATE), c2),
        pl.BlockSpec((n_s, D_RNN), c2),
        pl.BlockSpec((n_s, (CONV_W - 1) * D_RNN), c2),
        pl.BlockSpec((n_s, S5_STATE), c2),
        pl.BlockSpec((n_s, S5_STATE), c2),
    ] + _mixer_weight_specs(1)
    out_specs = [
        pl.BlockSpec((chunk, D_RNN), lambda c: (c, 0)),
        pl.BlockSpec((chunk, D_S5), lambda c: (c, 0)),
        pl.BlockSpec((NSEQ, D_RNN), c2),
        pl.BlockSpec((NSEQ * (CONV_W - 1), D_RNN), c2),
        pl.BlockSpec((NSEQ, S5_STATE), c2),
        pl.BlockSpec((NSEQ, S5_STATE), c2),
        pl.BlockSpec((n_s, D_RNN), c2),
        pl.BlockSpec((n_s, (CONV_W - 1) * D_RNN), c2),
        pl.BlockSpec((n_s, S5_STATE), c2),
        pl.BlockSpec((n_s, S5_STATE), c2),
    ]
    out_shape = [
        jax.ShapeDtypeStruct((n1, D_RNN), BF16),
        jax.ShapeDtypeStruct((n1, D_S5), BF16),
        jax.ShapeDtypeStruct((NSEQ, D_RNN), F32),
        jax.ShapeDtypeStruct((NSEQ * (CONV_W - 1), D_RNN), F32),
        jax.ShapeDtypeStruct((NSEQ, S5_STATE), F32),
        jax.ShapeDtypeStruct((NSEQ, S5_STATE), F32),
        jax.ShapeDtypeStruct((n_s, D_RNN), F32),
        jax.ShapeDtypeStruct((n_s, (CONV_W - 1) * D_RNN), F32),
        jax.ShapeDtypeStruct((n_s, S5_STATE), F32),
        jax.ShapeDtypeStruct((n_s, S5_STATE), F32),
    ]
    scratch = [
        pltpu.VMEM((chunk + halo, D_RNN), F32),
        pltpu.VMEM((chunk, D_RNN), F32),
        pltpu.VMEM((chunk, D_RNN), F32),
        pltpu.VMEM((chunk, D_RNN), F32),
        pltpu.VMEM((chunk, 2 * S5_BLOCK_STATE), F32),
        pltpu.VMEM((chunk, 2 * S5_BLOCK_STATE), F32),
        pltpu.VMEM((chunk, D_S5), F32),
        pltpu.VMEM((halo, D_RNN), F32),
        pltpu.VMEM((SUBLANES, D_RNN), F32),
        pltpu.VMEM((2, SUBLANES, S5_STATE), F32),
        pltpu.VMEM((6, SUBLANES, S5_STATE), F32),
    ]
    return pl.pallas_call(
        functools.partial(_mixer_body, meta_rows, n_s),
        grid=(nchunks + 1,),
        in_specs=in_specs,
        out_specs=out_specs,
        out_shape=out_shape,
        scratch_shapes=scratch,
        compiler_params=_params(("arbitrary",)),
        name="mixer",
    )(xa, z16, z16, xa, z16, a_r, a_i, h0, cbuf, s5r0, s5i0, *weights)


def _sample_step(xa_ref, ya_ref, us_ref, h0_ref, cbuf_ref, s5r0_ref, s5i0_ref, ar_ref, ai_ref,
                 w, ua_ref, gl_ref, h1_ref, cnew_ref, s5r1_ref, s5i1_ref, y_scr):
    (cw_ref, cb_ref, wa_ref, wi_ref, ba_ref, bi_ref, lam_ref, bdb_ref, cre_ref, cim_ref,
     d_ref, gluw_ref, glub_ref) = w
    xa = xa_ref[...]
    xc = cb_ref[...] + cw_ref[pl.ds(CONV_W - 1, 1), :] * xa
    for k in range(CONV_W - 1):
        xc = xc + cw_ref[pl.ds(k, 1), :] * cbuf_ref[:, k * D_RNN:(k + 1) * D_RNN]
    for k in range(CONV_W - 2):
        cnew_ref[:, k * D_RNN:(k + 1) * D_RNN] = cbuf_ref[:, (k + 1) * D_RNN:(k + 2) * D_RNN]
    cnew_ref[:, (CONV_W - 2) * D_RNN:(CONV_W - 1) * D_RNN] = xa

    sp = _softplus(-lam_ref[...])
    a, u = _rg_gates(xc, wa_ref, wi_ref, ba_ref[...], bi_ref[...], sp)
    h1 = a * h0_ref[...] + u
    h1_ref[...] = h1
    ua_ref[...] = (h1 * _gelu(ya_ref[...].astype(F32))).astype(BF16)

    for kb in range(S5_BLOCKS):
        lo = kb * S5_BLOCK_STATE
        ub = us_ref[:, kb * MXU_DIM:(kb + 1) * MXU_DIM]
        bu = _dot(ub, bdb_ref[kb])
        ar = ar_ref[:, lo:lo + S5_BLOCK_STATE]
        ai = ai_ref[:, lo:lo + S5_BLOCK_STATE]
        h0r = s5r0_ref[:, lo:lo + S5_BLOCK_STATE]
        h0i = s5i0_ref[:, lo:lo + S5_BLOCK_STATE]
        hr = bu[:, 0:S5_BLOCK_STATE] + ar * h0r - ai * h0i
        hi = bu[:, S5_BLOCK_STATE:2 * S5_BLOCK_STATE] + ar * h0i + ai * h0r
        s5r1_ref[:, lo:lo + S5_BLOCK_STATE] = hr
        s5i1_ref[:, lo:lo + S5_BLOCK_STATE] = hi
        y = _dot(hr.astype(BF16), cre_ref[kb]) - _dot(hi.astype(BF16), cim_ref[kb])
        y_scr[:, kb * MXU_DIM:(kb + 1) * MXU_DIM] = (
            y + d_ref[:, kb * MXU_DIM:(kb + 1) * MXU_DIM] * ub.astype(F32))

    g5 = _gelu(y_scr[...])
    gate = _sigmoid(_dot(g5.astype(BF16), gluw_ref[...]) + glub_ref[...])
    gl_ref[...] = (g5 * gate).astype(BF16)


def _col_min(x):
    return jnp.min(x, axis=0, keepdims=True)


def _col_max(x):
    return jnp.max(x, axis=0, keepdims=True)


def _col_sum(x):
    return jnp.sum(x, axis=0, keepdims=True)


def _route(sel, scores, tm):
    neg = -jnp.inf
    iota = lax.broadcasted_iota(I32, (GROUP_SIZE, tm), 0)
    sel_b = [sel[g * GROUP_SIZE:(g + 1) * GROUP_SIZE, :] for g in range(N_GROUPS)]
    sc_b = [scores[g * GROUP_SIZE:(g + 1) * GROUP_SIZE, :] for g in range(N_GROUPS)]

    iota_g = lax.broadcasted_iota(I32, (N_GROUPS, tm), 0)
    gs = jnp.zeros((N_GROUPS, tm), F32)
    for g in range(N_GROUPS):
        b = sel_b[g]
        m1 = _col_max(b)
        i1 = _col_min(jnp.where(b == m1, iota, GROUP_SIZE))
        m2 = _col_max(jnp.where(iota == i1, neg, b))
        gs = jnp.where(iota_g == g, m1 + m2, gs)

    keep = jnp.zeros((N_GROUPS, tm), I32)
    work = gs
    for _ in range(TOPK_GROUPS):
        m = _col_max(work)
        idx = _col_min(jnp.where(work == m, iota_g, N_GROUPS))
        hit = iota_g == idx
        keep = jnp.where(hit, 1, keep)
        work = jnp.where(hit, neg, work)

    cand = [jnp.where(keep[g:g + 1, :] > 0, sel_b[g], neg) for g in range(N_GROUPS)]
    ids, vals = [], []
    for _ in range(TOP_K):
        m = _col_max(cand[0])
        for g in range(1, N_GROUPS):
            m = jnp.maximum(m, _col_max(cand[g]))
        idx = _col_min(jnp.where(cand[0] == m, iota, N_EXPERTS))
        for g in range(1, N_GROUPS):
            idx = jnp.minimum(
                idx, _col_min(jnp.where(cand[g] == m, iota + g * GROUP_SIZE, N_EXPERTS)))
        val = jnp.zeros((1, tm), F32)
        for g in range(N_GROUPS):
            hit = (iota + g * GROUP_SIZE) == idx
            val = val + _col_sum(jnp.where(hit, sc_b[g], 0.0))
            cand[g] = jnp.where(hit, neg, cand[g])
        ids.append(idx)
        vals.append(val)
    return ids, vals


def _post_body(ua_ref, gl_ref, ga_ref, gb_ref, x_ref, ling_ref, linb_ref, pa_ref, pb_ref,
               wo_ref, l1g_ref, l1b_ref, x1_ref):
    branch_a = _dot(ua_ref[...], pa_ref[...])
    branch_b = _dot(gl_ref[...], pb_ref[...])
    merged = (_sigmoid(ga_ref[...].astype(F32)) * branch_a
              + _sigmoid(gb_ref[...].astype(F32)) * branch_b)
    o = _dot(merged.astype(BF16), wo_ref[...])
    xn = _layernorm(x_ref[...], ling_ref[...], linb_ref[...])
    x1_ref[...] = _layernorm(ALPHA * xn + o, l1g_ref[...], l1b_ref[...])


def _post_mixer(ua, gl, z16, xflat, ln_in_g, ln_in_b, proj_a, proj_b, w_o, ln1_g, ln1_b,
                n_tok, tile):
    c2 = lambda i: (0, 0)
    in_specs = [
        pl.BlockSpec((tile, D_RNN), lambda i: (i, 0)),
        pl.BlockSpec((tile, D_S5), lambda i: (i, 0)),
        pl.BlockSpec((tile, D_MODEL), lambda i: (i, 1)),
        pl.BlockSpec((tile, D_MODEL), lambda i: (i, 2)),
        pl.BlockSpec((tile, D_MODEL), lambda i: (i, 0)),
        pl.BlockSpec((1, D_MODEL), c2),
        pl.BlockSpec((1, D_MODEL), c2),
        pl.BlockSpec((D_RNN, D_MODEL), c2),
        pl.BlockSpec((D_S5, D_MODEL), c2),
        pl.BlockSpec((D_MODEL, D_MODEL), c2),
        pl.BlockSpec((1, D_MODEL), c2),
        pl.BlockSpec((1, D_MODEL), c2),
    ]
    return pl.pallas_call(
        _post_body,
        grid=(n_tok // tile,),
        in_specs=in_specs,
        out_specs=pl.BlockSpec((tile, D_MODEL), lambda i: (i, 0)),
        out_shape=jax.ShapeDtypeStruct((n_tok, D_MODEL), F32),
        compiler_params=_params(("arbitrary",)),
        name="post_mixer",
    )(ua, gl, z16, z16, xflat, ln_in_g, ln_in_b, proj_a, proj_b, w_o, ln1_g, ln1_b)


def _router_body(x1_ref, wrh_ref, wrl_ref, rb_ref,
                 x1p_ref, eidx_ref, rank_ref, gatet_ref, cnt_ref, cnt_scr):
    i = pl.program_id(0)
    tm = x1_ref.shape[0]

    @pl.when(i == 0)
    def _():
        cnt_scr[...] = jnp.zeros_like(cnt_scr)

    x1 = x1_ref[...]
    half = D_MODEL // 2
    x1p_ref[...] = _pack_bf16_pair(x1[:, :half], x1[:, half:])

    x_hi = x1.astype(BF16)
    x_lo = (x1 - x_hi.astype(F32)).astype(BF16)
    nt = (((1,), (1,)), ((), ()))
    dg = lambda a, b: lax.dot_general(a, b, nt, preferred_element_type=F32)
    logits = dg(wrh_ref[...], x_hi) + dg(wrh_ref[...], x_lo) + dg(wrl_ref[...], x_hi)
    scores = _sigmoid(logits)
    sel = scores + rb_ref[...]
    ids, vals = _route(sel, scores, tm)

    total = vals[0]
    for v in vals[1:]:
        total = total + v
    iota_k = lax.broadcasted_iota(I32, (TOP_K, tm), 0)
    iota_e = lax.broadcasted_iota(I32, (N_EXPERTS, tm), 0)
    eidx = jnp.zeros((TOP_K, tm), I32)
    gates = jnp.zeros((TOP_K, tm), F32)
    selm = jnp.zeros((N_EXPERTS, tm), F32)
    for k in range(TOP_K):
        eidx = jnp.where(iota_k == k, ids[k], eidx)
        gates = jnp.where(iota_k == k, vals[k] / total * ROUTED_SCALE, gates)
        selm = jnp.where(iota_e == ids[k], 1.0, selm)
    eidx_ref[...] = eidx

    r_i = lax.broadcasted_iota(I32, (tm, tm), 0)
    c_i = lax.broadcasted_iota(I32, (tm, tm), 1)
    upper = jnp.where(r_i < c_i, 1.0, 0.0).astype(BF16)
    rank_all = _dot(selm.astype(BF16), upper) + cnt_scr[...]
    rank = jnp.zeros((TOP_K, tm), F32)
    for k in range(TOP_K):
        rk = _col_sum(jnp.where(iota_e == ids[k], rank_all, 0.0))
        rank = jnp.where(iota_k == k, rk, rank)
    rank_ref[...] = rank.astype(I32)
    cnt_scr[...] = cnt_scr[...] + jnp.sum(selm, axis=1, keepdims=True)
    cnt_ref[...] = cnt_scr[...]

    gpad = jnp.concatenate([gates, jnp.zeros((LANES - TOP_K, tm), F32)], axis=0)
    gatet_ref[...] = gpad.T


def _router(x1, wr_hi, wr_lo, rbias, tile):
    n_tok = x1.shape[0]
    c2 = lambda i: (0, 0)
    half = D_MODEL // 2
    return pl.pallas_call(
        _router_body,
        grid=(n_tok // tile,),
        in_specs=[
            pl.BlockSpec((tile, D_MODEL), lambda i: (i, 0)),
            pl.BlockSpec((N_EXPERTS, D_MODEL), c2),
            pl.BlockSpec((N_EXPERTS, D_MODEL), c2),
            pl.BlockSpec((N_EXPERTS, 1), c2),
        ],
        out_specs=[
            pl.BlockSpec((tile, half), lambda i: (i, 0)),
            pl.BlockSpec((TOP_K, tile), lambda i: (0, i)),
            pl.BlockSpec((TOP_K, tile), lambda i: (0, i)),
            pl.BlockSpec((tile, LANES), lambda i: (i, 0)),
            pl.BlockSpec((N_EXPERTS, 1), c2),
        ],
        out_shape=[
            jax.ShapeDtypeStruct((n_tok, half), U32),
            jax.ShapeDtypeStruct((TOP_K, n_tok), I32),
            jax.ShapeDtypeStruct((TOP_K, n_tok), I32),
            jax.ShapeDtypeStruct((n_tok, LANES), F32),
            jax.ShapeDtypeStruct((N_EXPERTS, 1), F32),
        ],
        scratch_shapes=[pltpu.VMEM((N_EXPERTS, 1), F32)],
        compiler_params=_params(("arbitrary",)),
        name="router",
    )(x1, wr_hi, wr_lo, rbias)


def _positions_body(eidx_ref, rank_ref, cnt_ref, pos_ref, texp_ref, info_ref):
    cnt = cnt_ref[...]
    padded = jnp.floor((cnt + (MOE_TILE - 1)) * (1.0 / MOE_TILE)) * MOE_TILE
    r_i = lax.broadcasted_iota(I32, (N_EXPERTS, N_EXPERTS), 0)
    c_i = lax.broadcasted_iota(I32, (N_EXPERTS, N_EXPERTS), 1)
    eye = r_i == c_i
    as_row = lambda col: jnp.sum(jnp.where(eye, col, 0.0), axis=0, keepdims=True)
    padded_row = as_row(padded)
    base = jnp.sum(jnp.where(c_i < r_i, padded_row, 0.0), axis=1, keepdims=True)
    end = base + padded

    eidx = eidx_ref[...]
    pos = rank_ref[...]
    base_i = base.astype(I32)
    for e in range(N_EXPERTS):
        pos = pos + jnp.where(eidx == e, base_i[e:e + 1, :], 0)
    pos_ref[...] = pos

    ntp = texp_ref.shape[1]
    t_row = lax.broadcasted_iota(I32, (1, ntp), 1).astype(F32) * MOE_TILE
    texp = jnp.sum(jnp.where(end <= t_row, 1, 0), axis=0, keepdims=True)
    texp_ref[...] = jnp.minimum(texp, N_EXPERTS - 1).astype(I32)

    row = lax.broadcasted_iota(I32, (SUBLANES, LANES), 0)
    lane = lax.broadcasted_iota(I32, (SUBLANES, LANES), 1)
    pad_lanes = lambda r: jnp.concatenate(
        [r, jnp.zeros((1, LANES - N_EXPERTS), F32)], axis=1)
    ntiles = jnp.sum(padded, axis=0, keepdims=True) * (1.0 / MOE_TILE)
    info = jnp.where(row == 0, pad_lanes(as_row(cnt)), 0.0)
    info = jnp.where(row == 1, pad_lanes(as_row(base)), info)
    info = jnp.where(row == 2, pad_lanes(as_row(end)), info)
    info = jnp.where((row == 3) & (lane == 0), ntiles, info)
    info_ref[...] = info.astype(I32)


def _positions(eidx, rank, cnt, n_tiles_max):
    n_tok = eidx.shape[1]
    ntp = -(-n_tiles_max // LANES) * LANES
    return pl.pallas_call(
        _positions_body,
        out_shape=[
            jax.ShapeDtypeStruct((TOP_K, n_tok), I32),
            jax.ShapeDtypeStruct((1, ntp), I32),
            jax.ShapeDtypeStruct((SUBLANES, LANES), I32),
        ],
        compiler_params=pltpu.CompilerParams(vmem_limit_bytes=VMEM_LIMIT),
        name="positions",
    )(eidx, rank, cnt)


def _row_copy(src_hbm, src_row, dst_hbm, dst_row, sem):
    return pltpu.make_async_copy(
        src_hbm.at[pl.ds(src_row, 1)], dst_hbm.at[pl.ds(dst_row, 1)], sem)


def _swiglu_packed(xp, w1_ref, w3_ref, w2_ref):
    half = D_MODEL // 2
    lo, hi = _unpack_bf16_pair(xp)
    lo = lo.astype(BF16)
    hi = hi.astype(BF16)
    h1 = _dot(lo, w1_ref[0:half, :]) + _dot(hi, w1_ref[half:D_MODEL, :])
    h3 = _dot(lo, w3_ref[0:half, :]) + _dot(hi, w3_ref[half:D_MODEL, :])
    h = (_silu(h1) * h3).astype(BF16)
    return _dot(h, w2_ref[...])


def _dispatch_body(info_ref, pos_ref, xp_ref, w1_ref, w3_ref, w2_ref,
                   xs_hbm, sh_ref, zero_scr, sem):
    i = pl.program_id(0)
    tm = xp_ref.shape[0]

    def issue(t, carry):
        for k in range(TOP_K):
            _row_copy(xp_ref, t, xs_hbm, pos_ref[k, t], sem).start()
        return carry

    lax.fori_loop(0, tm, issue, 0)
    sh_ref[...] = _swiglu_packed(xp_ref[...], w1_ref, w3_ref, w2_ref)

    def drain(t, carry):
        for k in range(TOP_K):
            _row_copy(xp_ref, 0, xs_hbm, 0, sem).wait()
        return carry

    lax.fori_loop(0, tm, drain, 0)

    @pl.when(i == pl.num_programs(0) - 1)
    def _():
        zero_scr[...] = jnp.zeros_like(zero_scr)

        def per_expert(e, carry):
            start = info_ref[1, e] + info_ref[0, e]
            stop = info_ref[2, e]

            def fill(r, c):
                _row_copy(zero_scr, 0, xs_hbm, r, sem).start()
                return c

            def fill_wait(r, c):
                _row_copy(zero_scr, 0, xs_hbm, 0, sem).wait()
                return c

            lax.fori_loop(start, stop, fill, 0)
            lax.fori_loop(start, stop, fill_wait, 0)
            return carry

        lax.fori_loop(0, N_EXPERTS, per_expert, 0)


def _dispatch(info, pos, x1p, sh_w1, sh_w3, sh_w2, rows_sorted, tile):
    n_tok, half = x1p.shape
    c2 = lambda i, info: (0, 0)
    grid_spec = pltpu.PrefetchScalarGridSpec(
        num_scalar_prefetch=1,
        grid=(n_tok // tile,),
        in_specs=[
            pl.BlockSpec((TOP_K, tile), lambda i, info: (0, i), memory_space=pltpu.SMEM),
            pl.BlockSpec((tile, half), lambda i, info: (i, 0)),
            pl.BlockSpec((D_MODEL, D_EXPERT), c2),
            pl.BlockSpec((D_MODEL, D_EXPERT), c2),
            pl.BlockSpec((D_EXPERT, D_MODEL), c2),
        ],
        out_specs=[
            pl.BlockSpec(memory_space=pl.ANY),
            pl.BlockSpec((tile, D_MODEL), lambda i, info: (i, 0)),
        ],
        scratch_shapes=[pltpu.VMEM((SUBLANES, half), U32), pltpu.SemaphoreType.DMA],
    )
    return pl.pallas_call(
        _dispatch_body,
        grid_spec=grid_spec,
        out_shape=[
            jax.ShapeDtypeStruct((rows_sorted, half), U32),
            jax.ShapeDtypeStruct((n_tok, D_MODEL), F32),
        ],
        compiler_params=_params(("arbitrary",)),
        name="dispatch",
    )(info, pos, x1p, sh_w1, sh_w3, sh_w2)


def _moe_body(texp_ref, info_ref, xs_ref, w1_ref, w3_ref, w2_ref, ys_ref,
              w1b_scr, w3b_scr, w2b_scr):
    t = pl.program_id(0)
    ntiles = info_ref[3, 0]
    tt = jnp.minimum(t, ntiles - 1)
    prev = jnp.maximum(tt - 1, 0)
    fresh = (t == 0) | (texp_ref[0, tt] != texp_ref[0, prev])
    valid = t < ntiles

    @pl.when(valid & fresh)
    def _():
        w1b_scr[...] = w1_ref[0].astype(BF16)
        w3b_scr[...] = w3_ref[0].astype(BF16)
        w2b_scr[...] = w2_ref[0].astype(BF16)

    @pl.when(valid)
    def _():
        y = _swiglu_packed(xs_ref[...], w1b_scr, w3b_scr, w2b_scr)
        half = D_MODEL // 2
        ys_ref[...] = _pack_bf16_pair(y[:, :half], y[:, half:])


def _moe(texp, info, xs, ex_w1, ex_w3, ex_w2, n_tiles_max):
    half = D_MODEL // 2

    def tile_idx(t, texp, info):
        return jnp.minimum(t, info[3, 0] - 1)

    def w_idx(t, texp, info):
        return (texp[0, tile_idx(t, texp, info)], 0, 0)

    grid_spec = pltpu.PrefetchScalarGridSpec(
        num_scalar_prefetch=2,
        grid=(n_tiles_max,),
        in_specs=[
            pl.BlockSpec((MOE_TILE, half), lambda t, texp, info: (tile_idx(t, texp, info), 0)),
            pl.BlockSpec((1, D_MODEL, D_EXPERT), w_idx),
            pl.BlockSpec((1, D_MODEL, D_EXPERT), w_idx),
            pl.BlockSpec((1, D_EXPERT, D_MODEL), w_idx),
        ],
        out_specs=pl.BlockSpec(
            (MOE_TILE, half), lambda t, texp, info: (tile_idx(t, texp, info), 0)),
        scratch_shapes=[
            pltpu.VMEM((D_MODEL, D_EXPERT), BF16),
            pltpu.VMEM((D_MODEL, D_EXPERT), BF16),
            pltpu.VMEM((D_EXPERT, D_MODEL), BF16),
        ],
    )
    return pl.pallas_call(
        _moe_body,
        grid_spec=grid_spec,
        out_shape=jax.ShapeDtypeStruct(xs.shape, U32),
        compiler_params=_params(("arbitrary",)),
        name="moe_experts",
    )(texp, info, xs, ex_w1, ex_w3, ex_w2)


def _combine_body(pos_ref, ys_hbm, gate_ref, x1_ref, sh_ref, g_ref, b_ref, out_ref,
                  buf_scr, sem):
    tm = x1_ref.shape[0]

    def issue(t, carry):
        for k in range(TOP_K):
            pltpu.make_async_copy(
                ys_hbm.at[pl.ds(pos_ref[k, t], 1)], buf_scr.at[k, pl.ds(t, 1)], sem).start()
        return carry

    lax.fori_loop(0, tm, issue, 0)

    def drain(t, carry):
        for k in range(TOP_K):
            pltpu.make_async_copy(
                ys_hbm.at[pl.ds(0, 1)], buf_scr.at[k, pl.ds(0, 1)], sem).wait()
        return carry

    lax.fori_loop(0, tm, drain, 0)

    gates = gate_ref[...]
    half = D_MODEL // 2
    acc = sh_ref[...]
    for k in range(TOP_K):
        lo, hi = _unpack_bf16_pair(buf_scr[k])
        acc = acc + gates[:, k:k + 1] * jnp.concatenate([lo, hi], axis=1)
    out_ref[...] = _layernorm(ALPHA * x1_ref[...] + acc, g_ref[...], b_ref[...])


def _combine(pos, ys, gate_t, x1, sh_out, ln2_g, ln2_b, tile):
    n_tok = x1.shape[0]
    half = D_MODEL // 2
    c2 = lambda i: (0, 0)
    return pl.pallas_call(
        _combine_body,
        grid=(n_tok // tile,),
        in_specs=[
            pl.BlockSpec((TOP_K, tile), lambda i: (0, i), memory_space=pltpu.SMEM),
            pl.BlockSpec(memory_space=pl.ANY),
            pl.BlockSpec((tile, LANES), lambda i: (i, 0)),
            pl.BlockSpec((tile, D_MODEL), lambda i: (i, 0)),
            pl.BlockSpec((tile, D_MODEL), lambda i: (i, 0)),
            pl.BlockSpec((1, D_MODEL), c2),
            pl.BlockSpec((1, D_MODEL), c2),
        ],
        out_specs=pl.BlockSpec((tile, D_MODEL), lambda i: (i, 0)),
        out_shape=jax.ShapeDtypeStruct((n_tok, D_MODEL), F32),
        scratch_shapes=[
            pltpu.VMEM((TOP_K, tile, half), U32),
            pltpu.SemaphoreType.DMA,
        ],
        compiler_params=_params(("arbitrary",)),
        name="combine",
    )(pos, ys, gate_t, x1, sh_out, ln2_g, ln2_b)


def _pick_tile(n, cap, mult):
    best = mult
    for t in range(mult, cap + 1, mult):
        if n % t == 0:
            best = t
    assert n % best == 0
    return best


def kernel(x_prompt, x_sample, state_rglru_h, state_conv, state_s5_re, state_s5_im, meta_tokens, ln_in_g, ln_in_b, w_in, b_in, conv_w, conv_b, rg_wa, rg_ba, rg_wi, rg_bi, rg_lambda, s5_a_re, s5_a_im, s5_b_re, s5_b_im, s5_c_re, s5_c_im, s5_d, s5_log_dt, glu_w, glu_b, proj_a, proj_b, w_o, ln1_g, ln1_b, router_w, router_bias, ex_w1, ex_w3, ex_w2, sh_w1, sh_w3, sh_w2, ln2_g, ln2_b):
    bp, seq, d = x_prompt.shape
    n_s = x_sample.shape[0]
    assert bp == NSEQ and d == D_MODEL and x_sample.shape[1] == 1
    assert w_in.shape[0] == DEPTH
    n_prompt = bp * seq
    n_tok = n_prompt + n_s
    meta_rows = NSEQ * N_META
    n1 = n_tok + 2 * meta_rows
    row = lambda v: v.reshape(1, -1)

    xflat = jnp.concatenate([
        jnp.transpose(x_prompt, (1, 0, 2)).reshape(n_prompt, d),
        x_sample.reshape(n_s, d),
        jnp.repeat(meta_tokens, NSEQ, axis=0),
        jnp.zeros((meta_rows, d), F32),
    ], axis=0)

    tile1 = _pick_tile(n1, 1056, 2 * SUBLANES)
    xa, z16 = _inproj(xflat, row(ln_in_g), row(ln_in_b), w_in[0].astype(BF16), b_in, tile1)

    a_r, a_i, bb_r, bb_i = _s5_prep(s5_a_re[0], s5_a_im[0], s5_log_dt[0], s5_b_re[0], s5_b_im[0])
    bg = S5_BLOCK_GROUPS
    bd_b = jnp.concatenate([
        _block_diag(bb_r.reshape(S5_BLOCKS, bg, S5_CH, S5_N)),
        _block_diag(bb_i.reshape(S5_BLOCKS, bg, S5_CH, S5_N)),
    ], axis=2).astype(BF16)
    c_t = lambda c: jnp.transpose(c[0], (0, 2, 1)).reshape(S5_BLOCKS, bg, S5_N, S5_CH)
    bd_cre = _block_diag(c_t(s5_c_re)).astype(BF16)
    bd_cim = _block_diag(c_t(s5_c_im)).astype(BF16)
    heads_per_blk = RG_HEADS // (D_RNN // MXU_DIM)
    rg_blk = lambda w: _block_diag(
        w[0].reshape(D_RNN // MXU_DIM, heads_per_blk, D_RNN // RG_HEADS, D_RNN // RG_HEADS)
    ).astype(BF16)
    weights = (conv_w[0], conv_b, rg_blk(rg_wa), rg_blk(rg_wi), rg_ba, rg_bi, rg_lambda,
               bd_b, bd_cre, bd_cim, row(s5_d[0]), glu_w[0].astype(BF16), glu_b)

    chunk = n_s + 2 * meta_rows
    assert n_prompt % chunk == 0
    ua, gl, p_h, p_conv, p_s5r, p_s5i, s_h, s_conv, s_s5r, s_s5i = _mixer(
        xa, z16, a_r, a_i, state_rglru_h[0],
        state_conv[0].reshape(n_s, (CONV_W - 1) * D_RNN),
        state_s5_re[0].reshape(n_s, S5_STATE), state_s5_im[0].reshape(n_s, S5_STATE),
        weights, n_prompt, n_s, n_tok, meta_rows, chunk)

    wr_t = jnp.transpose(router_w[0])
    wr_hi = wr_t.astype(BF16)
    wr_lo = (wr_t - wr_hi.astype(F32)).astype(BF16)
    tile3 = _pick_tile(n_tok, POST_TILE, 2 * SUBLANES)
    x1 = _post_mixer(
        ua, gl, z16, xflat, row(ln_in_g), row(ln_in_b), proj_a[0].astype(BF16),
        proj_b[0].astype(BF16), w_o[0].astype(BF16), ln1_g, ln1_b, n_tok, tile3)
    tile4 = _pick_tile(n_tok, TOKEN_TILE, LANES)
    x1p, eidx, rank, gate_t, cnt = _router(
        x1, wr_hi, wr_lo, router_bias[0].reshape(N_EXPERTS, 1), tile4)

    rows_max = n_tok * TOP_K + N_EXPERTS * (MOE_TILE - 1)
    n_tiles_max = -(-rows_max // MOE_TILE)
    pos, texp, info = _positions(eidx, rank, cnt, n_tiles_max)
    xs, sh_out = _dispatch(info, pos, x1p, sh_w1[0].astype(BF16), sh_w3[0].astype(BF16),
                           sh_w2[0].astype(BF16), n_tiles_max * MOE_TILE, tile4)
    ys = _moe(texp, info, xs, ex_w1[0], ex_w3[0], ex_w2[0], n_tiles_max)
    y = _combine(pos, ys, gate_t, x1, sh_out, ln2_g, ln2_b, _pick_tile(n_tok, FIN_TILE, LANES))

    dt = x_prompt.dtype
    y_prompt = jnp.transpose(y[:n_prompt].reshape(seq, bp, d), (1, 0, 2))
    y_sample = y[n_prompt:].reshape(n_s, 1, d)
    conv_p = jnp.transpose(p_conv.reshape(CONV_W - 1, bp, D_RNN), (1, 0, 2))
    s5_shape = (S5_GROUPS, S5_N)
    return (y_prompt.astype(dt), y_sample.astype(dt),
            p_h[None], conv_p[None],
            p_s5r.reshape(1, bp, *s5_shape), p_s5i.reshape(1, bp, *s5_shape),
            s_h[None], s_conv.reshape(1, n_s, CONV_W - 1, D_RNN),
            s_s5r.reshape(1, n_s, *s5_shape), s_s5i.reshape(1, n_s, *s5_shape))
```

```python
import functools
import math

import jax
import jax.numpy as jnp
from jax import lax
from jax.experimental import pallas as pl
from jax.experimental.pallas import tpu as pltpu

F32 = jnp.float32
BF16 = jnp.bfloat16
I32 = jnp.int32

D_MODEL = 2048
D_RNN = D_MODEL // 2
D_S5 = D_MODEL // 2
N_IN = 2 * D_RNN + D_S5 + 2 * D_MODEL
RG_HEADS = 8
CONV_W = 4
LRU_C = 8.0
S5_CH = 16
S5_GROUPS = D_S5 // S5_CH
S5_N = 64
S5_STATE = S5_GROUPS * S5_N
N_EXPERTS = 64
TOP_K = 8
N_GROUPS = 8
GROUP_SIZE = N_EXPERTS // N_GROUPS
TOPK_GROUPS = 4
D_EXPERT = 512
ROUTED_SCALE = 2.5
LN_EPS = 1e-5
N_META = 16
DEPTH = 1
ALPHA = (2.0 * DEPTH) ** 0.25

SUBLANES = 8
LANES = 128
MXU_DIM = 256
VMEM_LIMIT = 56 * 1024 * 1024

NSEQ = 4
S5_BLOCK_GROUPS = MXU_DIM // S5_CH
S5_BLOCKS = S5_GROUPS // S5_BLOCK_GROUPS
S5_BLOCK_STATE = S5_BLOCK_GROUPS * S5_N
IN_TILE_N = 512
POST_TILE = 320
TOKEN_TILE = 640
MOE_TILE = 256
FIN_TILE = 128


def _params(sem, vmem=VMEM_LIMIT):
    return pltpu.CompilerParams(dimension_semantics=sem, vmem_limit_bytes=vmem)


def _dot(a, b):
    return jnp.dot(a, b, preferred_element_type=F32)


def _layernorm(x, g, b):
    mu = jnp.mean(x, axis=-1, keepdims=True)
    xc = x - mu
    var = jnp.mean(xc * xc, axis=-1, keepdims=True)
    return xc * lax.rsqrt(var + LN_EPS) * g + b


def _sigmoid(x):
    return 1.0 / (1.0 + jnp.exp(-x))


def _gelu(x):
    c = math.sqrt(2.0 / math.pi)
    return 0.5 * x * (1.0 + jnp.tanh(c * (x + 0.044715 * (x * x * x))))


def _silu(x):
    return x * _sigmoid(x)


def _softplus(x):
    return jnp.maximum(x, 0.0) + jnp.log1p(jnp.exp(-jnp.abs(x)))


def _neg_expm1(x):
    poly = x * (1.0 + x * (1.0 / 2) * (1.0 + x * (1.0 / 3) * (1.0 + x * (1.0 / 4) * (
        1.0 + x * (1.0 / 5) * (1.0 + x * (1.0 / 6) * (1.0 + x * (1.0 / 7)))))))
    return -jnp.where(x > -0.25, poly, jnp.exp(x) - 1.0)


def _s5_prep_body(are_ref, aim_ref, ldt_ref, bre_ref, bim_ref,
                  abr_ref, abi_ref, bbr_ref, bbi_ref):
    a_re = are_ref[...]
    a_im = aim_ref[...]
    dt = jnp.exp(ldt_ref[...])
    mag = jnp.exp(a_re * dt)
    ab_r = mag * jnp.cos(a_im * dt)
    ab_i = mag * jnp.sin(a_im * dt)
    den = a_re * a_re + a_im * a_im
    nr = ab_r - 1.0
    cr = (nr * a_re + ab_i * a_im) / den
    ci = (ab_i * a_re - nr * a_im) / den
    b_re = bre_ref[...]
    b_im = bim_ref[...]
    abr_ref[...] = ab_r
    abi_ref[...] = ab_i
    bbr_ref[...] = cr * b_re - ci * b_im
    bbi_ref[...] = cr * b_im + ci * b_re


def _s5_prep(a_re, a_im, log_dt, b_re, b_im):
    g, n, c = b_re.shape
    wide = c * n
    bc = lambda v: jnp.broadcast_to(v[:, None, :], (g, c, n)).reshape(g, wide)
    are_x = bc(a_re)
    aim_x = bc(a_im)
    ldt_x = jnp.broadcast_to(log_dt[:, None], (g, wide))
    bre_x = jnp.transpose(b_re, (0, 2, 1)).reshape(g, wide)
    bim_x = jnp.transpose(b_im, (0, 2, 1)).reshape(g, wide)
    shp = jax.ShapeDtypeStruct((g, wide), F32)
    abr, abi, bbr, bbi = pl.pallas_call(
        _s5_prep_body, out_shape=(shp, shp, shp, shp), name="s5_prep",
    )(are_x, aim_x, ldt_x, bre_x, bim_x)
    a_r = abr[:, :n].reshape(1, g * n)
    a_i = abi[:, :n].reshape(1, g * n)
    return a_r, a_i, bbr.reshape(g, c, n), bbi.reshape(g, c, n)


def _block_diag(x):
    k, g, a, b = x.shape
    eye = jnp.eye(g, dtype=x.dtype)
    return jnp.einsum("kgab,gh->kgahb", x, eye).reshape(k, g * a, g * b)


def _inproj_body(x_ref, g_ref, b_ref, w_ref, bias_ref, xa_ref, z_ref, xn_scr):
    j = pl.program_id(1)

    @pl.when(j == 0)
    def _():
        xn_scr[...] = _layernorm(x_ref[...], g_ref[...], b_ref[...]).astype(BF16)

    z = _dot(xn_scr[...], w_ref[...]) + bias_ref[...]
    n_xa = D_RNN // IN_TILE_N

    @pl.when(j < n_xa)
    def _():
        xa_ref[...] = z

    @pl.when(j >= n_xa)
    def _():
        z_ref[...] = z.astype(BF16)


def _inproj(xflat, ln_g, ln_b, w_in_bf, b_in, tile):
    n1 = xflat.shape[0]
    n_xa = D_RNN // IN_TILE_N
    grid = (n1 // tile, N_IN // IN_TILE_N)
    return pl.pallas_call(
        _inproj_body,
        grid=grid,
        in_specs=[
            pl.BlockSpec((tile, D_MODEL), lambda i, j: (i, 0)),
            pl.BlockSpec((1, D_MODEL), lambda i, j: (0, 0)),
            pl.BlockSpec((1, D_MODEL), lambda i, j: (0, 0)),
            pl.BlockSpec((D_MODEL, IN_TILE_N), lambda i, j: (0, j)),
            pl.BlockSpec((1, IN_TILE_N), lambda i, j: (0, j)),
        ],
        out_specs=[
            pl.BlockSpec((tile, IN_TILE_N), lambda i, j: (i, jnp.minimum(j, n_xa - 1))),
            pl.BlockSpec((tile, IN_TILE_N), lambda i, j: (i, jnp.maximum(j - n_xa, 0))),
        ],
        out_shape=[
            jax.ShapeDtypeStruct((n1, D_RNN), F32),
            jax.ShapeDtypeStruct((n1, N_IN - D_RNN), BF16),
        ],
        scratch_shapes=[pltpu.VMEM((tile, D_MODEL), BF16)],
        compiler_params=_params(("arbitrary", "arbitrary")),
        name="in_proj",
    )(xflat, ln_g, ln_b, w_in_bf, b_in)


def _rg_gates(xc, wa_ref, wi_ref, ba, bi, sp):
    xcb = xc.astype(BF16)
    nblk = D_RNN // MXU_DIM
    r_pre = jnp.concatenate(
        [_dot(xcb[:, k * MXU_DIM:(k + 1) * MXU_DIM], wa_ref[k]) for k in range(nblk)], axis=1)
    i_pre = jnp.concatenate(
        [_dot(xcb[:, k * MXU_DIM:(k + 1) * MXU_DIM], wi_ref[k]) for k in range(nblk)], axis=1)
    r = _sigmoid(r_pre + ba)
    i = _sigmoid(i_pre + bi)
    log_a = (-LRU_C * r) * sp
    a = jnp.exp(log_a)
    u = jnp.sqrt(_neg_expm1(2.0 * log_a)) * (i * xc)
    return a, u


def _odd_rows(width):
    return lax.broadcasted_iota(I32, (SUBLANES, width), 0) >= NSEQ


def _mixer_chunk(rows, xa_ref, ya_ref, us_ref, ua_ref, gl_ref, w, s):
    (cw_ref, cb_ref, wa_ref, wi_ref, ba_ref, bi_ref, lam_ref, bdb_ref, cre_ref, cim_ref,
     d_ref, gluw_ref, glub_ref) = w
    (ext_scr, a_scr, u_scr, hs_scr, bu_scr, hst_scr, y_scr, tail_scr, hcar_scr, s5car_scr,
     cst_scr) = s
    emit = ua_ref is not None
    halo = NSEQ * CONV_W
    ngroups = rows // SUBLANES

    ext_scr[pl.ds(0, halo), :] = tail_scr[...]
    ext_scr[pl.ds(halo, rows), :] = xa_ref[...]
    xc = cb_ref[...] + cw_ref[pl.ds(CONV_W - 1, 1), :] * ext_scr[pl.ds(halo, rows), :]
    for j in range(1, CONV_W):
        xc = xc + cw_ref[pl.ds(CONV_W - 1 - j, 1), :] * ext_scr[pl.ds(halo - NSEQ * j, rows), :]
    tail_scr[...] = ext_scr[pl.ds(rows, halo), :]

    sp = _softplus(-lam_ref[...])
    a, u = _rg_gates(xc, wa_ref, wi_ref, ba_ref[...], bi_ref[...], sp)
    a_scr[pl.ds(0, rows), :] = a
    u_scr[pl.ds(0, rows), :] = u
    odd = _odd_rows(D_RNN)

    def rg_body(g, c):
        row = pl.multiple_of(g * SUBLANES, SUBLANES)
        a_v = a_scr[pl.ds(row, SUBLANES), :]
        u_v = u_scr[pl.ds(row, SUBLANES), :]
        hl = u_v + jnp.where(odd, a_v * pltpu.roll(u_v, NSEQ, 0), 0.0)
        p = jnp.where(odd, a_v * pltpu.roll(a_v, NSEQ, 0), a_v)
        hs_scr[pl.ds(row, SUBLANES), :] = hl + p * c
        q = jnp.where(odd, hl, pltpu.roll(hl, NSEQ, 0))
        pp = jnp.where(odd, p, pltpu.roll(p, NSEQ, 0))
        return q + pp * c

    hcar_scr[...] = lax.fori_loop(0, ngroups, rg_body, hcar_scr[...])
    if emit:
        ua_ref[...] = (hs_scr[pl.ds(0, rows), :] * _gelu(ya_ref[...].astype(F32))).astype(BF16)

    odd_s = _odd_rows(S5_BLOCK_STATE)
    for kb in range(S5_BLOCKS):
        lo = kb * S5_BLOCK_STATE
        ub = us_ref[:, kb * MXU_DIM:(kb + 1) * MXU_DIM]
        bu_scr[pl.ds(0, rows), :] = _dot(ub, bdb_ref[kb])
        aor, aoi, pr, pi, a2r, a2i = [cst_scr[i, :, lo:lo + S5_BLOCK_STATE] for i in range(6)]

        def s5_body(g, c, aor=aor, aoi=aoi, pr=pr, pi=pi, a2r=a2r, a2i=a2i):
            cr, ci = c
            row = pl.multiple_of(g * SUBLANES, SUBLANES)
            bur = bu_scr[pl.ds(row, SUBLANES), 0:S5_BLOCK_STATE]
            bui = bu_scr[pl.ds(row, SUBLANES), S5_BLOCK_STATE:2 * S5_BLOCK_STATE]
            sr = pltpu.roll(bur, NSEQ, 0)
            si = pltpu.roll(bui, NSEQ, 0)
            hlr = bur + aor * sr - aoi * si
            hli = bui + aor * si + aoi * sr
            if emit:
                hst_scr[pl.ds(row, SUBLANES), 0:S5_BLOCK_STATE] = hlr + pr * cr - pi * ci
                hst_scr[pl.ds(row, SUBLANES), S5_BLOCK_STATE:2 * S5_BLOCK_STATE] = (
                    hli + pr * ci + pi * cr)
            qr = jnp.where(odd_s, hlr, pltpu.roll(hlr, NSEQ, 0))
            qi = jnp.where(odd_s, hli, pltpu.roll(hli, NSEQ, 0))
            return qr + a2r * cr - a2i * ci, qi + a2r * ci + a2i * cr

        cr, ci = lax.fori_loop(
            0, ngroups, s5_body,
            (s5car_scr[0, :, lo:lo + S5_BLOCK_STATE], s5car_scr[1, :, lo:lo + S5_BLOCK_STATE]))
        s5car_scr[0, :, lo:lo + S5_BLOCK_STATE] = cr
        s5car_scr[1, :, lo:lo + S5_BLOCK_STATE] = ci
        if emit:
            hre = hst_scr[pl.ds(0, rows), 0:S5_BLOCK_STATE].astype(BF16)
            him = hst_scr[pl.ds(0, rows), S5_BLOCK_STATE:2 * S5_BLOCK_STATE].astype(BF16)
            y = _dot(hre, cre_ref[kb]) - _dot(him, cim_ref[kb])
            y = y + d_ref[:, kb * MXU_DIM:(kb + 1) * MXU_DIM] * ub.astype(F32)
            y_scr[pl.ds(0, rows), kb * MXU_DIM:(kb + 1) * MXU_DIM] = y

    if emit:
        g5 = _gelu(y_scr[pl.ds(0, rows), :])
        gate = _sigmoid(_dot(g5.astype(BF16), gluw_ref[...]) + glub_ref[...])
        gl_ref[...] = (g5 * gate).astype(BF16)


def _mixer_body(meta_rows, n_s, xa_ref, ya_ref, us_ref, xam_ref, usm_ref, ar_ref, ai_ref,
                h0_ref, cbuf_ref, s5r0_ref, s5i0_ref, *rest):
    w = rest[:13]
    (ua_ref, gl_ref, hout_ref, convout_ref, s5r_ref, s5i_ref,
     h1_ref, cnew_ref, s5r1_ref, s5i1_ref) = rest[13:23]
    s = rest[23:]
    y_scr, tail_scr, hcar_scr, s5car_scr, cst_scr = s[6], s[7], s[8], s[9], s[10]
    c = pl.program_id(0)
    rows = xa_ref.shape[0]
    nchunks = pl.num_programs(0) - 1

    @pl.when(c == nchunks)
    def _():
        head = lambda r: r.at[pl.ds(0, n_s)]
        _sample_step(head(xa_ref), head(ya_ref), head(us_ref), h0_ref, cbuf_ref, s5r0_ref,
                     s5i0_ref, ar_ref, ai_ref, w, head(ua_ref), head(gl_ref), h1_ref, cnew_ref,
                     s5r1_ref, s5i1_ref, head(y_scr))
        ua_ref[pl.ds(n_s, rows - n_s), :] = jnp.zeros((rows - n_s, D_RNN), BF16)
        gl_ref[pl.ds(n_s, rows - n_s), :] = jnp.zeros((rows - n_s, D_S5), BF16)

    @pl.when(c < nchunks)
    def _():
        _prompt_step(meta_rows, nchunks, xa_ref, ya_ref, us_ref, xam_ref, usm_ref, ar_ref, ai_ref,
                     w, ua_ref, gl_ref, hout_ref, convout_ref, s5r_ref, s5i_ref, s)


def _prompt_step(meta_rows, nchunks, xa_ref, ya_ref, us_ref, xam_ref, usm_ref, ar_ref, ai_ref,
                 w, ua_ref, gl_ref, hout_ref, convout_ref, s5r_ref, s5i_ref, s):
    tail_scr, hcar_scr, s5car_scr, cst_scr = s[7], s[8], s[9], s[10]
    c = pl.program_id(0)
    rows = xa_ref.shape[0]

    @pl.when(c == 0)
    def _():
        odd = _odd_rows(S5_STATE)
        ar = jnp.broadcast_to(ar_ref[...], (SUBLANES, S5_STATE))
        ai = jnp.broadcast_to(ai_ref[...], (SUBLANES, S5_STATE))
        a2r = ar * ar - ai * ai
        a2i = 2.0 * (ar * ai)
        cst_scr[0] = jnp.where(odd, ar, 0.0)
        cst_scr[1] = jnp.where(odd, ai, 0.0)
        cst_scr[2] = jnp.where(odd, a2r, ar)
        cst_scr[3] = jnp.where(odd, a2i, ai)
        cst_scr[4] = a2r
        cst_scr[5] = a2i
        tail_scr[...] = jnp.zeros_like(tail_scr)
        hcar_scr[...] = jnp.zeros_like(hcar_scr)
        s5car_scr[...] = jnp.zeros_like(s5car_scr)
        _mixer_chunk(meta_rows, xam_ref, None, usm_ref, None, None, w, s)

    _mixer_chunk(rows, xa_ref, ya_ref, us_ref, ua_ref, gl_ref, w, s)

    @pl.when(c == nchunks - 1)
    def _():
        hout_ref[...] = hcar_scr[pl.ds(NSEQ, NSEQ), :]
        convout_ref[...] = tail_scr[pl.ds(NSEQ, NSEQ * (CONV_W - 1)), :]
        s5r_ref[...] = s5car_scr[0, pl.ds(NSEQ, NSEQ), :]
        s5i_ref[...] = s5car_scr[1, pl.ds(NSEQ, NSEQ), :]


def _mixer_weight_specs(nidx):
    z = (0,) * nidx if nidx else ()
    c2 = lambda *_: (0, 0)
    c3 = lambda *_: (0, 0, 0)
    nblk = D_RNN // MXU_DIM
    return [
        pl.BlockSpec((CONV_W, D_RNN), c2),
        pl.BlockSpec((1, D_RNN), c2),
        pl.BlockSpec((nblk, MXU_DIM, MXU_DIM), c3),
        pl.BlockSpec((nblk, MXU_DIM, MXU_DIM), c3),
        pl.BlockSpec((1, D_RNN), c2),
        pl.BlockSpec((1, D_RNN), c2),
        pl.BlockSpec((1, D_RNN), c2),
        pl.BlockSpec((S5_BLOCKS, MXU_DIM, 2 * S5_BLOCK_STATE), c3),
        pl.BlockSpec((S5_BLOCKS, S5_BLOCK_STATE, MXU_DIM), c3),
        pl.BlockSpec((S5_BLOCKS, S5_BLOCK_STATE, MXU_DIM), c3),
        pl.BlockSpec((1, D_S5), c2),
        pl.BlockSpec((D_S5, D_S5), c2),
        pl.BlockSpec((1, D_S5), c2),
    ]


def _mixer(xa, z16, a_r, a_i, h0, cbuf, s5r0, s5i0, weights, n_prompt, n_s, meta_row0, meta_rows,
           chunk):
    nchunks = n_prompt // chunk
    n1 = xa.shape[0]
    assert n1 == n_prompt + chunk and n_s <= chunk
    meta_blk = meta_row0 // meta_rows
    halo = NSEQ * CONV_W
    c2 = lambda c: (0, 0)
    in_specs = [
        pl.BlockSpec((chunk, D_RNN), lambda c: (c, 0)),
        pl.BlockSpec((chunk, D_RNN), lambda c: (c, 0)),
        pl.BlockSpec((chunk, D_S5), lambda c: (c, 1)),
        pl.BlockSpec((meta_rows, D_RNN), lambda c: (meta_blk, 0)),
        pl.BlockSpec((meta_rows, D_S5), lambda c: (meta_blk, 1)),
        pl.BlockSpec((1, S5_STATE), c2),
        pl.BlockSpec((1, S5_STATE), c2),
        pl.BlockSpec((n_s, D_RNN), c2),
        pl.BlockSpec((n_s, (CONV_W - 1) * D_RNN), c2),
        pl.BlockSpec((n_s, S5_STATE), c2),
        pl.BlockSpec((n_s, S5_STATE), c2),
    ] + _mixer_weight_specs(1)
    out_specs = [
        pl.BlockSpec((chunk, D_RNN), lambda c: (c, 0)),
        pl.BlockSpec((chunk, D_S5), lambda c: (c, 0)),
        pl.BlockSpec((NSEQ, D_RNN), c2),
        pl.BlockSpec((NSEQ * (CONV_W - 1), D_RNN), c2),
        pl.BlockSpec((NSEQ, S5_STATE), c2),
        pl.BlockSpec((NSEQ, S5_STATE), c2),
        pl.BlockSpec((n_s, D_RNN), c2),
        pl.BlockSpec((n_s, (CONV_W - 1) * D_RNN), c2),
        pl.BlockSpec((n_s, S5_STATE), c2),
        pl.BlockSpec((n_s, S5_STATE), c2),
    ]
    out_shape = [
        jax.ShapeDtypeStruct((n1, D_RNN), BF16),
        jax.ShapeDtypeStruct((n1, D_S5), BF16),
        jax.ShapeDtypeStruct((NSEQ, D_RNN), F32),
        jax.ShapeDtypeStruct((NSEQ * (CONV_W - 1), D_RNN), F32),
        jax.ShapeDtypeStruct((NSEQ, S5_STATE), F32),
        jax.ShapeDtypeStruct((NSEQ, S5_STATE), F32),
        jax.ShapeDtypeStruct((n_s, D_RNN), F32),
        jax.ShapeDtypeStruct((n_s, (CONV_W - 1) * D_RNN), F32),
        jax.ShapeDtypeStruct((n_s, S5_STATE), F32),
        jax.ShapeDtypeStruct((n_s, S5_STATE), F32),
    ]
    scratch = [
        pltpu.VMEM((chunk + halo, D_RNN), F32),
        pltpu.VMEM((chunk, D_RNN), F32),
        pltpu.VMEM((chunk, D_RNN), F32),
        pltpu.VMEM((chunk, D_RNN), F32),
        pltpu.VMEM((chunk, 2 * S5_BLOCK_STATE), F32),
        pltpu.VMEM((chunk, 2 * S5_BLOCK_STATE), F32),
        pltpu.VMEM((chunk, D_S5), F32),
        pltpu.VMEM((halo, D_RNN), F32),
        pltpu.VMEM((SUBLANES, D_RNN), F32),
        pltpu.VMEM((2, SUBLANES, S5_STATE), F32),
        pltpu.VMEM((6, SUBLANES, S5_STATE), F32),
    ]
    return pl.pallas_call(
        functools.partial(_mixer_body, meta_rows, n_s),
        grid=(nchunks + 1,),
        in_specs=in_specs,
        out_specs=out_specs,
        out_shape=out_shape,
        scratch_shapes=scratch,
        compiler_params=_params(("arbitrary",)),
        name="mixer",
    )(xa, z16, z16, xa, z16, a_r, a_i, h0, cbuf, s5r0, s5i0, *weights)


def _sample_step(xa_ref, ya_ref, us_ref, h0_ref, cbuf_ref, s5r0_ref, s5i0_ref, ar_ref, ai_ref,
                 w, ua_ref, gl_ref, h1_ref, cnew_ref, s5r1_ref, s5i1_ref, y_scr):
    (cw_ref, cb_ref, wa_ref, wi_ref, ba_ref, bi_ref, lam_ref, bdb_ref, cre_ref, cim_ref,
     d_ref, gluw_ref, glub_ref) = w
    xa = xa_ref[...]
    xc = cb_ref[...] + cw_ref[pl.ds(CONV_W - 1, 1), :] * xa
    for k in range(CONV_W - 1):
        xc = xc + cw_ref[pl.ds(k, 1), :] * cbuf_ref[:, k * D_RNN:(k + 1) * D_RNN]
    for k in range(CONV_W - 2):
        cnew_ref[:, k * D_RNN:(k + 1) * D_RNN] = cbuf_ref[:, (k + 1) * D_RNN:(k + 2) * D_RNN]
    cnew_ref[:, (CONV_W - 2) * D_RNN:(CONV_W - 1) * D_RNN] = xa

    sp = _softplus(-lam_ref[...])
    a, u = _rg_gates(xc, wa_ref, wi_ref, ba_ref[...], bi_ref[...], sp)
    h1 = a * h0_ref[...] + u
    h1_ref[...] = h1
    ua_ref[...] = (h1 * _gelu(ya_ref[...].astype(F32))).astype(BF16)

    for kb in range(S5_BLOCKS):
        lo = kb * S5_BLOCK_STATE
        ub = us_ref[:, kb * MXU_DIM:(kb + 1) * MXU_DIM]
        bu = _dot(ub, bdb_ref[kb])
        ar = ar_ref[:, lo:lo + S5_BLOCK_STATE]
        ai = ai_ref[:, lo:lo + S5_BLOCK_STATE]
        h0r = s5r0_ref[:, lo:lo + S5_BLOCK_STATE]
        h0i = s5i0_ref[:, lo:lo + S5_BLOCK_STATE]
        hr = bu[:, 0:S5_BLOCK_STATE] + ar * h0r - ai * h0i
        hi = bu[:, S5_BLOCK_STATE:2 * S5_BLOCK_STATE] + ar * h0i + ai * h0r
        s5r1_ref[:, lo:lo + S5_BLOCK_STATE] = hr
        s5i1_ref[:, lo:lo + S5_BLOCK_STATE] = hi
        y = _dot(hr.astype(BF16), cre_ref[kb]) - _dot(hi.astype(BF16), cim_ref[kb])
        y_scr[:, kb * MXU_DIM:(kb + 1) * MXU_DIM] = (
            y + d_ref[:, kb * MXU_DIM:(kb + 1) * MXU_DIM] * ub.astype(F32))

    g5 = _gelu(y_scr[...])
    gate = _sigmoid(_dot(g5.astype(BF16), gluw_ref[...]) + glub_ref[...])
    gl_ref[...] = (g5 * gate).astype(BF16)


def _col_min(x):
    return jnp.min(x, axis=0, keepdims=True)


def _col_max(x):
    return jnp.max(x, axis=0, keepdims=True)


def _col_sum(x):
    return jnp.sum(x, axis=0, keepdims=True)


def _route(sel, scores, tm):
    neg = -jnp.inf
    iota = lax.broadcasted_iota(I32, (GROUP_SIZE, tm), 0)
    sel_b = [sel[g * GROUP_SIZE:(g + 1) * GROUP_SIZE, :] for g in range(N_GROUPS)]
    sc_b = [scores[g * GROUP_SIZE:(g + 1) * GROUP_SIZE, :] for g in range(N_GROUPS)]

    iota_g = lax.broadcasted_iota(I32, (N_GROUPS, tm), 0)
    gs = jnp.zeros((N_GROUPS, tm), F32)
    for g in range(N_GROUPS):
        b = sel_b[g]
        m1 = _col_max(b)
        i1 = _col_min(jnp.where(b == m1, iota, GROUP_SIZE))
        m2 = _col_max(jnp.where(iota == i1, neg, b))
        gs = jnp.where(iota_g == g, m1 + m2, gs)

    keep = jnp.zeros((N_GROUPS, tm), I32)
    work = gs
    for _ in range(TOPK_GROUPS):
        m = _col_max(work)
        idx = _col_min(jnp.where(work == m, iota_g, N_GROUPS))
        hit = iota_g == idx
        keep = jnp.where(hit, 1, keep)
        work = jnp.where(hit, neg, work)

    cand = [jnp.where(keep[g:g + 1, :] > 0, sel_b[g], neg) for g in range(N_GROUPS)]
    ids, vals = [], []
    for _ in range(TOP_K):
        m = _col_max(cand[0])
        for g in range(1, N_GROUPS):
            m = jnp.maximum(m, _col_max(cand[g]))
        idx = _col_min(jnp.where(cand[0] == m, iota, N_EXPERTS))
        for g in range(1, N_GROUPS):
            idx = jnp.minimum(
                idx, _col_min(jnp.where(cand[g] == m, iota + g * GROUP_SIZE, N_EXPERTS)))
        val = jnp.zeros((1, tm), F32)
        for g in range(N_GROUPS):
            hit = (iota + g * GROUP_SIZE) == idx
            val = val + _col_sum(jnp.where(hit, sc_b[g], 0.0))
            cand[g] = jnp.where(hit, neg, cand[g])
        ids.append(idx)
        vals.append(val)
    return ids, vals


def _post_body(ua_ref, gl_ref, ga_ref, gb_ref, x_ref, ling_ref, linb_ref, pa_ref, pb_ref,
               wo_ref, l1g_ref, l1b_ref, x1_ref):
    branch_a = _dot(ua_ref[...], pa_ref[...])
    branch_b = _dot(gl_ref[...], pb_ref[...])
    merged = (_sigmoid(ga_ref[...].astype(F32)) * branch_a
              + _sigmoid(gb_ref[...].astype(F32)) * branch_b)
    o = _dot(merged.astype(BF16), wo_ref[...])
    xn = _layernorm(x_ref[...], ling_ref[...], linb_ref[...])
    x1_ref[...] = _layernorm(ALPHA * xn + o, l1g_ref[...], l1b_ref[...])


def _post_mixer(ua, gl, z16, xflat, ln_in_g, ln_in_b, proj_a, proj_b, w_o, ln1_g, ln1_b,
                n_tok, tile):
    c2 = lambda i: (0, 0)
    in_specs = [
        pl.BlockSpec((tile, D_RNN), lambda i: (i, 0)),
        pl.BlockSpec((tile, D_S5), lambda i: (i, 0)),
        pl.BlockSpec((tile, D_MODEL), lambda i: (i, 1)),
        pl.BlockSpec((tile, D_MODEL), lambda i: (i, 2)),
        pl.BlockSpec((tile, D_MODEL), lambda i: (i, 0)),
        pl.BlockSpec((1, D_MODEL), c2),
        pl.BlockSpec((1, D_MODEL), c2),
        pl.BlockSpec((D_RNN, D_MODEL), c2),
        pl.BlockSpec((D_S5, D_MODEL), c2),
        pl.BlockSpec((D_MODEL, D_MODEL), c2),
        pl.BlockSpec((1, D_MODEL), c2),
        pl.BlockSpec((1, D_MODEL), c2),
    ]
    return pl.pallas_call(
        _post_body,
        grid=(n_tok // tile,),
        in_specs=in_specs,
        out_specs=pl.BlockSpec((tile, D_MODEL), lambda i: (i, 0)),
        out_shape=jax.ShapeDtypeStruct((n_tok, D_MODEL), F32),
        compiler_params=_params(("arbitrary",)),
        name="post_mixer",
    )(ua, gl, z16, z16, xflat, ln_in_g, ln_in_b, proj_a, proj_b, w_o, ln1_g, ln1_b)


def _router_body(x1_ref, wrh_ref, wrl_ref, rb_ref,
                 eidx_ref, rank_ref, gatet_ref, cnt_ref, cnt_scr):
    i = pl.program_id(0)
    tm = x1_ref.shape[0]

    @pl.when(i == 0)
    def _():
        cnt_scr[...] = jnp.zeros_like(cnt_scr)

    x1 = x1_ref[...]

    x_hi = x1.astype(BF16)
    x_lo = (x1 - x_hi.astype(F32)).astype(BF16)
    nt = (((1,), (1,)), ((), ()))
    dg = lambda a, b: lax.dot_general(a, b, nt, preferred_element_type=F32)
    logits = dg(wrh_ref[...], x_hi) + dg(wrh_ref[...], x_lo) + dg(wrl_ref[...], x_hi)
    scores = _sigmoid(logits)
    sel = scores + rb_ref[...]
    ids, vals = _route(sel, scores, tm)

    total = vals[0]
    for v in vals[1:]:
        total = total + v
    iota_k = lax.broadcasted_iota(I32, (TOP_K, tm), 0)
    iota_e = lax.broadcasted_iota(I32, (N_EXPERTS, tm), 0)
    eidx = jnp.zeros((TOP_K, tm), I32)
    gates = jnp.zeros((TOP_K, tm), F32)
    selm = jnp.zeros((N_EXPERTS, tm), F32)
    for k in range(TOP_K):
        eidx = jnp.where(iota_k == k, ids[k], eidx)
        gates = jnp.where(iota_k == k, vals[k] / total * ROUTED_SCALE, gates)
        selm = jnp.where(iota_e == ids[k], 1.0, selm)
    eidx_ref[...] = eidx

    r_i = lax.broadcasted_iota(I32, (tm, tm), 0)
    c_i = lax.broadcasted_iota(I32, (tm, tm), 1)
    upper = jnp.where(r_i < c_i, 1.0, 0.0).astype(BF16)
    rank_all = _dot(selm.astype(BF16), upper) + cnt_scr[...]
    rank = jnp.zeros((TOP_K, tm), F32)
    for k in range(TOP_K):
        rk = _col_sum(jnp.where(iota_e == ids[k], rank_all, 0.0))
        rank = jnp.where(iota_k == k, rk, rank)
    rank_ref[...] = rank.astype(I32)
    cnt_scr[...] = cnt_scr[...] + jnp.sum(selm, axis=1, keepdims=True)
    cnt_ref[...] = cnt_scr[...]

    gpad = jnp.concatenate([gates, jnp.zeros((LANES - TOP_K, tm), F32)], axis=0)
    gatet_ref[...] = gpad.T


def _router(x1, wr_hi, wr_lo, rbias, tile):
    n_tok = x1.shape[0]
    c2 = lambda i: (0, 0)
    return pl.pallas_call(
        _router_body,
        grid=(n_tok // tile,),
        in_specs=[
            pl.BlockSpec((tile, D_MODEL), lambda i: (i, 0)),
            pl.BlockSpec((N_EXPERTS, D_MODEL), c2),
            pl.BlockSpec((N_EXPERTS, D_MODEL), c2),
            pl.BlockSpec((N_EXPERTS, 1), c2),
        ],
        out_specs=[
            pl.BlockSpec((TOP_K, tile), lambda i: (0, i)),
            pl.BlockSpec((TOP_K, tile), lambda i: (0, i)),
            pl.BlockSpec((tile, LANES), lambda i: (i, 0)),
            pl.BlockSpec((N_EXPERTS, 1), c2),
        ],
        out_shape=[
            jax.ShapeDtypeStruct((TOP_K, n_tok), I32),
            jax.ShapeDtypeStruct((TOP_K, n_tok), I32),
            jax.ShapeDtypeStruct((n_tok, LANES), F32),
            jax.ShapeDtypeStruct((N_EXPERTS, 1), F32),
        ],
        scratch_shapes=[pltpu.VMEM((N_EXPERTS, 1), F32)],
        compiler_params=_params(("arbitrary",)),
        name="router",
    )(x1, wr_hi, wr_lo, rbias)


def _positions_body(eidx_ref, rank_ref, cnt_ref, pos_ref, texp_ref, info_ref):
    cnt = cnt_ref[...]
    padded = jnp.floor((cnt + (MOE_TILE - 1)) * (1.0 / MOE_TILE)) * MOE_TILE
    r_i = lax.broadcasted_iota(I32, (N_EXPERTS, N_EXPERTS), 0)
    c_i = lax.broadcasted_iota(I32, (N_EXPERTS, N_EXPERTS), 1)
    eye = r_i == c_i
    as_row = lambda col: jnp.sum(jnp.where(eye, col, 0.0), axis=0, keepdims=True)
    padded_row = as_row(padded)
    base = jnp.sum(jnp.where(c_i < r_i, padded_row, 0.0), axis=1, keepdims=True)
    end = base + padded

    eidx = eidx_ref[...]
    pos = rank_ref[...]
    base_i = base.astype(I32)
    for e in range(N_EXPERTS):
        pos = pos + jnp.where(eidx == e, base_i[e:e + 1, :], 0)
    pos_ref[...] = pos

    ntp = texp_ref.shape[1]
    t_row = lax.broadcasted_iota(I32, (1, ntp), 1).astype(F32) * MOE_TILE
    texp = jnp.sum(jnp.where(end <= t_row, 1, 0), axis=0, keepdims=True)
    texp_ref[...] = jnp.minimum(texp, N_EXPERTS - 1).astype(I32)

    row = lax.broadcasted_iota(I32, (SUBLANES, LANES), 0)
    lane = lax.broadcasted_iota(I32, (SUBLANES, LANES), 1)
    pad_lanes = lambda r: jnp.concatenate(
        [r, jnp.zeros((1, LANES - N_EXPERTS), F32)], axis=1)
    ntiles = jnp.sum(padded, axis=0, keepdims=True) * (1.0 / MOE_TILE)
    used_row = as_row(cnt) > 0.0
    c_f = c_i.astype(F32)
    nxt = jnp.min(jnp.where((c_i > r_i) & used_row, c_f, float(N_EXPERTS)), axis=1, keepdims=True)
    order = jnp.sum(jnp.where((c_i < r_i) & used_row, 1.0, 0.0), axis=1, keepdims=True)
    slot = order - 2.0 * jnp.floor(order * 0.5)
    info = jnp.where(row == 0, pad_lanes(as_row(cnt)), 0.0)
    info = jnp.where(row == 1, pad_lanes(as_row(base)), info)
    info = jnp.where(row == 2, pad_lanes(as_row(end)), info)
    info = jnp.where((row == 3) & (lane == 0), ntiles, info)
    info = jnp.where(row == 4, pad_lanes(as_row(nxt)), info)
    info = jnp.where(row == 5, pad_lanes(as_row(slot)), info)
    info_ref[...] = info.astype(I32)


def _positions(eidx, rank, cnt, n_tiles_max):
    n_tok = eidx.shape[1]
    ntp = -(-n_tiles_max // LANES) * LANES
    return pl.pallas_call(
        _positions_body,
        out_shape=[
            jax.ShapeDtypeStruct((TOP_K, n_tok), I32),
            jax.ShapeDtypeStruct((1, ntp), I32),
            jax.ShapeDtypeStruct((SUBLANES, LANES), I32),
        ],
        compiler_params=pltpu.CompilerParams(vmem_limit_bytes=VMEM_LIMIT),
        name="positions",
    )(eidx, rank, cnt)


def _row_copy(src_hbm, src_row, dst_hbm, dst_row, sem):
    return pltpu.make_async_copy(
        src_hbm.at[pl.ds(src_row, 1)], dst_hbm.at[pl.ds(dst_row, 1)], sem)


def _swiglu_packed(x, w1_ref, w3_ref, w2_ref):
    xb = x.astype(BF16)
    h = (_silu(_dot(xb, w1_ref[...])) * _dot(xb, w3_ref[...])).astype(BF16)
    return _dot(h, w2_ref[...])


def _dispatch_body(info_ref, pos_ref, xp_ref, w1_ref, w3_ref, w2_ref,
                   xs_hbm, sh_ref, zero_scr, sem):
    i = pl.program_id(0)
    tm = xp_ref.shape[0]

    def issue(t, carry):
        for k in range(TOP_K):
            _row_copy(xp_ref, t, xs_hbm, pos_ref[k, t], sem).start(priority=k % 2)
        return carry

    lax.fori_loop(0, tm, issue, 0)
    sh_ref[...] = _swiglu_packed(xp_ref[...], w1_ref, w3_ref, w2_ref)

    for k in range(TOP_K):
        pltpu.make_async_copy(xp_ref, xs_hbm.at[pl.ds(0, tm)], sem).wait()

    @pl.when(i == pl.num_programs(0) - 1)
    def _():
        zero_scr[...] = jnp.zeros_like(zero_scr)

        def per_expert(e, carry):
            start = info_ref[1, e] + info_ref[0, e]
            stop = info_ref[2, e]

            def fill(r, c):
                _row_copy(zero_scr, 0, xs_hbm, r, sem).start()
                return c

            def fill_wait(r, c):
                _row_copy(zero_scr, 0, xs_hbm, 0, sem).wait()
                return c

            lax.fori_loop(start, stop, fill, 0)
            lax.fori_loop(start, stop, fill_wait, 0)
            return carry

        lax.fori_loop(0, N_EXPERTS, per_expert, 0)

        ntiles = info_ref[3, 0]
        tiles_alloc = xs_hbm.shape[0] // MOE_TILE

        def tile_copy(j):
            return pltpu.make_async_copy(
                zero_scr, xs_hbm.at[pl.ds(pl.multiple_of(j * MOE_TILE, MOE_TILE), MOE_TILE)], sem)

        def fill_tile(j, c):
            tile_copy(j).start()
            return c

        def fill_tile_wait(j, c):
            tile_copy(j).wait()
            return c

        lax.fori_loop(ntiles, tiles_alloc, fill_tile, 0)
        lax.fori_loop(ntiles, tiles_alloc, fill_tile_wait, 0)


def _dispatch(info, pos, x1, sh_w1, sh_w3, sh_w2, rows_sorted, tile):
    n_tok, half = x1.shape
    c2 = lambda i, info: (0, 0)
    grid_spec = pltpu.PrefetchScalarGridSpec(
        num_scalar_prefetch=1,
        grid=(n_tok // tile,),
        in_specs=[
            pl.BlockSpec((TOP_K, tile), lambda i, info: (0, i), memory_space=pltpu.SMEM),
            pl.BlockSpec((tile, half), lambda i, info: (i, 0)),
            pl.BlockSpec((D_MODEL, D_EXPERT), c2),
            pl.BlockSpec((D_MODEL, D_EXPERT), c2),
            pl.BlockSpec((D_EXPERT, D_MODEL), c2),
        ],
        out_specs=[
            pl.BlockSpec(memory_space=pl.ANY),
            pl.BlockSpec((tile, D_MODEL), lambda i, info: (i, 0)),
        ],
        scratch_shapes=[pltpu.VMEM((MOE_TILE, half), F32), pltpu.SemaphoreType.DMA],
    )
    return pl.pallas_call(
        _dispatch_body,
        grid_spec=grid_spec,
        out_shape=[
            jax.ShapeDtypeStruct((rows_sorted, half), F32),
            jax.ShapeDtypeStruct((n_tok, D_MODEL), F32),
        ],
        compiler_params=_params(("arbitrary",)),
        name="dispatch",
    )(info, pos, x1, sh_w1, sh_w3, sh_w2)


def _moe_body(texp_ref, info_ref, xs_ref, w1_hbm, w3_hbm, w2_hbm, ys_ref,
              w1f_scr, w3f_scr, w2f_scr, w1b_scr, w3b_scr, w2b_scr, sems):
    t = pl.program_id(0)
    ntiles = info_ref[3, 0]
    tt = jnp.minimum(t, ntiles - 1)
    prev = jnp.maximum(tt - 1, 0)
    expert = texp_ref[0, tt]
    fresh = (t == 0) | (expert != texp_ref[0, prev])
    valid = t < ntiles

    def weight_copies(ex, slot):
        return [
            pltpu.make_async_copy(w1_hbm.at[ex], w1f_scr.at[slot], sems.at[0, slot]),
            pltpu.make_async_copy(w3_hbm.at[ex], w3f_scr.at[slot], sems.at[1, slot]),
            pltpu.make_async_copy(w2_hbm.at[ex], w2f_scr.at[slot], sems.at[2, slot]),
        ]

    @pl.when(t == 0)
    def _():
        for c in weight_copies(expert, info_ref[5, expert]):
            c.start()

    @pl.when(valid & fresh)
    def _():
        slot = info_ref[5, expert]
        for c in weight_copies(expert, slot):
            c.wait()
        w1b_scr[...] = w1f_scr[slot].astype(BF16)
        w3b_scr[...] = w3f_scr[slot].astype(BF16)
        w2b_scr[...] = w2f_scr[slot].astype(BF16)
        nxt = info_ref[4, expert]

        @pl.when(nxt < N_EXPERTS)
        def _():
            for c in weight_copies(nxt, 1 - slot):
                c.start()

    @pl.when(valid)
    def _():
        ys_ref[...] = _swiglu_packed(xs_ref[...], w1b_scr, w3b_scr, w2b_scr)

    @pl.when(jnp.logical_not(valid))
    def _():
        ys_ref[...] = jnp.zeros_like(ys_ref)


def _moe(texp, info, xs, ex_w1, ex_w3, ex_w2, n_tiles_max):
    half = D_MODEL

    def tile_idx(t, texp, info):
        return jnp.minimum(t, info[3, 0] - 1)

    grid_spec = pltpu.PrefetchScalarGridSpec(
        num_scalar_prefetch=2,
        grid=(n_tiles_max,),
        in_specs=[
            pl.BlockSpec((MOE_TILE, half), lambda t, texp, info: (tile_idx(t, texp, info), 0)),
            pl.BlockSpec(memory_space=pl.ANY),
            pl.BlockSpec(memory_space=pl.ANY),
            pl.BlockSpec(memory_space=pl.ANY),
        ],
        out_specs=pl.BlockSpec((MOE_TILE, half), lambda t, texp, info: (t, 0)),
        scratch_shapes=[
            pltpu.VMEM((2, D_MODEL, D_EXPERT), F32),
            pltpu.VMEM((2, D_MODEL, D_EXPERT), F32),
            pltpu.VMEM((2, D_EXPERT, D_MODEL), F32),
            pltpu.VMEM((D_MODEL, D_EXPERT), BF16),
            pltpu.VMEM((D_MODEL, D_EXPERT), BF16),
            pltpu.VMEM((D_EXPERT, D_MODEL), BF16),
            pltpu.SemaphoreType.DMA((3, 2)),
        ],
    )
    return pl.pallas_call(
        _moe_body,
        grid_spec=grid_spec,
        out_shape=jax.ShapeDtypeStruct(xs.shape, F32),
        compiler_params=_params(("arbitrary",)),
        name="moe_experts",
    )(texp, info, xs, ex_w1, ex_w3, ex_w2)


def _combine_body(pos_ref, ys_hbm, gate_ref, x1_ref, sh_ref, g_ref, b_ref, out_ref,
                  buf_scr, sem):
    tm = x1_ref.shape[0]

    def issue(t, carry):
        for k in range(TOP_K):
            pltpu.make_async_copy(
                ys_hbm.at[pl.ds(pos_ref[k, t], 1)], buf_scr.at[k, pl.ds(t, 1)], sem
            ).start(priority=k % 2)
        return carry

    lax.fori_loop(0, tm, issue, 0)

    for k in range(TOP_K):
        pltpu.make_async_copy(ys_hbm.at[pl.ds(0, tm)], buf_scr.at[k], sem).wait()

    gates = gate_ref[...]
    acc = sh_ref[...]
    for k in range(TOP_K):
        acc = acc + gates[:, k:k + 1] * buf_scr[k]
    out_ref[...] = _layernorm(ALPHA * x1_ref[...] + acc, g_ref[...], b_ref[...])


def _combine(pos, ys, gate_t, x1, sh_out, ln2_g, ln2_b, tile):
    n_tok = x1.shape[0]
    c2 = lambda i: (0, 0)
    return pl.pallas_call(
        _combine_body,
        grid=(n_tok // tile,),
        in_specs=[
            pl.BlockSpec((TOP_K, tile), lambda i: (0, i), memory_space=pltpu.SMEM),
            pl.BlockSpec(memory_space=pl.ANY),
            pl.BlockSpec((tile, LANES), lambda i: (i, 0)),
            pl.BlockSpec((tile, D_MODEL), lambda i: (i, 0)),
            pl.BlockSpec((tile, D_MODEL), lambda i: (i, 0)),
            pl.BlockSpec((1, D_MODEL), c2),
            pl.BlockSpec((1, D_MODEL), c2),
        ],
        out_specs=pl.BlockSpec((tile, D_MODEL), lambda i: (i, 0)),
        out_shape=jax.ShapeDtypeStruct((n_tok, D_MODEL), F32),
        scratch_shapes=[
            pltpu.VMEM((TOP_K, tile, D_MODEL), F32),
            pltpu.SemaphoreType.DMA,
        ],
        compiler_params=_params(("arbitrary",)),
        name="combine",
    )(pos, ys, gate_t, x1, sh_out, ln2_g, ln2_b)


def _pick_tile(n, cap, mult):
    best = mult
    for t in range(mult, cap + 1, mult):
        if n % t == 0:
            best = t
    assert n % best == 0
    return best


def kernel(x_prompt, x_sample, state_rglru_h, state_conv, state_s5_re, state_s5_im, meta_tokens, ln_in_g, ln_in_b, w_in, b_in, conv_w, conv_b, rg_wa, rg_ba, rg_wi, rg_bi, rg_lambda, s5_a_re, s5_a_im, s5_b_re, s5_b_im, s5_c_re, s5_c_im, s5_d, s5_log_dt, glu_w, glu_b, proj_a, proj_b, w_o, ln1_g, ln1_b, router_w, router_bias, ex_w1, ex_w3, ex_w2, sh_w1, sh_w3, sh_w2, ln2_g, ln2_b):
    bp, seq, d = x_prompt.shape
    n_s = x_sample.shape[0]
    assert bp == NSEQ and d == D_MODEL and x_sample.shape[1] == 1
    assert w_in.shape[0] == DEPTH
    n_prompt = bp * seq
    n_tok = n_prompt + n_s
    meta_rows = NSEQ * N_META
    n1 = n_tok + 2 * meta_rows
    row = lambda v: v.reshape(1, -1)

    xflat = jnp.concatenate([
        jnp.transpose(x_prompt, (1, 0, 2)).reshape(n_prompt, d),
        x_sample.reshape(n_s, d),
        jnp.repeat(meta_tokens, NSEQ, axis=0),
        jnp.zeros((meta_rows, d), F32),
    ], axis=0)

    tile1 = _pick_tile(n1, 1056, 2 * SUBLANES)
    xa, z16 = _inproj(xflat, row(ln_in_g), row(ln_in_b), w_in[0].astype(BF16), b_in, tile1)

    a_r, a_i, bb_r, bb_i = _s5_prep(s5_a_re[0], s5_a_im[0], s5_log_dt[0], s5_b_re[0], s5_b_im[0])
    bg = S5_BLOCK_GROUPS
    bd_b = jnp.concatenate([
        _block_diag(bb_r.reshape(S5_BLOCKS, bg, S5_CH, S5_N)),
        _block_diag(bb_i.reshape(S5_BLOCKS, bg, S5_CH, S5_N)),
    ], axis=2).astype(BF16)
    c_t = lambda c: jnp.transpose(c[0], (0, 2, 1)).reshape(S5_BLOCKS, bg, S5_N, S5_CH)
    bd_cre = _block_diag(c_t(s5_c_re)).astype(BF16)
    bd_cim = _block_diag(c_t(s5_c_im)).astype(BF16)
    heads_per_blk = RG_HEADS // (D_RNN // MXU_DIM)
    rg_blk = lambda w: _block_diag(
        w[0].reshape(D_RNN // MXU_DIM, heads_per_blk, D_RNN // RG_HEADS, D_RNN // RG_HEADS)
    ).astype(BF16)
    weights = (conv_w[0], conv_b, rg_blk(rg_wa), rg_blk(rg_wi), rg_ba, rg_bi, rg_lambda,
               bd_b, bd_cre, bd_cim, row(s5_d[0]), glu_w[0].astype(BF16), glu_b)

    chunk = n_s + 2 * meta_rows
    assert n_prompt % chunk == 0
    ua, gl, p_h, p_conv, p_s5r, p_s5i, s_h, s_conv, s_s5r, s_s5i = _mixer(
        xa, z16, a_r, a_i, state_rglru_h[0],
        state_conv[0].reshape(n_s, (CONV_W - 1) * D_RNN),
        state_s5_re[0].reshape(n_s, S5_STATE), state_s5_im[0].reshape(n_s, S5_STATE),
        weights, n_prompt, n_s, n_tok, meta_rows, chunk)

    wr_t = jnp.transpose(router_w[0])
    wr_hi = wr_t.astype(BF16)
    wr_lo = (wr_t - wr_hi.astype(F32)).astype(BF16)
    tile3 = _pick_tile(n_tok, POST_TILE, 2 * SUBLANES)
    x1 = _post_mixer(
        ua, gl, z16, xflat, row(ln_in_g), row(ln_in_b), proj_a[0].astype(BF16),
        proj_b[0].astype(BF16), w_o[0].astype(BF16), ln1_g, ln1_b, n_tok, tile3)
    tile4 = _pick_tile(n_tok, TOKEN_TILE, LANES)
    eidx, rank, gate_t, cnt = _router(
        x1, wr_hi, wr_lo, router_bias[0].reshape(N_EXPERTS, 1), tile4)

    rows_max = n_tok * TOP_K + N_EXPERTS * (MOE_TILE - 1)
    n_tiles_max = -(-rows_max // MOE_TILE)
    pos, texp, info = _positions(eidx, rank, cnt, n_tiles_max)
    xs, sh_out = _dispatch(info, pos, x1, sh_w1[0].astype(BF16), sh_w3[0].astype(BF16),
                           sh_w2[0].astype(BF16), n_tiles_max * MOE_TILE, tile4)
    ys = _moe(texp, info, xs, ex_w1[0], ex_w3[0], ex_w2[0], n_tiles_max)
    y = _combine(pos, ys, gate_t, x1, sh_out, ln2_g, ln2_b, _pick_tile(n_tok, FIN_TILE, LANES))

    dt = x_prompt.dtype
    y_prompt = jnp.transpose(y[:n_prompt].reshape(seq, bp, d), (1, 0, 2))
    y_sample = y[n_prompt:].reshape(n_s, 1, d)
    conv_p = jnp.transpose(p_conv.reshape(CONV_W - 1, bp, D_RNN), (1, 0, 2))
    s5_shape = (S5_GROUPS, S5_N)
    return (y_prompt.astype(dt), y_sample.astype(dt),
            p_h[None], conv_p[None],
            p_s5r.reshape(1, bp, *s5_shape), p_s5i.reshape(1, bp, *s5_shape),
            s_h[None], s_conv.reshape(1, n_s, CONV_W - 1, D_RNN),
            s_s5r.reshape(1, n_s, *s5_shape), s_s5i.reshape(1, n_s, *s5_shape))
```

```python
import functools
import math

import jax
import jax.numpy as jnp
from jax import lax
from jax.experimental import pallas as pl
from jax.experimental.pallas import tpu as pltpu

F32 = jnp.float32
BF16 = jnp.bfloat16
I32 = jnp.int32

D_MODEL = 2048
D_RNN = D_MODEL // 2
D_S5 = D_MODEL // 2
N_IN = 2 * D_RNN + D_S5 + 2 * D_MODEL
RG_HEADS = 8
CONV_W = 4
LRU_C = 8.0
S5_CH = 16
S5_GROUPS = D_S5 // S5_CH
S5_N = 64
S5_STATE = S5_GROUPS * S5_N
N_EXPERTS = 64
TOP_K = 8
N_GROUPS = 8
GROUP_SIZE = N_EXPERTS // N_GROUPS
TOPK_GROUPS = 4
D_EXPERT = 512
ROUTED_SCALE = 2.5
LN_EPS = 1e-5
N_META = 16
DEPTH = 1
ALPHA = (2.0 * DEPTH) ** 0.25

SUBLANES = 8
LANES = 128
MXU_DIM = 256
VMEM_LIMIT = 56 * 1024 * 1024

NSEQ = 4
S5_BLOCK_GROUPS = MXU_DIM // S5_CH
S5_BLOCKS = S5_GROUPS // S5_BLOCK_GROUPS
S5_BLOCK_STATE = S5_BLOCK_GROUPS * S5_N
IN_TILE_N = 512
POST_TILE = 320
TOKEN_TILE = 640
MOE_TILE = 256
FIN_TILE = 128


def _params(sem, vmem=VMEM_LIMIT):
    return pltpu.CompilerParams(dimension_semantics=sem, vmem_limit_bytes=vmem)


def _dot(a, b):
    return jnp.dot(a, b, preferred_element_type=F32)


def _layernorm(x, g, b):
    mu = jnp.mean(x, axis=-1, keepdims=True)
    xc = x - mu
    var = jnp.mean(xc * xc, axis=-1, keepdims=True)
    return xc * lax.rsqrt(var + LN_EPS) * g + b


def _sigmoid(x):
    return 1.0 / (1.0 + jnp.exp(-x))


def _gelu(x):
    c = math.sqrt(2.0 / math.pi)
    return 0.5 * x * (1.0 + jnp.tanh(c * (x + 0.044715 * (x * x * x))))


def _silu(x):
    return x * _sigmoid(x)


def _softplus(x):
    return jnp.maximum(x, 0.0) + jnp.log1p(jnp.exp(-jnp.abs(x)))


def _neg_expm1(x):
    poly = x * (1.0 + x * (1.0 / 2) * (1.0 + x * (1.0 / 3) * (1.0 + x * (1.0 / 4) * (
        1.0 + x * (1.0 / 5) * (1.0 + x * (1.0 / 6) * (1.0 + x * (1.0 / 7)))))))
    return -jnp.where(x > -0.25, poly, jnp.exp(x) - 1.0)


def _s5_prep_body(are_ref, aim_ref, ldt_ref, bre_ref, bim_ref,
                  abr_ref, abi_ref, bbr_ref, bbi_ref):
    a_re = are_ref[...]
    a_im = aim_ref[...]
    dt = jnp.exp(ldt_ref[...])
    mag = jnp.exp(a_re * dt)
    ab_r = mag * jnp.cos(a_im * dt)
    ab_i = mag * jnp.sin(a_im * dt)
    den = a_re * a_re + a_im * a_im
    nr = ab_r - 1.0
    cr = (nr * a_re + ab_i * a_im) / den
    ci = (ab_i * a_re - nr * a_im) / den
    b_re = bre_ref[...]
    b_im = bim_ref[...]
    abr_ref[...] = ab_r
    abi_ref[...] = ab_i
    bbr_ref[...] = cr * b_re - ci * b_im
    bbi_ref[...] = cr * b_im + ci * b_re


def _s5_prep(a_re, a_im, log_dt, b_re, b_im):
    g, n, c = b_re.shape
    wide = c * n
    bc = lambda v: jnp.broadcast_to(v[:, None, :], (g, c, n)).reshape(g, wide)
    are_x = bc(a_re)
    aim_x = bc(a_im)
    ldt_x = jnp.broadcast_to(log_dt[:, None], (g, wide))
    bre_x = jnp.transpose(b_re, (0, 2, 1)).reshape(g, wide)
    bim_x = jnp.transpose(b_im, (0, 2, 1)).reshape(g, wide)
    shp = jax.ShapeDtypeStruct((g, wide), F32)
    abr, abi, bbr, bbi = pl.pallas_call(
        _s5_prep_body, out_shape=(shp, shp, shp, shp), name="s5_prep",
    )(are_x, aim_x, ldt_x, bre_x, bim_x)
    a_r = abr[:, :n].reshape(1, g * n)
    a_i = abi[:, :n].reshape(1, g * n)
    return a_r, a_i, bbr.reshape(g, c, n), bbi.reshape(g, c, n)


def _block_diag(x):
    k, g, a, b = x.shape
    eye = jnp.eye(g, dtype=x.dtype)
    return jnp.einsum("kgab,gh->kgahb", x, eye).reshape(k, g * a, g * b)


def _inproj_body(x_ref, g_ref, b_ref, w_ref, bias_ref, xa_ref, z_ref, xn_scr):
    j = pl.program_id(1)

    @pl.when(j == 0)
    def _():
        xn_scr[...] = _layernorm(x_ref[...], g_ref[...], b_ref[...]).astype(BF16)

    z = _dot(xn_scr[...], w_ref[...]) + bias_ref[...]
    n_xa = D_RNN // IN_TILE_N

    @pl.when(j < n_xa)
    def _():
        xa_ref[...] = z

    @pl.when(j >= n_xa)
    def _():
        z_ref[...] = z.astype(BF16)


def _inproj(xflat, ln_g, ln_b, w_in_bf, b_in, tile):
    n1 = xflat.shape[0]
    n_xa = D_RNN // IN_TILE_N
    grid = (n1 // tile, N_IN // IN_TILE_N)
    return pl.pallas_call(
        _inproj_body,
        grid=grid,
        in_specs=[
            pl.BlockSpec((tile, D_MODEL), lambda i, j: (i, 0)),
            pl.BlockSpec((1, D_MODEL), lambda i, j: (0, 0)),
            pl.BlockSpec((1, D_MODEL), lambda i, j: (0, 0)),
            pl.BlockSpec((D_MODEL, IN_TILE_N), lambda i, j: (0, j)),
            pl.BlockSpec((1, IN_TILE_N), lambda i, j: (0, j)),
        ],
        out_specs=[
            pl.BlockSpec((tile, IN_TILE_N), lambda i, j: (i, jnp.minimum(j, n_xa - 1))),
            pl.BlockSpec((tile, IN_TILE_N), lambda i, j: (i, jnp.maximum(j - n_xa, 0))),
        ],
        out_shape=[
            jax.ShapeDtypeStruct((n1, D_RNN), F32),
            jax.ShapeDtypeStruct((n1, N_IN - D_RNN), BF16),
        ],
        scratch_shapes=[pltpu.VMEM((tile, D_MODEL), BF16)],
        compiler_params=_params(("arbitrary", "arbitrary")),
        name="in_proj",
    )(xflat, ln_g, ln_b, w_in_bf, b_in)


def _rg_gates(xc, wa_ref, wi_ref, ba, bi, sp):
    xcb = xc.astype(BF16)
    nblk = D_RNN // MXU_DIM
    r_pre = jnp.concatenate(
        [_dot(xcb[:, k * MXU_DIM:(k + 1) * MXU_DIM], wa_ref[k]) for k in range(nblk)], axis=1)
    i_pre = jnp.concatenate(
        [_dot(xcb[:, k * MXU_DIM:(k + 1) * MXU_DIM], wi_ref[k]) for k in range(nblk)], axis=1)
    r = _sigmoid(r_pre + ba)
    i = _sigmoid(i_pre + bi)
    log_a = (-LRU_C * r) * sp
    a = jnp.exp(log_a)
    u = jnp.sqrt(_neg_expm1(2.0 * log_a)) * (i * xc)
    return a, u


def _odd_rows(width):
    return lax.broadcasted_iota(I32, (SUBLANES, width), 0) >= NSEQ


def _mixer_chunk(rows, xa_ref, ya_ref, us_ref, ua_ref, gl_ref, w, s):
    (cw_ref, cb_ref, wa_ref, wi_ref, ba_ref, bi_ref, lam_ref, bdb_ref, cre_ref, cim_ref,
     d_ref, gluw_ref, glub_ref) = w
    (ext_scr, a_scr, u_scr, hs_scr, bu_scr, hst_scr, y_scr, tail_scr, hcar_scr, s5car_scr,
     cst_scr) = s
    emit = ua_ref is not None
    halo = NSEQ * CONV_W
    ngroups = rows // SUBLANES

    ext_scr[pl.ds(0, halo), :] = tail_scr[...]
    ext_scr[pl.ds(halo, rows), :] = xa_ref[...]
    xc = cb_ref[...] + cw_ref[pl.ds(CONV_W - 1, 1), :] * ext_scr[pl.ds(halo, rows), :]
    for j in range(1, CONV_W):
        xc = xc + cw_ref[pl.ds(CONV_W - 1 - j, 1), :] * ext_scr[pl.ds(halo - NSEQ * j, rows), :]
    tail_scr[...] = ext_scr[pl.ds(rows, halo), :]

    sp = _softplus(-lam_ref[...])
    a, u = _rg_gates(xc, wa_ref, wi_ref, ba_ref[...], bi_ref[...], sp)
    a_scr[pl.ds(0, rows), :] = a
    u_scr[pl.ds(0, rows), :] = u
    odd = _odd_rows(D_RNN)

    def rg_body(g, c):
        row = pl.multiple_of(g * SUBLANES, SUBLANES)
        a_v = a_scr[pl.ds(row, SUBLANES), :]
        u_v = u_scr[pl.ds(row, SUBLANES), :]
        hl = u_v + jnp.where(odd, a_v * pltpu.roll(u_v, NSEQ, 0), 0.0)
        p = jnp.where(odd, a_v * pltpu.roll(a_v, NSEQ, 0), a_v)
        hs_scr[pl.ds(row, SUBLANES), :] = hl + p * c
        q = jnp.where(odd, hl, pltpu.roll(hl, NSEQ, 0))
        pp = jnp.where(odd, p, pltpu.roll(p, NSEQ, 0))
        return q + pp * c

    hcar_scr[...] = lax.fori_loop(0, ngroups, rg_body, hcar_scr[...])
    if emit:
        ua_ref[...] = (hs_scr[pl.ds(0, rows), :] * _gelu(ya_ref[...].astype(F32))).astype(BF16)

    odd_s = _odd_rows(S5_BLOCK_STATE)
    for kb in range(S5_BLOCKS):
        lo = kb * S5_BLOCK_STATE
        ub = us_ref[:, kb * MXU_DIM:(kb + 1) * MXU_DIM]
        bu_scr[pl.ds(0, rows), :] = _dot(ub, bdb_ref[kb])
        aor, aoi, pr, pi, a2r, a2i = [cst_scr[i, :, lo:lo + S5_BLOCK_STATE] for i in range(6)]

        def s5_body(g, c, aor=aor, aoi=aoi, pr=pr, pi=pi, a2r=a2r, a2i=a2i):
            cr, ci = c
            row = pl.multiple_of(g * SUBLANES, SUBLANES)
            bur = bu_scr[pl.ds(row, SUBLANES), 0:S5_BLOCK_STATE]
            bui = bu_scr[pl.ds(row, SUBLANES), S5_BLOCK_STATE:2 * S5_BLOCK_STATE]
            sr = pltpu.roll(bur, NSEQ, 0)
            si = pltpu.roll(bui, NSEQ, 0)
            hlr = bur + aor * sr - aoi * si
            hli = bui + aor * si + aoi * sr
            if emit:
                hst_scr[pl.ds(row, SUBLANES), 0:S5_BLOCK_STATE] = hlr + pr * cr - pi * ci
                hst_scr[pl.ds(row, SUBLANES), S5_BLOCK_STATE:2 * S5_BLOCK_STATE] = (
                    hli + pr * ci + pi * cr)
            qr = jnp.where(odd_s, hlr, pltpu.roll(hlr, NSEQ, 0))
            qi = jnp.where(odd_s, hli, pltpu.roll(hli, NSEQ, 0))
            return qr + a2r * cr - a2i * ci, qi + a2r * ci + a2i * cr

        cr, ci = lax.fori_loop(
            0, ngroups, s5_body,
            (s5car_scr[0, :, lo:lo + S5_BLOCK_STATE], s5car_scr[1, :, lo:lo + S5_BLOCK_STATE]))
        s5car_scr[0, :, lo:lo + S5_BLOCK_STATE] = cr
        s5car_scr[1, :, lo:lo + S5_BLOCK_STATE] = ci
        if emit:
            hre = hst_scr[pl.ds(0, rows), 0:S5_BLOCK_STATE].astype(BF16)
            him = hst_scr[pl.ds(0, rows), S5_BLOCK_STATE:2 * S5_BLOCK_STATE].astype(BF16)
            y = _dot(hre, cre_ref[kb]) - _dot(him, cim_ref[kb])
            y = y + d_ref[:, kb * MXU_DIM:(kb + 1) * MXU_DIM] * ub.astype(F32)
            y_scr[pl.ds(0, rows), kb * MXU_DIM:(kb + 1) * MXU_DIM] = y

    if emit:
        g5 = _gelu(y_scr[pl.ds(0, rows), :])
        gate = _sigmoid(_dot(g5.astype(BF16), gluw_ref[...]) + glub_ref[...])
        gl_ref[...] = (g5 * gate).astype(BF16)


def _mixer_body(meta_rows, n_s, xa_ref, ya_ref, us_ref, xam_ref, usm_ref, ar_ref, ai_ref,
                h0_ref, cbuf_ref, s5r0_ref, s5i0_ref, *rest):
    w = rest[:13]
    (ua_ref, gl_ref, hout_ref, convout_ref, s5r_ref, s5i_ref,
     h1_ref, cnew_ref, s5r1_ref, s5i1_ref) = rest[13:23]
    s = rest[23:]
    y_scr, tail_scr, hcar_scr, s5car_scr, cst_scr = s[6], s[7], s[8], s[9], s[10]
    c = pl.program_id(0)
    rows = xa_ref.shape[0]
    nchunks = pl.num_programs(0) - 1

    @pl.when(c == nchunks)
    def _():
        head = lambda r: r.at[pl.ds(0, n_s)]
        _sample_step(head(xa_ref), head(ya_ref), head(us_ref), h0_ref, cbuf_ref, s5r0_ref,
                     s5i0_ref, ar_ref, ai_ref, w, head(ua_ref), head(gl_ref), h1_ref, cnew_ref,
                     s5r1_ref, s5i1_ref, head(y_scr))
        ua_ref[pl.ds(n_s, rows - n_s), :] = jnp.zeros((rows - n_s, D_RNN), BF16)
        gl_ref[pl.ds(n_s, rows - n_s), :] = jnp.zeros((rows - n_s, D_S5), BF16)

    @pl.when(c < nchunks)
    def _():
        _prompt_step(meta_rows, nchunks, xa_ref, ya_ref, us_ref, xam_ref, usm_ref, ar_ref, ai_ref,
                     w, ua_ref, gl_ref, hout_ref, convout_ref, s5r_ref, s5i_ref, s)


def _prompt_step(meta_rows, nchunks, xa_ref, ya_ref, us_ref, xam_ref, usm_ref, ar_ref, ai_ref,
                 w, ua_ref, gl_ref, hout_ref, convout_ref, s5r_ref, s5i_ref, s):
    tail_scr, hcar_scr, s5car_scr, cst_scr = s[7], s[8], s[9], s[10]
    c = pl.program_id(0)
    rows = xa_ref.shape[0]

    @pl.when(c == 0)
    def _():
        odd = _odd_rows(S5_STATE)
        ar = jnp.broadcast_to(ar_ref[...], (SUBLANES, S5_STATE))
        ai = jnp.broadcast_to(ai_ref[...], (SUBLANES, S5_STATE))
        a2r = ar * ar - ai * ai
        a2i = 2.0 * (ar * ai)
        cst_scr[0] = jnp.where(odd, ar, 0.0)
        cst_scr[1] = jnp.where(odd, ai, 0.0)
        cst_scr[2] = jnp.where(odd, a2r, ar)
        cst_scr[3] = jnp.where(odd, a2i, ai)
        cst_scr[4] = a2r
        cst_scr[5] = a2i
        tail_scr[...] = jnp.zeros_like(tail_scr)
        hcar_scr[...] = jnp.zeros_like(hcar_scr)
        s5car_scr[...] = jnp.zeros_like(s5car_scr)
        _mixer_chunk(meta_rows, xam_ref, None, usm_ref, None, None, w, s)

    _mixer_chunk(rows, xa_ref, ya_ref, us_ref, ua_ref, gl_ref, w, s)

    @pl.when(c == nchunks - 1)
    def _():
        hout_ref[...] = hcar_scr[pl.ds(NSEQ, NSEQ), :]
        convout_ref[...] = tail_scr[pl.ds(NSEQ, NSEQ * (CONV_W - 1)), :]
        s5r_ref[...] = s5car_scr[0, pl.ds(NSEQ, NSEQ), :]
        s5i_ref[...] = s5car_scr[1, pl.ds(NSEQ, NSEQ), :]


def _mixer_weight_specs(nidx):
    z = (0,) * nidx if nidx else ()
    c2 = lambda *_: (0, 0)
    c3 = lambda *_: (0, 0, 0)
    nblk = D_RNN // MXU_DIM
    return [
        pl.BlockSpec((CONV_W, D_RNN), c2),
        pl.BlockSpec((1, D_RNN), c2),
        pl.BlockSpec((nblk, MXU_DIM, MXU_DIM), c3),
        pl.BlockSpec((nblk, MXU_DIM, MXU_DIM), c3),
        pl.BlockSpec((1, D_RNN), c2),
        pl.BlockSpec((1, D_RNN), c2),
        pl.BlockSpec((1, D_RNN), c2),
        pl.BlockSpec((S5_BLOCKS, MXU_DIM, 2 * S5_BLOCK_STATE), c3),
        pl.BlockSpec((S5_BLOCKS, S5_BLOCK_STATE, MXU_DIM), c3),
        pl.BlockSpec((S5_BLOCKS, S5_BLOCK_STATE, MXU_DIM), c3),
        pl.BlockSpec((1, D_S5), c2),
        pl.BlockSpec((D_S5, D_S5), c2),
        pl.BlockSpec((1, D_S5), c2),
    ]


def _mixer(xa, z16, a_r, a_i, h0, cbuf, s5r0, s5i0, weights, n_prompt, n_s, meta_row0, meta_rows,
           chunk):
    nchunks = n_prompt // chunk
    n1 = xa.shape[0]
    assert n1 == n_prompt + chunk and n_s <= chunk
    meta_blk = meta_row0 // meta_rows
    halo = NSEQ * CONV_W
    c2 = lambda c: (0, 0)
    in_specs = [
        pl.BlockSpec((chunk, D_RNN), lambda c: (c, 0)),
        pl.BlockSpec((chunk, D_RNN), lambda c: (c, 0)),
        pl.BlockSpec((chunk, D_S5), lambda c: (c, 1)),
        pl.BlockSpec((meta_rows, D_RNN), lambda c: (meta_blk, 0)),
        pl.BlockSpec((meta_rows, D_S5), lambda c: (meta_blk, 1)),
        pl.BlockSpec((1, S5_STATE), c2),
        pl.BlockSpec((1, S5_STATE), c2),
        pl.BlockSpec((n_s, D_RNN), c2),
        pl.BlockSpec((n_s, (CONV_W - 1) * D_RNN), c2),
        pl.BlockSpec((n_s, S5_STATE), c2),
        pl.BlockSpec((n_s, S5_STATE), c2),
    ] + _mixer_weight_specs(1)
    out_specs = [
        pl.BlockSpec((chunk, D_RNN), lambda c: (c, 0)),
        pl.BlockSpec((chunk, D_S5), lambda c: (c, 0)),
        pl.BlockSpec((NSEQ, D_RNN), c2),
        pl.BlockSpec((NSEQ * (CONV_W - 1), D_RNN), c2),
        pl.BlockSpec((NSEQ, S5_STATE), c2),
        pl.BlockSpec((NSEQ, S5_STATE), c2),
        pl.BlockSpec((n_s, D_RNN), c2),
        pl.BlockSpec((n_s, (CONV_W - 1) * D_RNN), c2),
        pl.BlockSpec((n_s, S5_STATE), c2),
        pl.BlockSpec((n_s, S5_STATE), c2),
    ]
    out_shape = [
        jax.ShapeDtypeStruct((n1, D_RNN), BF16),
        jax.ShapeDtypeStruct((n1, D_S5), BF16),
        jax.ShapeDtypeStruct((NSEQ, D_RNN), F32),
        jax.ShapeDtypeStruct((NSEQ * (CONV_W - 1), D_RNN), F32),
        jax.ShapeDtypeStruct((NSEQ, S5_STATE), F32),
        jax.ShapeDtypeStruct((NSEQ, S5_STATE), F32),
        jax.ShapeDtypeStruct((n_s, D_RNN), F32),
        jax.ShapeDtypeStruct((n_s, (CONV_W - 1) * D_RNN), F32),
        jax.ShapeDtypeStruct((n_s, S5_STATE), F32),
        jax.ShapeDtypeStruct((n_s, S5_STATE), F32),
    ]
    scratch = [
        pltpu.VMEM((chunk + halo, D_RNN), F32),
        pltpu.VMEM((chunk, D_RNN), F32),
        pltpu.VMEM((chunk, D_RNN), F32),
        pltpu.VMEM((chunk, D_RNN), F32),
        pltpu.VMEM((chunk, 2 * S5_BLOCK_STATE), F32),
        pltpu.VMEM((chunk, 2 * S5_BLOCK_STATE), F32),
        pltpu.VMEM((chunk, D_S5), F32),
        pltpu.VMEM((halo, D_RNN), F32),
        pltpu.VMEM((SUBLANES, D_RNN), F32),
        pltpu.VMEM((2, SUBLANES, S5_STATE), F32),
        pltpu.VMEM((6, SUBLANES, S5_STATE), F32),
    ]
    return pl.pallas_call(
        functools.partial(_mixer_body, meta_rows, n_s),
        grid=(nchunks + 1,),
        in_specs=in_specs,
        out_specs=out_specs,
        out_shape=out_shape,
        scratch_shapes=scratch,
        compiler_params=_params(("arbitrary",)),
        name="mixer",
    )(xa, z16, z16, xa, z16, a_r, a_i, h0, cbuf, s5r0, s5i0, *weights)


def _sample_step(xa_ref, ya_ref, us_ref, h0_ref, cbuf_ref, s5r0_ref, s5i0_ref, ar_ref, ai_ref,
                 w, ua_ref, gl_ref, h1_ref, cnew_ref, s5r1_ref, s5i1_ref, y_scr):
    (cw_ref, cb_ref, wa_ref, wi_ref, ba_ref, bi_ref, lam_ref, bdb_ref, cre_ref, cim_ref,
     d_ref, gluw_ref, glub_ref) = w
    xa = xa_ref[...]
    xc = cb_ref[...] + cw_ref[pl.ds(CONV_W - 1, 1), :] * xa
    for k in range(CONV_W - 1):
        xc = xc + cw_ref[pl.ds(k, 1), :] * cbuf_ref[:, k * D_RNN:(k + 1) * D_RNN]
    for k in range(CONV_W - 2):
        cnew_ref[:, k * D_RNN:(k + 1) * D_RNN] = cbuf_ref[:, (k + 1) * D_RNN:(k + 2) * D_RNN]
    cnew_ref[:, (CONV_W - 2) * D_RNN:(CONV_W - 1) * D_RNN] = xa

    sp = _softplus(-lam_ref[...])
    a, u = _rg_gates(xc, wa_ref, wi_ref, ba_ref[...], bi_ref[...], sp)
    h1 = a * h0_ref[...] + u
    h1_ref[...] = h1
    ua_ref[...] = (h1 * _gelu(ya_ref[...].astype(F32))).astype(BF16)

    for kb in range(S5_BLOCKS):
        lo = kb * S5_BLOCK_STATE
        ub = us_ref[:, kb * MXU_DIM:(kb + 1) * MXU_DIM]
        bu = _dot(ub, bdb_ref[kb])
        ar = ar_ref[:, lo:lo + S5_BLOCK_STATE]
        ai = ai_ref[:, lo:lo + S5_BLOCK_STATE]
        h0r = s5r0_ref[:, lo:lo + S5_BLOCK_STATE]
        h0i = s5i0_ref[:, lo:lo + S5_BLOCK_STATE]
        hr = bu[:, 0:S5_BLOCK_STATE] + ar * h0r - ai * h0i
        hi = bu[:, S5_BLOCK_STATE:2 * S5_BLOCK_STATE] + ar * h0i + ai * h0r
        s5r1_ref[:, lo:lo + S5_BLOCK_STATE] = hr
        s5i1_ref[:, lo:lo + S5_BLOCK_STATE] = hi
        y = _dot(hr.astype(BF16), cre_ref[kb]) - _dot(hi.astype(BF16), cim_ref[kb])
        y_scr[:, kb * MXU_DIM:(kb + 1) * MXU_DIM] = (
            y + d_ref[:, kb * MXU_DIM:(kb + 1) * MXU_DIM] * ub.astype(F32))

    g5 = _gelu(y_scr[...])
    gate = _sigmoid(_dot(g5.astype(BF16), gluw_ref[...]) + glub_ref[...])
    gl_ref[...] = (g5 * gate).astype(BF16)


def _col_min(x):
    return jnp.min(x, axis=0, keepdims=True)


def _col_max(x):
    return jnp.max(x, axis=0, keepdims=True)


def _col_sum(x):
    return jnp.sum(x, axis=0, keepdims=True)


def _route(sel, scores, tm):
    neg = -jnp.inf
    iota = lax.broadcasted_iota(I32, (GROUP_SIZE, tm), 0)
    sel_b = [sel[g * GROUP_SIZE:(g + 1) * GROUP_SIZE, :] for g in range(N_GROUPS)]
    sc_b = [scores[g * GROUP_SIZE:(g + 1) * GROUP_SIZE, :] for g in range(N_GROUPS)]

    iota_g = lax.broadcasted_iota(I32, (N_GROUPS, tm), 0)
    gs = jnp.zeros((N_GROUPS, tm), F32)
    for g in range(N_GROUPS):
        b = sel_b[g]
        m1 = _col_max(b)
        i1 = _col_min(jnp.where(b == m1, iota, GROUP_SIZE))
        m2 = _col_max(jnp.where(iota == i1, neg, b))
        gs = jnp.where(iota_g == g, m1 + m2, gs)

    keep = jnp.zeros((N_GROUPS, tm), I32)
    work = gs
    for _ in range(TOPK_GROUPS):
        m = _col_max(work)
        idx = _col_min(jnp.where(work == m, iota_g, N_GROUPS))
        hit = iota_g == idx
        keep = jnp.where(hit, 1, keep)
        work = jnp.where(hit, neg, work)

    cand = [jnp.where(keep[g:g + 1, :] > 0, sel_b[g], neg) for g in range(N_GROUPS)]
    ids, vals = [], []
    for _ in range(TOP_K):
        m = _col_max(cand[0])
        for g in range(1, N_GROUPS):
            m = jnp.maximum(m, _col_max(cand[g]))
        idx = _col_min(jnp.where(cand[0] == m, iota, N_EXPERTS))
        for g in range(1, N_GROUPS):
            idx = jnp.minimum(
                idx, _col_min(jnp.where(cand[g] == m, iota + g * GROUP_SIZE, N_EXPERTS)))
        val = jnp.zeros((1, tm), F32)
        for g in range(N_GROUPS):
            hit = (iota + g * GROUP_SIZE) == idx
            val = val + _col_sum(jnp.where(hit, sc_b[g], 0.0))
            cand[g] = jnp.where(hit, neg, cand[g])
        ids.append(idx)
        vals.append(val)
    return ids, vals


def _post_body(ua_ref, gl_ref, ga_ref, gb_ref, x_ref, ling_ref, linb_ref, pa_ref, pb_ref,
               wo_ref, l1g_ref, l1b_ref, x1_ref):
    branch_a = _dot(ua_ref[...], pa_ref[...])
    branch_b = _dot(gl_ref[...], pb_ref[...])
    merged = (_sigmoid(ga_ref[...].astype(F32)) * branch_a
              + _sigmoid(gb_ref[...].astype(F32)) * branch_b)
    o = _dot(merged.astype(BF16), wo_ref[...])
    xn = _layernorm(x_ref[...], ling_ref[...], linb_ref[...])
    x1_ref[...] = _layernorm(ALPHA * xn + o, l1g_ref[...], l1b_ref[...])


def _post_mixer(ua, gl, z16, xflat, ln_in_g, ln_in_b, proj_a, proj_b, w_o, ln1_g, ln1_b,
                n_tok, tile):
    c2 = lambda i: (0, 0)
    in_specs = [
        pl.BlockSpec((tile, D_RNN), lambda i: (i, 0)),
        pl.BlockSpec((tile, D_S5), lambda i: (i, 0)),
        pl.BlockSpec((tile, D_MODEL), lambda i: (i, 1)),
        pl.BlockSpec((tile, D_MODEL), lambda i: (i, 2)),
        pl.BlockSpec((tile, D_MODEL), lambda i: (i, 0)),
        pl.BlockSpec((1, D_MODEL), c2),
        pl.BlockSpec((1, D_MODEL), c2),
        pl.BlockSpec((D_RNN, D_MODEL), c2),
        pl.BlockSpec((D_S5, D_MODEL), c2),
        pl.BlockSpec((D_MODEL, D_MODEL), c2),
        pl.BlockSpec((1, D_MODEL), c2),
        pl.BlockSpec((1, D_MODEL), c2),
    ]
    return pl.pallas_call(
        _post_body,
        grid=(n_tok // tile,),
        in_specs=in_specs,
        out_specs=pl.BlockSpec((tile, D_MODEL), lambda i: (i, 0)),
        out_shape=jax.ShapeDtypeStruct((n_tok, D_MODEL), F32),
        compiler_params=_params(("arbitrary",)),
        name="post_mixer",
    )(ua, gl, z16, z16, xflat, ln_in_g, ln_in_b, proj_a, proj_b, w_o, ln1_g, ln1_b)


def _router_body(x1_ref, wrh_ref, wrl_ref, rb_ref,
                 eidx_ref, rank_ref, gatet_ref, cnt_ref, cnt_scr):
    i = pl.program_id(0)
    tm = x1_ref.shape[0]

    @pl.when(i == 0)
    def _():
        cnt_scr[...] = jnp.zeros_like(cnt_scr)

    x1 = x1_ref[...]

    x_hi = x1.astype(BF16)
    x_lo = (x1 - x_hi.astype(F32)).astype(BF16)
    nt = (((1,), (1,)), ((), ()))
    dg = lambda a, b: lax.dot_general(a, b, nt, preferred_element_type=F32)
    logits = dg(wrh_ref[...], x_hi) + dg(wrh_ref[...], x_lo) + dg(wrl_ref[...], x_hi)
    scores = _sigmoid(logits)
    sel = scores + rb_ref[...]
    ids, vals = _route(sel, scores, tm)

    total = vals[0]
    for v in vals[1:]:
        total = total + v
    iota_k = lax.broadcasted_iota(I32, (TOP_K, tm), 0)
    iota_e = lax.broadcasted_iota(I32, (N_EXPERTS, tm), 0)
    eidx = jnp.zeros((TOP_K, tm), I32)
    gates = jnp.zeros((TOP_K, tm), F32)
    selm = jnp.zeros((N_EXPERTS, tm), F32)
    for k in range(TOP_K):
        eidx = jnp.where(iota_k == k, ids[k], eidx)
        gates = jnp.where(iota_k == k, vals[k] / total * ROUTED_SCALE, gates)
        selm = jnp.where(iota_e == ids[k], 1.0, selm)
    eidx_ref[...] = eidx

    r_i = lax.broadcasted_iota(I32, (tm, tm), 0)
    c_i = lax.broadcasted_iota(I32, (tm, tm), 1)
    upper = jnp.where(r_i < c_i, 1.0, 0.0).astype(BF16)
    rank_all = _dot(selm.astype(BF16), upper) + cnt_scr[...]
    rank = jnp.zeros((TOP_K, tm), F32)
    for k in range(TOP_K):
        rk = _col_sum(jnp.where(iota_e == ids[k], rank_all, 0.0))
        rank = jnp.where(iota_k == k, rk, rank)
    rank_ref[...] = rank.astype(I32)
    cnt_scr[...] = cnt_scr[...] + jnp.sum(selm, axis=1, keepdims=True)
    cnt_ref[...] = cnt_scr[...]

    gpad = jnp.concatenate([gates, jnp.zeros((LANES - TOP_K, tm), F32)], axis=0)
    gatet_ref[...] = gpad.T


def _router(x1, wr_hi, wr_lo, rbias, tile):
    n_tok = x1.shape[0]
    c2 = lambda i: (0, 0)
    return pl.pallas_call(
        _router_body,
        grid=(n_tok // tile,),
        in_specs=[
            pl.BlockSpec((tile, D_MODEL), lambda i: (i, 0)),
            pl.BlockSpec((N_EXPERTS, D_MODEL), c2),
            pl.BlockSpec((N_EXPERTS, D_MODEL), c2),
            pl.BlockSpec((N_EXPERTS, 1), c2),
        ],
        out_specs=[
            pl.BlockSpec((TOP_K, tile), lambda i: (0, i)),
            pl.BlockSpec((TOP_K, tile), lambda i: (0, i)),
            pl.BlockSpec((tile, LANES), lambda i: (i, 0)),
            pl.BlockSpec((N_EXPERTS, 1), c2),
        ],
        out_shape=[
            jax.ShapeDtypeStruct((TOP_K, n_tok), I32),
            jax.ShapeDtypeStruct((TOP_K, n_tok), I32),
            jax.ShapeDtypeStruct((n_tok, LANES), F32),
            jax.ShapeDtypeStruct((N_EXPERTS, 1), F32),
        ],
        scratch_shapes=[pltpu.VMEM((N_EXPERTS, 1), F32)],
        compiler_params=_params(("arbitrary",)),
        name="router",
    )(x1, wr_hi, wr_lo, rbias)


def _positions_body(eidx_ref, rank_ref, cnt_ref, pos_ref, texp_ref, info_ref):
    cnt = cnt_ref[...]
    padded = jnp.floor((cnt + (MOE_TILE - 1)) * (1.0 / MOE_TILE)) * MOE_TILE
    r_i = lax.broadcasted_iota(I32, (N_EXPERTS, N_EXPERTS), 0)
    c_i = lax.broadcasted_iota(I32, (N_EXPERTS, N_EXPERTS), 1)
    eye = r_i == c_i
    as_row = lambda col: jnp.sum(jnp.where(eye, col, 0.0), axis=0, keepdims=True)
    padded_row = as_row(padded)
    base = jnp.sum(jnp.where(c_i < r_i, padded_row, 0.0), axis=1, keepdims=True)
    end = base + padded

    eidx = eidx_ref[...]
    pos = rank_ref[...]
    base_i = base.astype(I32)
    for e in range(N_EXPERTS):
        pos = pos + jnp.where(eidx == e, base_i[e:e + 1, :], 0)
    pos_ref[...] = pos

    ntp = texp_ref.shape[1]
    t_row = lax.broadcasted_iota(I32, (1, ntp), 1).astype(F32) * MOE_TILE
    texp = jnp.sum(jnp.where(end <= t_row, 1, 0), axis=0, keepdims=True)
    texp_ref[...] = jnp.minimum(texp, N_EXPERTS - 1).astype(I32)

    row = lax.broadcasted_iota(I32, (SUBLANES, LANES), 0)
    lane = lax.broadcasted_iota(I32, (SUBLANES, LANES), 1)
    pad_lanes = lambda r: jnp.concatenate(
        [r, jnp.zeros((1, LANES - N_EXPERTS), F32)], axis=1)
    ntiles = jnp.sum(padded, axis=0, keepdims=True) * (1.0 / MOE_TILE)
    used_row = as_row(cnt) > 0.0
    c_f = c_i.astype(F32)
    nxt = jnp.min(jnp.where((c_i > r_i) & used_row, c_f, float(N_EXPERTS)), axis=1, keepdims=True)
    order = jnp.sum(jnp.where((c_i < r_i) & used_row, 1.0, 0.0), axis=1, keepdims=True)
    slot = order - 2.0 * jnp.floor(order * 0.5)
    info = jnp.where(row == 0, pad_lanes(as_row(cnt)), 0.0)
    info = jnp.where(row == 1, pad_lanes(as_row(base)), info)
    info = jnp.where(row == 2, pad_lanes(as_row(end)), info)
    info = jnp.where((row == 3) & (lane == 0), ntiles, info)
    info = jnp.where(row == 4, pad_lanes(as_row(nxt)), info)
    info = jnp.where(row == 5, pad_lanes(as_row(slot)), info)
    info_ref[...] = info.astype(I32)


def _positions(eidx, rank, cnt, n_tiles_max):
    n_tok = eidx.shape[1]
    ntp = -(-n_tiles_max // LANES) * LANES
    return pl.pallas_call(
        _positions_body,
        out_shape=[
            jax.ShapeDtypeStruct((TOP_K, n_tok), I32),
            jax.ShapeDtypeStruct((1, ntp), I32),
            jax.ShapeDtypeStruct((SUBLANES, LANES), I32),
        ],
        compiler_params=pltpu.CompilerParams(vmem_limit_bytes=VMEM_LIMIT),
        name="positions",
    )(eidx, rank, cnt)


def _row_copy(src_hbm, src_row, dst_hbm, dst_row, sem):
    return pltpu.make_async_copy(
        src_hbm.at[pl.ds(src_row, 1)], dst_hbm.at[pl.ds(dst_row, 1)], sem)


def _swiglu_packed(x, w1_ref, w3_ref, w2_ref):
    xb = x.astype(BF16)
    h = (_silu(_dot(xb, w1_ref[...])) * _dot(xb, w3_ref[...])).astype(BF16)
    return _dot(h, w2_ref[...])


def _dispatch_body(info_ref, pos_ref, xp_ref, w1_ref, w3_ref, w2_ref,
                   xs_hbm, sh_ref, zero_scr, sem):
    i = pl.program_id(0)
    tm = xp_ref.shape[0]

    def issue(t, carry):
        for k in range(TOP_K):
            _row_copy(xp_ref, t, xs_hbm, pos_ref[k, t], sem).start(priority=k % 2)
        return carry

    lax.fori_loop(0, tm, issue, 0)
    sh_ref[...] = _swiglu_packed(xp_ref[...], w1_ref, w3_ref, w2_ref)

    for k in range(TOP_K):
        pltpu.make_async_copy(xp_ref, xs_hbm.at[pl.ds(0, tm)], sem).wait()

    @pl.when(i == pl.num_programs(0) - 1)
    def _():
        zero_scr[...] = jnp.zeros_like(zero_scr)

        def per_expert(e, carry):
            start = info_ref[1, e] + info_ref[0, e]
            stop = info_ref[2, e]

            def fill(r, c):
                _row_copy(zero_scr, 0, xs_hbm, r, sem).start()
                return c

            def fill_wait(r, c):
                _row_copy(zero_scr, 0, xs_hbm, 0, sem).wait()
                return c

            lax.fori_loop(start, stop, fill, 0)
            lax.fori_loop(start, stop, fill_wait, 0)
            return carry

        lax.fori_loop(0, N_EXPERTS, per_expert, 0)

        ntiles = info_ref[3, 0]
        tiles_alloc = xs_hbm.shape[0] // MOE_TILE

        def tile_copy(j):
            return pltpu.make_async_copy(
                zero_scr, xs_hbm.at[pl.ds(pl.multiple_of(j * MOE_TILE, MOE_TILE), MOE_TILE)], sem)

        def fill_tile(j, c):
            tile_copy(j).start()
            return c

        def fill_tile_wait(j, c):
            tile_copy(j).wait()
            return c

        lax.fori_loop(ntiles, tiles_alloc, fill_tile, 0)
        lax.fori_loop(ntiles, tiles_alloc, fill_tile_wait, 0)


def _dispatch(info, pos, x1, sh_w1, sh_w3, sh_w2, rows_sorted, tile):
    n_tok, half = x1.shape
    c2 = lambda i, info: (0, 0)
    grid_spec = pltpu.PrefetchScalarGridSpec(
        num_scalar_prefetch=1,
        grid=(n_tok // tile,),
        in_specs=[
            pl.BlockSpec((TOP_K, tile), lambda i, info: (0, i), memory_space=pltpu.SMEM),
            pl.BlockSpec((tile, half), lambda i, info: (i, 0)),
            pl.BlockSpec((D_MODEL, D_EXPERT), c2),
            pl.BlockSpec((D_MODEL, D_EXPERT), c2),
            pl.BlockSpec((D_EXPERT, D_MODEL), c2),
        ],
        out_specs=[
            pl.BlockSpec(memory_space=pl.ANY),
            pl.BlockSpec((tile, D_MODEL), lambda i, info: (i, 0)),
        ],
        scratch_shapes=[pltpu.VMEM((MOE_TILE, half), F32), pltpu.SemaphoreType.DMA],
    )
    return pl.pallas_call(
        _dispatch_body,
        grid_spec=grid_spec,
        out_shape=[
            jax.ShapeDtypeStruct((rows_sorted, half), F32),
            jax.ShapeDtypeStruct((n_tok, D_MODEL), F32),
        ],
        compiler_params=_params(("arbitrary",)),
        name="dispatch",
    )(info, pos, x1, sh_w1, sh_w3, sh_w2)


def _moe_body(texp_ref, info_ref, xs_ref, w1_hbm, w3_hbm, w2_hbm, ys_ref,
              w1f_scr, w3f_scr, w2f_scr, w1b_scr, w3b_scr, w2b_scr, sems):
    t = pl.program_id(0)
    ntiles = info_ref[3, 0]
    tt = jnp.minimum(t, ntiles - 1)
    prev = jnp.maximum(tt - 1, 0)
    expert = texp_ref[0, tt]
    fresh = (t == 0) | (expert != texp_ref[0, prev])
    valid = t < ntiles

    def weight_copies(ex, slot):
        return [
            pltpu.make_async_copy(w1_hbm.at[ex], w1f_scr.at[slot], sems.at[0, slot]),
            pltpu.make_async_copy(w3_hbm.at[ex], w3f_scr.at[slot], sems.at[1, slot]),
            pltpu.make_async_copy(w2_hbm.at[ex], w2f_scr.at[slot], sems.at[2, slot]),
        ]

    @pl.when(t == 0)
    def _():
        for c in weight_copies(expert, info_ref[5, expert]):
            c.start()

    @pl.when(valid & fresh)
    def _():
        slot = info_ref[5, expert]
        for c in weight_copies(expert, slot):
            c.wait()
        w1b_scr[...] = w1f_scr[slot].astype(BF16)
        w3b_scr[...] = w3f_scr[slot].astype(BF16)
        w2b_scr[...] = w2f_scr[slot].astype(BF16)
        nxt = info_ref[4, expert]

        @pl.when(nxt < N_EXPERTS)
        def _():
            for c in weight_copies(nxt, 1 - slot):
                c.start()

    @pl.when(valid)
    def _():
        ys_ref[...] = _swiglu_packed(xs_ref[...], w1b_scr, w3b_scr, w2b_scr)

    @pl.when(jnp.logical_not(valid))
    def _():
        ys_ref[...] = jnp.zeros_like(ys_ref)


def _moe(texp, info, xs, ex_w1, ex_w3, ex_w2, n_tiles_max):
    half = D_MODEL

    def tile_idx(t, texp, info):
        return jnp.minimum(t, info[3, 0] - 1)

    grid_spec = pltpu.PrefetchScalarGridSpec(
        num_scalar_prefetch=2,
        grid=(n_tiles_max,),
        in_specs=[
            pl.BlockSpec((MOE_TILE, half), lambda t, texp, info: (tile_idx(t, texp, info), 0)),
            pl.BlockSpec(memory_space=pl.ANY),
            pl.BlockSpec(memory_space=pl.ANY),
            pl.BlockSpec(memory_space=pl.ANY),
        ],
        out_specs=pl.BlockSpec((MOE_TILE, half), lambda t, texp, info: (t, 0)),
        scratch_shapes=[
            pltpu.VMEM((2, D_MODEL, D_EXPERT), F32),
            pltpu.VMEM((2, D_MODEL, D_EXPERT), F32),
            pltpu.VMEM((2, D_EXPERT, D_MODEL), F32),
            pltpu.VMEM((D_MODEL, D_EXPERT), BF16),
            pltpu.VMEM((D_MODEL, D_EXPERT), BF16),
            pltpu.VMEM((D_EXPERT, D_MODEL), BF16),
            pltpu.SemaphoreType.DMA((3, 2)),
        ],
    )
    return pl.pallas_call(
        _moe_body,
        grid_spec=grid_spec,
        out_shape=jax.ShapeDtypeStruct(xs.shape, F32),
        compiler_params=_params(("arbitrary",)),
        name="moe_experts",
    )(texp, info, xs, ex_w1, ex_w3, ex_w2)


def _combine_body(n_prompt_tiles, pos_ref, ys_hbm, gate_ref, x1_ref, sh_ref, g_ref, b_ref,
                  yp_ref, ysm_ref, buf_scr, y_scr, sem):
    i = pl.program_id(0)
    tm = x1_ref.shape[0]

    def issue(t, carry):
        for k in range(TOP_K):
            pltpu.make_async_copy(
                ys_hbm.at[pl.ds(pos_ref[k, t], 1)], buf_scr.at[k, pl.ds(t, 1)], sem
            ).start(priority=k % 2)
        return carry

    lax.fori_loop(0, tm, issue, 0)

    for k in range(TOP_K):
        pltpu.make_async_copy(ys_hbm.at[pl.ds(0, tm)], buf_scr.at[k], sem).wait()

    gates = gate_ref[...]
    acc = sh_ref[...]
    for k in range(TOP_K):
        acc = acc + gates[:, k:k + 1] * buf_scr[k]
    y = _layernorm(ALPHA * x1_ref[...] + acc, g_ref[...], b_ref[...])

    @pl.when(i < n_prompt_tiles)
    def _():
        steps = tm // NSEQ
        for c in range(D_MODEL // LANES):
            y_scr[c] = y[:, c * LANES:(c + 1) * LANES]
        for s in range(NSEQ):
            for c in range(D_MODEL // LANES):
                yp_ref[s, :, c * LANES:(c + 1) * LANES] = y_scr[c, pl.ds(s, steps, stride=NSEQ), :]

    @pl.when(i >= n_prompt_tiles)
    def _():
        ysm_ref[...] = y


def _combine(pos, ys, gate_t, x1, sh_out, ln2_g, ln2_b, tile, n_prompt):
    n_tok = x1.shape[0]
    n_s = n_tok - n_prompt
    assert n_prompt % tile == 0 and n_s % tile == 0
    npt = n_prompt // tile
    steps = tile // NSEQ
    c2 = lambda i: (0, 0)
    return pl.pallas_call(
        functools.partial(_combine_body, npt),
        grid=(n_tok // tile,),
        in_specs=[
            pl.BlockSpec((TOP_K, tile), lambda i: (0, i), memory_space=pltpu.SMEM),
            pl.BlockSpec(memory_space=pl.ANY),
            pl.BlockSpec((tile, LANES), lambda i: (i, 0)),
            pl.BlockSpec((tile, D_MODEL), lambda i: (i, 0)),
            pl.BlockSpec((tile, D_MODEL), lambda i: (i, 0)),
            pl.BlockSpec((1, D_MODEL), c2),
            pl.BlockSpec((1, D_MODEL), c2),
        ],
        out_specs=[
            pl.BlockSpec((NSEQ, steps, D_MODEL), lambda i: (0, jnp.minimum(i, npt - 1), 0)),
            pl.BlockSpec((tile, D_MODEL), lambda i: (jnp.maximum(i - npt, 0), 0)),
        ],
        out_shape=[
            jax.ShapeDtypeStruct((NSEQ, n_prompt // NSEQ, D_MODEL), F32),
            jax.ShapeDtypeStruct((n_s, D_MODEL), F32),
        ],
        scratch_shapes=[
            pltpu.VMEM((TOP_K, tile, D_MODEL), F32),
            pltpu.VMEM((D_MODEL // LANES, tile, LANES), F32),
            pltpu.SemaphoreType.DMA,
        ],
        compiler_params=_params(("arbitrary",)),
        name="combine",
    )(pos, ys, gate_t, x1, sh_out, ln2_g, ln2_b)


def _pick_tile(n, cap, mult):
    best = mult
    for t in range(mult, cap + 1, mult):
        if n % t == 0:
            best = t
    assert n % best == 0
    return best


def kernel(x_prompt, x_sample, state_rglru_h, state_conv, state_s5_re, state_s5_im, meta_tokens, ln_in_g, ln_in_b, w_in, b_in, conv_w, conv_b, rg_wa, rg_ba, rg_wi, rg_bi, rg_lambda, s5_a_re, s5_a_im, s5_b_re, s5_b_im, s5_c_re, s5_c_im, s5_d, s5_log_dt, glu_w, glu_b, proj_a, proj_b, w_o, ln1_g, ln1_b, router_w, router_bias, ex_w1, ex_w3, ex_w2, sh_w1, sh_w3, sh_w2, ln2_g, ln2_b):
    bp, seq, d = x_prompt.shape
    n_s = x_sample.shape[0]
    assert bp == NSEQ and d == D_MODEL and x_sample.shape[1] == 1
    assert w_in.shape[0] == DEPTH
    n_prompt = bp * seq
    n_tok = n_prompt + n_s
    meta_rows = NSEQ * N_META
    n1 = n_tok + 2 * meta_rows
    row = lambda v: v.reshape(1, -1)

    xflat = jnp.concatenate([
        jnp.transpose(x_prompt, (1, 0, 2)).reshape(n_prompt, d),
        x_sample.reshape(n_s, d),
        jnp.repeat(meta_tokens, NSEQ, axis=0),
        jnp.zeros((meta_rows, d), F32),
    ], axis=0)

    tile1 = _pick_tile(n1, 1056, 2 * SUBLANES)
    xa, z16 = _inproj(xflat, row(ln_in_g), row(ln_in_b), w_in[0].astype(BF16), b_in, tile1)

    a_r, a_i, bb_r, bb_i = _s5_prep(s5_a_re[0], s5_a_im[0], s5_log_dt[0], s5_b_re[0], s5_b_im[0])
    bg = S5_BLOCK_GROUPS
    bd_b = jnp.concatenate([
        _block_diag(bb_r.reshape(S5_BLOCKS, bg, S5_CH, S5_N)),
        _block_diag(bb_i.reshape(S5_BLOCKS, bg, S5_CH, S5_N)),
    ], axis=2).astype(BF16)
    c_t = lambda c: jnp.transpose(c[0], (0, 2, 1)).reshape(S5_BLOCKS, bg, S5_N, S5_CH)
    bd_cre = _block_diag(c_t(s5_c_re)).astype(BF16)
    bd_cim = _block_diag(c_t(s5_c_im)).astype(BF16)
    heads_per_blk = RG_HEADS // (D_RNN // MXU_DIM)
    rg_blk = lambda w: _block_diag(
        w[0].reshape(D_RNN // MXU_DIM, heads_per_blk, D_RNN // RG_HEADS, D_RNN // RG_HEADS)
    ).astype(BF16)
    weights = (conv_w[0], conv_b, rg_blk(rg_wa), rg_blk(rg_wi), rg_ba, rg_bi, rg_lambda,
               bd_b, bd_cre, bd_cim, row(s5_d[0]), glu_w[0].astype(BF16), glu_b)

    chunk = n_s + 2 * meta_rows
    assert n_prompt % chunk == 0
    ua, gl, p_h, p_conv, p_s5r, p_s5i, s_h, s_conv, s_s5r, s_s5i = _mixer(
        xa, z16, a_r, a_i, state_rglru_h[0],
        state_conv[0].reshape(n_s, (CONV_W - 1) * D_RNN),
        state_s5_re[0].reshape(n_s, S5_STATE), state_s5_im[0].reshape(n_s, S5_STATE),
        weights, n_prompt, n_s, n_tok, meta_rows, chunk)

    wr_t = jnp.transpose(router_w[0])
    wr_hi = wr_t.astype(BF16)
    wr_lo = (wr_t - wr_hi.astype(F32)).astype(BF16)
    tile3 = _pick_tile(n_tok, POST_TILE, 2 * SUBLANES)
    x1 = _post_mixer(
        ua, gl, z16, xflat, row(ln_in_g), row(ln_in_b), proj_a[0].astype(BF16),
        proj_b[0].astype(BF16), w_o[0].astype(BF16), ln1_g, ln1_b, n_tok, tile3)
    tile4 = _pick_tile(n_tok, TOKEN_TILE, LANES)
    eidx, rank, gate_t, cnt = _router(
        x1, wr_hi, wr_lo, router_bias[0].reshape(N_EXPERTS, 1), tile4)

    rows_max = n_tok * TOP_K + N_EXPERTS * (MOE_TILE - 1)
    n_tiles_max = -(-rows_max // MOE_TILE)
    pos, texp, info = _positions(eidx, rank, cnt, n_tiles_max)
    xs, sh_out = _dispatch(info, pos, x1, sh_w1[0].astype(BF16), sh_w3[0].astype(BF16),
                           sh_w2[0].astype(BF16), n_tiles_max * MOE_TILE, tile4)
    ys = _moe(texp, info, xs, ex_w1[0], ex_w3[0], ex_w2[0], n_tiles_max)
    y_prompt, y_sample = _combine(pos, ys, gate_t, x1, sh_out, ln2_g, ln2_b,
                                  _pick_tile(n_s, FIN_TILE, LANES), n_prompt)

    dt = x_prompt.dtype
    y_sample = y_sample.reshape(n_s, 1, d)
    conv_p = jnp.transpose(p_conv.reshape(CONV_W - 1, bp, D_RNN), (1, 0, 2))
    s5_shape = (S5_GROUPS, S5_N)
    return (y_prompt.astype(dt), y_sample.astype(dt),
            p_h[None], conv_p[None],
            p_s5r.reshape(1, bp, *s5_shape), p_s5i.reshape(1, bp, *s5_shape),
            s_h[None], s_conv.reshape(1, n_s, CONV_W - 1, D_RNN),
            s_s5r.reshape(1, n_s, *s5_shape), s_s5i.reshape(1, n_s, *s5_shape))
```

```python
import functools
import math

import jax
import jax.numpy as jnp
from jax import lax
from jax.experimental import pallas as pl
from jax.experimental.pallas import tpu as pltpu

F32 = jnp.float32
BF16 = jnp.bfloat16
I32 = jnp.int32

D_MODEL = 2048
D_RNN = D_MODEL // 2
D_S5 = D_MODEL // 2
N_IN = 2 * D_RNN + D_S5 + 2 * D_MODEL
RG_HEADS = 8
CONV_W = 4
LRU_C = 8.0
S5_CH = 16
S5_GROUPS = D_S5 // S5_CH
S5_N = 64
S5_STATE = S5_GROUPS * S5_N
N_EXPERTS = 64
TOP_K = 8
N_GROUPS = 8
GROUP_SIZE = N_EXPERTS // N_GROUPS
TOPK_GROUPS = 4
D_EXPERT = 512
ROUTED_SCALE = 2.5
LN_EPS = 1e-5
N_META = 16
DEPTH = 1
ALPHA = (2.0 * DEPTH) ** 0.25

SUBLANES = 8
LANES = 128
MXU_DIM = 256
VMEM_LIMIT = 56 * 1024 * 1024

NSEQ = 4
S5_BLOCK_GROUPS = MXU_DIM // S5_CH
S5_BLOCKS = S5_GROUPS // S5_BLOCK_GROUPS
S5_BLOCK_STATE = S5_BLOCK_GROUPS * S5_N
IN_TILE_N = 1024
POST_TILE = 320
TOKEN_TILE = 640
MOE_TILE = 256
FIN_TILE = 128


def _params(sem, vmem=VMEM_LIMIT):
    return pltpu.CompilerParams(dimension_semantics=sem, vmem_limit_bytes=vmem)


def _dot(a, b):
    return jnp.dot(a, b, preferred_element_type=F32)


def _layernorm(x, g, b):
    mu = jnp.mean(x, axis=-1, keepdims=True)
    xc = x - mu
    var = jnp.mean(xc * xc, axis=-1, keepdims=True)
    return xc * lax.rsqrt(var + LN_EPS) * g + b


def _sigmoid(x):
    return 1.0 / (1.0 + jnp.exp(-x))


def _gelu(x):
    c = math.sqrt(2.0 / math.pi)
    return 0.5 * x * (1.0 + jnp.tanh(c * (x + 0.044715 * (x * x * x))))


def _silu(x):
    return x * _sigmoid(x)


def _softplus(x):
    return jnp.maximum(x, 0.0) + jnp.log1p(jnp.exp(-jnp.abs(x)))


def _neg_expm1(x):
    poly = x * (1.0 + x * (1.0 / 2) * (1.0 + x * (1.0 / 3) * (1.0 + x * (1.0 / 4) * (
        1.0 + x * (1.0 / 5) * (1.0 + x * (1.0 / 6) * (1.0 + x * (1.0 / 7)))))))
    return -jnp.where(x > -0.25, poly, jnp.exp(x) - 1.0)


def _s5_prep_body(are_ref, aim_ref, ldt_ref, bre_ref, bim_ref,
                  abr_ref, abi_ref, bbr_ref, bbi_ref):
    a_re = are_ref[...]
    a_im = aim_ref[...]
    dt = jnp.exp(ldt_ref[...])
    mag = jnp.exp(a_re * dt)
    ab_r = mag * jnp.cos(a_im * dt)
    ab_i = mag * jnp.sin(a_im * dt)
    den = a_re * a_re + a_im * a_im
    nr = ab_r - 1.0
    cr = (nr * a_re + ab_i * a_im) / den
    ci = (ab_i * a_re - nr * a_im) / den
    b_re = bre_ref[...]
    b_im = bim_ref[...]
    abr_ref[...] = ab_r
    abi_ref[...] = ab_i
    bbr_ref[...] = cr * b_re - ci * b_im
    bbi_ref[...] = cr * b_im + ci * b_re


def _s5_prep(a_re, a_im, log_dt, b_re, b_im):
    g, n, c = b_re.shape
    wide = c * n
    bc = lambda v: jnp.broadcast_to(v[:, None, :], (g, c, n)).reshape(g, wide)
    are_x = bc(a_re)
    aim_x = bc(a_im)
    ldt_x = jnp.broadcast_to(log_dt[:, None], (g, wide))
    bre_x = jnp.transpose(b_re, (0, 2, 1)).reshape(g, wide)
    bim_x = jnp.transpose(b_im, (0, 2, 1)).reshape(g, wide)
    shp = jax.ShapeDtypeStruct((g, wide), F32)
    abr, abi, bbr, bbi = pl.pallas_call(
        _s5_prep_body, out_shape=(shp, shp, shp, shp), name="s5_prep",
    )(are_x, aim_x, ldt_x, bre_x, bim_x)
    a_r = abr[:, :n].reshape(1, g * n)
    a_i = abi[:, :n].reshape(1, g * n)
    return a_r, a_i, bbr.reshape(g, c, n), bbi.reshape(g, c, n)


def _block_diag(x):
    k, g, a, b = x.shape
    eye = jnp.eye(g, dtype=x.dtype)
    return jnp.einsum("kgab,gh->kgahb", x, eye).reshape(k, g * a, g * b)


def _inproj_body(x_ref, g_ref, b_ref, w_ref, bias_ref, xa_ref, z_ref, xn_scr):
    j = pl.program_id(1)

    @pl.when(j == 0)
    def _():
        xn_scr[...] = _layernorm(x_ref[...], g_ref[...], b_ref[...]).astype(BF16)

    z = _dot(xn_scr[...], w_ref[...]) + bias_ref[...]
    n_xa = D_RNN // IN_TILE_N

    @pl.when(j < n_xa)
    def _():
        xa_ref[...] = z

    @pl.when(j >= n_xa)
    def _():
        z_ref[...] = z.astype(BF16)


def _inproj(xflat, ln_g, ln_b, w_in_bf, b_in, tile):
    n1 = xflat.shape[0]
    n_xa = D_RNN // IN_TILE_N
    grid = (n1 // tile, N_IN // IN_TILE_N)
    return pl.pallas_call(
        _inproj_body,
        grid=grid,
        in_specs=[
            pl.BlockSpec((tile, D_MODEL), lambda i, j: (i, 0)),
            pl.BlockSpec((1, D_MODEL), lambda i, j: (0, 0)),
            pl.BlockSpec((1, D_MODEL), lambda i, j: (0, 0)),
            pl.BlockSpec((D_MODEL, IN_TILE_N), lambda i, j: (0, j)),
            pl.BlockSpec((1, IN_TILE_N), lambda i, j: (0, j)),
        ],
        out_specs=[
            pl.BlockSpec((tile, IN_TILE_N), lambda i, j: (i, jnp.minimum(j, n_xa - 1))),
            pl.BlockSpec((tile, IN_TILE_N), lambda i, j: (i, jnp.maximum(j - n_xa, 0))),
        ],
        out_shape=[
            jax.ShapeDtypeStruct((n1, D_RNN), F32),
            jax.ShapeDtypeStruct((n1, N_IN - D_RNN), BF16),
        ],
        scratch_shapes=[pltpu.VMEM((tile, D_MODEL), BF16)],
        compiler_params=_params(("arbitrary", "arbitrary")),
        name="in_proj",
    )(xflat, ln_g, ln_b, w_in_bf, b_in)


def _rg_gates(xc, wa_ref, wi_ref, ba, bi, sp):
    xcb = xc.astype(BF16)
    nblk = D_RNN // MXU_DIM
    r_pre = jnp.concatenate(
        [_dot(xcb[:, k * MXU_DIM:(k + 1) * MXU_DIM], wa_ref[k]) for k in range(nblk)], axis=1)
    i_pre = jnp.concatenate(
        [_dot(xcb[:, k * MXU_DIM:(k + 1) * MXU_DIM], wi_ref[k]) for k in range(nblk)], axis=1)
    r = _sigmoid(r_pre + ba)
    i = _sigmoid(i_pre + bi)
    log_a = (-LRU_C * r) * sp
    a = jnp.exp(log_a)
    u = jnp.sqrt(_neg_expm1(2.0 * log_a)) * (i * xc)
    return a, u


def _odd_rows(width):
    return lax.broadcasted_iota(I32, (SUBLANES, width), 0) >= NSEQ


def _mixer_chunk(rows, xa_ref, ya_ref, us_ref, ua_ref, gl_ref, w, s):
    (cw_ref, cb_ref, wa_ref, wi_ref, ba_ref, bi_ref, lam_ref, bdb_ref, cre_ref, cim_ref,
     d_ref, gluw_ref, glub_ref) = w
    (ext_scr, a_scr, u_scr, hs_scr, bu_scr, hst_scr, y_scr, tail_scr, hcar_scr, s5car_scr,
     cst_scr) = s
    emit = ua_ref is not None
    halo = NSEQ * CONV_W
    ngroups = rows // SUBLANES

    ext_scr[pl.ds(0, halo), :] = tail_scr[...]
    ext_scr[pl.ds(halo, rows), :] = xa_ref[...]
    xc = cb_ref[...] + cw_ref[pl.ds(CONV_W - 1, 1), :] * ext_scr[pl.ds(halo, rows), :]
    for j in range(1, CONV_W):
        xc = xc + cw_ref[pl.ds(CONV_W - 1 - j, 1), :] * ext_scr[pl.ds(halo - NSEQ * j, rows), :]
    tail_scr[...] = ext_scr[pl.ds(rows, halo), :]

    sp = _softplus(-lam_ref[...])
    a, u = _rg_gates(xc, wa_ref, wi_ref, ba_ref[...], bi_ref[...], sp)
    a_scr[pl.ds(0, rows), :] = a
    u_scr[pl.ds(0, rows), :] = u
    odd = _odd_rows(D_RNN)

    def rg_body(g, c):
        row = pl.multiple_of(g * SUBLANES, SUBLANES)
        a_v = a_scr[pl.ds(row, SUBLANES), :]
        u_v = u_scr[pl.ds(row, SUBLANES), :]
        hl = u_v + jnp.where(odd, a_v * pltpu.roll(u_v, NSEQ, 0), 0.0)
        p = jnp.where(odd, a_v * pltpu.roll(a_v, NSEQ, 0), a_v)
        hs_scr[pl.ds(row, SUBLANES), :] = hl + p * c
        q = jnp.where(odd, hl, pltpu.roll(hl, NSEQ, 0))
        pp = jnp.where(odd, p, pltpu.roll(p, NSEQ, 0))
        return q + pp * c

    hcar_scr[...] = lax.fori_loop(0, ngroups, rg_body, hcar_scr[...])
    if emit:
        ua_ref[...] = (hs_scr[pl.ds(0, rows), :] * _gelu(ya_ref[...].astype(F32))).astype(BF16)

    odd_s = _odd_rows(S5_BLOCK_STATE)
    for kb in range(S5_BLOCKS):
        lo = kb * S5_BLOCK_STATE
        ub = us_ref[:, kb * MXU_DIM:(kb + 1) * MXU_DIM]
        bu_scr[pl.ds(0, rows), :] = _dot(ub, bdb_ref[kb])
        aor, aoi, pr, pi, a2r, a2i = [cst_scr[i, :, lo:lo + S5_BLOCK_STATE] for i in range(6)]

        def s5_body(g, c, aor=aor, aoi=aoi, pr=pr, pi=pi, a2r=a2r, a2i=a2i):
            cr, ci = c
            row = pl.multiple_of(g * SUBLANES, SUBLANES)
            bur = bu_scr[pl.ds(row, SUBLANES), 0:S5_BLOCK_STATE]
            bui = bu_scr[pl.ds(row, SUBLANES), S5_BLOCK_STATE:2 * S5_BLOCK_STATE]
            sr = pltpu.roll(bur, NSEQ, 0)
            si = pltpu.roll(bui, NSEQ, 0)
            hlr = bur + aor * sr - aoi * si
            hli = bui + aor * si + aoi * sr
            if emit:
                hst_scr[pl.ds(row, SUBLANES), 0:S5_BLOCK_STATE] = hlr + pr * cr - pi * ci
                hst_scr[pl.ds(row, SUBLANES), S5_BLOCK_STATE:2 * S5_BLOCK_STATE] = (
                    hli + pr * ci + pi * cr)
            qr = jnp.where(odd_s, hlr, pltpu.roll(hlr, NSEQ, 0))
            qi = jnp.where(odd_s, hli, pltpu.roll(hli, NSEQ, 0))
            return qr + a2r * cr - a2i * ci, qi + a2r * ci + a2i * cr

        cr, ci = lax.fori_loop(
            0, ngroups, s5_body,
            (s5car_scr[0, :, lo:lo + S5_BLOCK_STATE], s5car_scr[1, :, lo:lo + S5_BLOCK_STATE]))
        s5car_scr[0, :, lo:lo + S5_BLOCK_STATE] = cr
        s5car_scr[1, :, lo:lo + S5_BLOCK_STATE] = ci
        if emit:
            hre = hst_scr[pl.ds(0, rows), 0:S5_BLOCK_STATE].astype(BF16)
            him = hst_scr[pl.ds(0, rows), S5_BLOCK_STATE:2 * S5_BLOCK_STATE].astype(BF16)
            y = _dot(hre, cre_ref[kb]) - _dot(him, cim_ref[kb])
            y = y + d_ref[:, kb * MXU_DIM:(kb + 1) * MXU_DIM] * ub.astype(F32)
            y_scr[pl.ds(0, rows), kb * MXU_DIM:(kb + 1) * MXU_DIM] = y

    if emit:
        g5 = _gelu(y_scr[pl.ds(0, rows), :])
        gate = _sigmoid(_dot(g5.astype(BF16), gluw_ref[...]) + glub_ref[...])
        gl_ref[...] = (g5 * gate).astype(BF16)


def _mixer_body(meta_rows, n_s, xa_ref, ya_ref, us_ref, xam_ref, usm_ref, ar_ref, ai_ref,
                h0_ref, cbuf_ref, s5r0_ref, s5i0_ref, *rest):
    w = rest[:13]
    (ua_ref, gl_ref, hout_ref, convout_ref, s5r_ref, s5i_ref,
     h1_ref, cnew_ref, s5r1_ref, s5i1_ref) = rest[13:23]
    s = rest[23:]
    y_scr, tail_scr, hcar_scr, s5car_scr, cst_scr = s[6], s[7], s[8], s[9], s[10]
    c = pl.program_id(0)
    rows = xa_ref.shape[0]
    nchunks = pl.num_programs(0) - 1

    @pl.when(c == nchunks)
    def _():
        head = lambda r: r.at[pl.ds(0, n_s)]
        _sample_step(head(xa_ref), head(ya_ref), head(us_ref), h0_ref, cbuf_ref, s5r0_ref,
                     s5i0_ref, ar_ref, ai_ref, w, head(ua_ref), head(gl_ref), h1_ref, cnew_ref,
                     s5r1_ref, s5i1_ref, head(y_scr))
        ua_ref[pl.ds(n_s, rows - n_s), :] = jnp.zeros((rows - n_s, D_RNN), BF16)
        gl_ref[pl.ds(n_s, rows - n_s), :] = jnp.zeros((rows - n_s, D_S5), BF16)

    @pl.when(c < nchunks)
    def _():
        _prompt_step(meta_rows, nchunks, xa_ref, ya_ref, us_ref, xam_ref, usm_ref, ar_ref, ai_ref,
                     w, ua_ref, gl_ref, hout_ref, convout_ref, s5r_ref, s5i_ref, s)


def _prompt_step(meta_rows, nchunks, xa_ref, ya_ref, us_ref, xam_ref, usm_ref, ar_ref, ai_ref,
                 w, ua_ref, gl_ref, hout_ref, convout_ref, s5r_ref, s5i_ref, s):
    tail_scr, hcar_scr, s5car_scr, cst_scr = s[7], s[8], s[9], s[10]
    c = pl.program_id(0)
    rows = xa_ref.shape[0]

    @pl.when(c == 0)
    def _():
        odd = _odd_rows(S5_STATE)
        ar = jnp.broadcast_to(ar_ref[...], (SUBLANES, S5_STATE))
        ai = jnp.broadcast_to(ai_ref[...], (SUBLANES, S5_STATE))
        a2r = ar * ar - ai * ai
        a2i = 2.0 * (ar * ai)
        cst_scr[0] = jnp.where(odd, ar, 0.0)
        cst_scr[1] = jnp.where(odd, ai, 0.0)
        cst_scr[2] = jnp.where(odd, a2r, ar)
        cst_scr[3] = jnp.where(odd, a2i, ai)
        cst_scr[4] = a2r
        cst_scr[5] = a2i
        tail_scr[...] = jnp.zeros_like(tail_scr)
        hcar_scr[...] = jnp.zeros_like(hcar_scr)
        s5car_scr[...] = jnp.zeros_like(s5car_scr)
        _mixer_chunk(meta_rows, xam_ref, None, usm_ref, None, None, w, s)

    _mixer_chunk(rows, xa_ref, ya_ref, us_ref, ua_ref, gl_ref, w, s)

    @pl.when(c == nchunks - 1)
    def _():
        hout_ref[...] = hcar_scr[pl.ds(NSEQ, NSEQ), :]
        convout_ref[...] = tail_scr[pl.ds(NSEQ, NSEQ * (CONV_W - 1)), :]
        s5r_ref[...] = s5car_scr[0, pl.ds(NSEQ, NSEQ), :]
        s5i_ref[...] = s5car_scr[1, pl.ds(NSEQ, NSEQ), :]


def _mixer_weight_specs(nidx):
    z = (0,) * nidx if nidx else ()
    c2 = lambda *_: (0, 0)
    c3 = lambda *_: (0, 0, 0)
    nblk = D_RNN // MXU_DIM
    return [
        pl.BlockSpec((CONV_W, D_RNN), c2),
        pl.BlockSpec((1, D_RNN), c2),
        pl.BlockSpec((nblk, MXU_DIM, MXU_DIM), c3),
        pl.BlockSpec((nblk, MXU_DIM, MXU_DIM), c3),
        pl.BlockSpec((1, D_RNN), c2),
        pl.BlockSpec((1, D_RNN), c2),
        pl.BlockSpec((1, D_RNN), c2),
        pl.BlockSpec((S5_BLOCKS, MXU_DIM, 2 * S5_BLOCK_STATE), c3),
        pl.BlockSpec((S5_BLOCKS, S5_BLOCK_STATE, MXU_DIM), c3),
        pl.BlockSpec((S5_BLOCKS, S5_BLOCK_STATE, MXU_DIM), c3),
        pl.BlockSpec((1, D_S5), c2),
        pl.BlockSpec((D_S5, D_S5), c2),
        pl.BlockSpec((1, D_S5), c2),
    ]


def _mixer(xa, z16, a_r, a_i, h0, cbuf, s5r0, s5i0, weights, n_prompt, n_s, meta_row0, meta_rows,
           chunk):
    nchunks = n_prompt // chunk
    n1 = xa.shape[0]
    assert n1 == n_prompt + chunk and n_s <= chunk
    meta_blk = meta_row0 // meta_rows
    halo = NSEQ * CONV_W
    c2 = lambda c: (0, 0)
    in_specs = [
        pl.BlockSpec((chunk, D_RNN), lambda c: (c, 0)),
        pl.BlockSpec((chunk, D_RNN), lambda c: (c, 0)),
        pl.BlockSpec((chunk, D_S5), lambda c: (c, 1)),
        pl.BlockSpec((meta_rows, D_RNN), lambda c: (meta_blk, 0)),
        pl.BlockSpec((meta_rows, D_S5), lambda c: (meta_blk, 1)),
        pl.BlockSpec((1, S5_STATE), c2),
        pl.BlockSpec((1, S5_STATE), c2),
        pl.BlockSpec((n_s, D_RNN), c2),
        pl.BlockSpec((n_s, (CONV_W - 1) * D_RNN), c2),
        pl.BlockSpec((n_s, S5_STATE), c2),
        pl.BlockSpec((n_s, S5_STATE), c2),
    ] + _mixer_weight_specs(1)
    out_specs = [
        pl.BlockSpec((chunk, D_RNN), lambda c: (c, 0)),
        pl.BlockSpec((chunk, D_S5), lambda c: (c, 0)),
        pl.BlockSpec((NSEQ, D_RNN), c2),
        pl.BlockSpec((NSEQ * (CONV_W - 1), D_RNN), c2),
        pl.BlockSpec((NSEQ, S5_STATE), c2),
        pl.BlockSpec((NSEQ, S5_STATE), c2),
        pl.BlockSpec((n_s, D_RNN), c2),
        pl.BlockSpec((n_s, (CONV_W - 1) * D_RNN), c2),
        pl.BlockSpec((n_s, S5_STATE), c2),
        pl.BlockSpec((n_s, S5_STATE), c2),
    ]
    out_shape = [
        jax.ShapeDtypeStruct((n1, D_RNN), BF16),
        jax.ShapeDtypeStruct((n1, D_S5), BF16),
        jax.ShapeDtypeStruct((NSEQ, D_RNN), F32),
        jax.ShapeDtypeStruct((NSEQ * (CONV_W - 1), D_RNN), F32),
        jax.ShapeDtypeStruct((NSEQ, S5_STATE), F32),
        jax.ShapeDtypeStruct((NSEQ, S5_STATE), F32),
        jax.ShapeDtypeStruct((n_s, D_RNN), F32),
        jax.ShapeDtypeStruct((n_s, (CONV_W - 1) * D_RNN), F32),
        jax.ShapeDtypeStruct((n_s, S5_STATE), F32),
        jax.ShapeDtypeStruct((n_s, S5_STATE), F32),
    ]
    scratch = [
        pltpu.VMEM((chunk + halo, D_RNN), F32),
        pltpu.VMEM((chunk, D_RNN), F32),
        pltpu.VMEM((chunk, D_RNN), F32),
        pltpu.VMEM((chunk, D_RNN), F32),
        pltpu.VMEM((chunk, 2 * S5_BLOCK_STATE), F32),
        pltpu.VMEM((chunk, 2 * S5_BLOCK_STATE), F32),
        pltpu.VMEM((chunk, D_S5), F32),
        pltpu.VMEM((halo, D_RNN), F32),
        pltpu.VMEM((SUBLANES, D_RNN), F32),
        pltpu.VMEM((2, SUBLANES, S5_STATE), F32),
        pltpu.VMEM((6, SUBLANES, S5_STATE), F32),
    ]
    return pl.pallas_call(
        functools.partial(_mixer_body, meta_rows, n_s),
        grid=(nchunks + 1,),
        in_specs=in_specs,
        out_specs=out_specs,
        out_shape=out_shape,
        scratch_shapes=scratch,
        compiler_params=_params(("arbitrary",)),
        name="mixer",
    )(xa, z16, z16, xa, z16, a_r, a_i, h0, cbuf, s5r0, s5i0, *weights)


def _sample_step(xa_ref, ya_ref, us_ref, h0_ref, cbuf_ref, s5r0_ref, s5i0_ref, ar_ref, ai_ref,
                 w, ua_ref, gl_ref, h1_ref, cnew_ref, s5r1_ref, s5i1_ref, y_scr):
    (cw_ref, cb_ref, wa_ref, wi_ref, ba_ref, bi_ref, lam_ref, bdb_ref, cre_ref, cim_ref,
     d_ref, gluw_ref, glub_ref) = w
    xa = xa_ref[...]
    xc = cb_ref[...] + cw_ref[pl.ds(CONV_W - 1, 1), :] * xa
    for k in range(CONV_W - 1):
        xc = xc + cw_ref[pl.ds(k, 1), :] * cbuf_ref[:, k * D_RNN:(k + 1) * D_RNN]
    for k in range(CONV_W - 2):
        cnew_ref[:, k * D_RNN:(k + 1) * D_RNN] = cbuf_ref[:, (k + 1) * D_RNN:(k + 2) * D_RNN]
    cnew_ref[:, (CONV_W - 2) * D_RNN:(CONV_W - 1) * D_RNN] = xa

    sp = _softplus(-lam_ref[...])
    a, u = _rg_gates(xc, wa_ref, wi_ref, ba_ref[...], bi_ref[...], sp)
    h1 = a * h0_ref[...] + u
    h1_ref[...] = h1
    ua_ref[...] = (h1 * _gelu(ya_ref[...].astype(F32))).astype(BF16)

    for kb in range(S5_BLOCKS):
        lo = kb * S5_BLOCK_STATE
        ub = us_ref[:, kb * MXU_DIM:(kb + 1) * MXU_DIM]
        bu = _dot(ub, bdb_ref[kb])
        ar = ar_ref[:, lo:lo + S5_BLOCK_STATE]
        ai = ai_ref[:, lo:lo + S5_BLOCK_STATE]
        h0r = s5r0_ref[:, lo:lo + S5_BLOCK_STATE]
        h0i = s5i0_ref[:, lo:lo + S5_BLOCK_STATE]
        hr = bu[:, 0:S5_BLOCK_STATE] + ar * h0r - ai * h0i
        hi = bu[:, S5_BLOCK_STATE:2 * S5_BLOCK_STATE] + ar * h0i + ai * h0r
        s5r1_ref[:, lo:lo + S5_BLOCK_STATE] = hr
        s5i1_ref[:, lo:lo + S5_BLOCK_STATE] = hi
        y = _dot(hr.astype(BF16), cre_ref[kb]) - _dot(hi.astype(BF16), cim_ref[kb])
        y_scr[:, kb * MXU_DIM:(kb + 1) * MXU_DIM] = (
            y + d_ref[:, kb * MXU_DIM:(kb + 1) * MXU_DIM] * ub.astype(F32))

    g5 = _gelu(y_scr[...])
    gate = _sigmoid(_dot(g5.astype(BF16), gluw_ref[...]) + glub_ref[...])
    gl_ref[...] = (g5 * gate).astype(BF16)


def _col_min(x):
    return jnp.min(x, axis=0, keepdims=True)


def _col_max(x):
    return jnp.max(x, axis=0, keepdims=True)


def _col_sum(x):
    return jnp.sum(x, axis=0, keepdims=True)


def _route(sel, scores, tm):
    neg = -jnp.inf
    iota = lax.broadcasted_iota(I32, (GROUP_SIZE, tm), 0)
    sel_b = [sel[g * GROUP_SIZE:(g + 1) * GROUP_SIZE, :] for g in range(N_GROUPS)]
    sc_b = [scores[g * GROUP_SIZE:(g + 1) * GROUP_SIZE, :] for g in range(N_GROUPS)]

    iota_g = lax.broadcasted_iota(I32, (N_GROUPS, tm), 0)
    gs = jnp.zeros((N_GROUPS, tm), F32)
    for g in range(N_GROUPS):
        b = sel_b[g]
        m1 = _col_max(b)
        i1 = _col_min(jnp.where(b == m1, iota, GROUP_SIZE))
        m2 = _col_max(jnp.where(iota == i1, neg, b))
        gs = jnp.where(iota_g == g, m1 + m2, gs)

    keep = jnp.zeros((N_GROUPS, tm), I32)
    work = gs
    for _ in range(TOPK_GROUPS):
        m = _col_max(work)
        idx = _col_min(jnp.where(work == m, iota_g, N_GROUPS))
        hit = iota_g == idx
        keep = jnp.where(hit, 1, keep)
        work = jnp.where(hit, neg, work)

    cand = [jnp.where(keep[g:g + 1, :] > 0, sel_b[g], neg) for g in range(N_GROUPS)]
    ids, vals = [], []
    for _ in range(TOP_K):
        m = _col_max(cand[0])
        for g in range(1, N_GROUPS):
            m = jnp.maximum(m, _col_max(cand[g]))
        idx = _col_min(jnp.where(cand[0] == m, iota, N_EXPERTS))
        for g in range(1, N_GROUPS):
            idx = jnp.minimum(
                idx, _col_min(jnp.where(cand[g] == m, iota + g * GROUP_SIZE, N_EXPERTS)))
        val = jnp.zeros((1, tm), F32)
        for g in range(N_GROUPS):
            hit = (iota + g * GROUP_SIZE) == idx
            val = val + _col_sum(jnp.where(hit, sc_b[g], 0.0))
            cand[g] = jnp.where(hit, neg, cand[g])
        ids.append(idx)
        vals.append(val)
    return ids, vals


def _post_body(ua_ref, gl_ref, ga_ref, gb_ref, x_ref, ling_ref, linb_ref, pa_ref, pb_ref,
               wo_ref, l1g_ref, l1b_ref, x1_ref):
    branch_a = _dot(ua_ref[...], pa_ref[...])
    branch_b = _dot(gl_ref[...], pb_ref[...])
    merged = (_sigmoid(ga_ref[...].astype(F32)) * branch_a
              + _sigmoid(gb_ref[...].astype(F32)) * branch_b)
    o = _dot(merged.astype(BF16), wo_ref[...])
    xn = _layernorm(x_ref[...], ling_ref[...], linb_ref[...])
    x1_ref[...] = _layernorm(ALPHA * xn + o, l1g_ref[...], l1b_ref[...])


def _post_mixer(ua, gl, z16, xflat, ln_in_g, ln_in_b, proj_a, proj_b, w_o, ln1_g, ln1_b,
                n_tok, tile):
    c2 = lambda i: (0, 0)
    in_specs = [
        pl.BlockSpec((tile, D_RNN), lambda i: (i, 0)),
        pl.BlockSpec((tile, D_S5), lambda i: (i, 0)),
        pl.BlockSpec((tile, D_MODEL), lambda i: (i, 1)),
        pl.BlockSpec((tile, D_MODEL), lambda i: (i, 2)),
        pl.BlockSpec((tile, D_MODEL), lambda i: (i, 0)),
        pl.BlockSpec((1, D_MODEL), c2),
        pl.BlockSpec((1, D_MODEL), c2),
        pl.BlockSpec((D_RNN, D_MODEL), c2),
        pl.BlockSpec((D_S5, D_MODEL), c2),
        pl.BlockSpec((D_MODEL, D_MODEL), c2),
        pl.BlockSpec((1, D_MODEL), c2),
        pl.BlockSpec((1, D_MODEL), c2),
    ]
    return pl.pallas_call(
        _post_body,
        grid=(n_tok // tile,),
        in_specs=in_specs,
        out_specs=pl.BlockSpec((tile, D_MODEL), lambda i: (i, 0)),
        out_shape=jax.ShapeDtypeStruct((n_tok, D_MODEL), F32),
        compiler_params=_params(("arbitrary",)),
        name="post_mixer",
    )(ua, gl, z16, z16, xflat, ln_in_g, ln_in_b, proj_a, proj_b, w_o, ln1_g, ln1_b)


def _router_body(x1_ref, wrh_ref, wrl_ref, rb_ref,
                 eidx_ref, rank_ref, gatet_ref, cnt_ref, cnt_scr):
    i = pl.program_id(0)
    tm = x1_ref.shape[0]

    @pl.when(i == 0)
    def _():
        cnt_scr[...] = jnp.zeros_like(cnt_scr)

    x1 = x1_ref[...]

    x_hi = x1.astype(BF16)
    x_lo = (x1 - x_hi.astype(F32)).astype(BF16)
    nt = (((1,), (1,)), ((), ()))
    dg = lambda a, b: lax.dot_general(a, b, nt, preferred_element_type=F32)
    logits = dg(wrh_ref[...], x_hi) + dg(wrh_ref[...], x_lo) + dg(wrl_ref[...], x_hi)
    scores = _sigmoid(logits)
    sel = scores + rb_ref[...]
    ids, vals = _route(sel, scores, tm)

    total = vals[0]
    for v in vals[1:]:
        total = total + v
    iota_k = lax.broadcasted_iota(I32, (TOP_K, tm), 0)
    iota_e = lax.broadcasted_iota(I32, (N_EXPERTS, tm), 0)
    eidx = jnp.zeros((TOP_K, tm), I32)
    gates = jnp.zeros((TOP_K, tm), F32)
    selm = jnp.zeros((N_EXPERTS, tm), F32)
    for k in range(TOP_K):
        eidx = jnp.where(iota_k == k, ids[k], eidx)
        gates = jnp.where(iota_k == k, vals[k] / total * ROUTED_SCALE, gates)
        selm = jnp.where(iota_e == ids[k], 1.0, selm)
    eidx_ref[...] = eidx

    r_i = lax.broadcasted_iota(I32, (tm, tm), 0)
    c_i = lax.broadcasted_iota(I32, (tm, tm), 1)
    upper = jnp.where(r_i < c_i, 1.0, 0.0).astype(BF16)
    rank_all = _dot(selm.astype(BF16), upper) + cnt_scr[...]
    rank = jnp.zeros((TOP_K, tm), F32)
    for k in range(TOP_K):
        rk = _col_sum(jnp.where(iota_e == ids[k], rank_all, 0.0))
        rank = jnp.where(iota_k == k, rk, rank)
    rank_ref[...] = rank.astype(I32)
    cnt_scr[...] = cnt_scr[...] + jnp.sum(selm, axis=1, keepdims=True)
    cnt_ref[...] = cnt_scr[...]

    gpad = jnp.concatenate([gates, jnp.zeros((LANES - TOP_K, tm), F32)], axis=0)
    gatet_ref[...] = gpad.T


def _router(x1, wr_hi, wr_lo, rbias, tile):
    n_tok = x1.shape[0]
    c2 = lambda i: (0, 0)
    return pl.pallas_call(
        _router_body,
        grid=(n_tok // tile,),
        in_specs=[
            pl.BlockSpec((tile, D_MODEL), lambda i: (i, 0)),
            pl.BlockSpec((N_EXPERTS, D_MODEL), c2),
            pl.BlockSpec((N_EXPERTS, D_MODEL), c2),
            pl.BlockSpec((N_EXPERTS, 1), c2),
        ],
        out_specs=[
            pl.BlockSpec((TOP_K, tile), lambda i: (0, i)),
            pl.BlockSpec((TOP_K, tile), lambda i: (0, i)),
            pl.BlockSpec((tile, LANES), lambda i: (i, 0)),
            pl.BlockSpec((N_EXPERTS, 1), c2),
        ],
        out_shape=[
            jax.ShapeDtypeStruct((TOP_K, n_tok), I32),
            jax.ShapeDtypeStruct((TOP_K, n_tok), I32),
            jax.ShapeDtypeStruct((n_tok, LANES), F32),
            jax.ShapeDtypeStruct((N_EXPERTS, 1), F32),
        ],
        scratch_shapes=[pltpu.VMEM((N_EXPERTS, 1), F32)],
        compiler_params=_params(("arbitrary",)),
        name="router",
    )(x1, wr_hi, wr_lo, rbias)


def _positions_body(eidx_ref, rank_ref, cnt_ref, pos_ref, texp_ref, info_ref):
    cnt = cnt_ref[...]
    padded = jnp.floor((cnt + (MOE_TILE - 1)) * (1.0 / MOE_TILE)) * MOE_TILE
    r_i = lax.broadcasted_iota(I32, (N_EXPERTS, N_EXPERTS), 0)
    c_i = lax.broadcasted_iota(I32, (N_EXPERTS, N_EXPERTS), 1)
    eye = r_i == c_i
    as_row = lambda col: jnp.sum(jnp.where(eye, col, 0.0), axis=0, keepdims=True)
    padded_row = as_row(padded)
    base = jnp.sum(jnp.where(c_i < r_i, padded_row, 0.0), axis=1, keepdims=True)
    end = base + padded

    eidx = eidx_ref[...]
    pos = rank_ref[...]
    base_i = base.astype(I32)
    for e in range(N_EXPERTS):
        pos = pos + jnp.where(eidx == e, base_i[e:e + 1, :], 0)
    pos_ref[...] = pos

    ntp = texp_ref.shape[1]
    t_row = lax.broadcasted_iota(I32, (1, ntp), 1).astype(F32) * MOE_TILE
    texp = jnp.sum(jnp.where(end <= t_row, 1, 0), axis=0, keepdims=True)
    texp_ref[...] = jnp.minimum(texp, N_EXPERTS - 1).astype(I32)

    row = lax.broadcasted_iota(I32, (SUBLANES, LANES), 0)
    lane = lax.broadcasted_iota(I32, (SUBLANES, LANES), 1)
    pad_lanes = lambda r: jnp.concatenate(
        [r, jnp.zeros((1, LANES - N_EXPERTS), F32)], axis=1)
    ntiles = jnp.sum(padded, axis=0, keepdims=True) * (1.0 / MOE_TILE)
    used_row = as_row(cnt) > 0.0
    c_f = c_i.astype(F32)
    nxt = jnp.min(jnp.where((c_i > r_i) & used_row, c_f, float(N_EXPERTS)), axis=1, keepdims=True)
    order = jnp.sum(jnp.where((c_i < r_i) & used_row, 1.0, 0.0), axis=1, keepdims=True)
    slot = order - 2.0 * jnp.floor(order * 0.5)
    info = jnp.where(row == 0, pad_lanes(as_row(cnt)), 0.0)
    info = jnp.where(row == 1, pad_lanes(as_row(base)), info)
    info = jnp.where(row == 2, pad_lanes(as_row(end)), info)
    info = jnp.where((row == 3) & (lane == 0), ntiles, info)
    info = jnp.where(row == 4, pad_lanes(as_row(nxt)), info)
    info = jnp.where(row == 5, pad_lanes(as_row(slot)), info)
    info_ref[...] = info.astype(I32)


def _positions(eidx, rank, cnt, n_tiles_max):
    n_tok = eidx.shape[1]
    ntp = -(-n_tiles_max // LANES) * LANES
    return pl.pallas_call(
        _positions_body,
        out_shape=[
            jax.ShapeDtypeStruct((TOP_K, n_tok), I32),
            jax.ShapeDtypeStruct((1, ntp), I32),
            jax.ShapeDtypeStruct((SUBLANES, LANES), I32),
        ],
        compiler_params=pltpu.CompilerParams(vmem_limit_bytes=VMEM_LIMIT),
        name="positions",
    )(eidx, rank, cnt)


def _row_copy(src_hbm, src_row, dst_hbm, dst_row, sem):
    return pltpu.make_async_copy(
        src_hbm.at[pl.ds(src_row, 1)], dst_hbm.at[pl.ds(dst_row, 1)], sem)


def _swiglu_packed(x, w1_ref, w3_ref, w2_ref):
    xb = x.astype(BF16)
    h = (_silu(_dot(xb, w1_ref[...])) * _dot(xb, w3_ref[...])).astype(BF16)
    return _dot(h, w2_ref[...])


def _dispatch_body(info_ref, pos_ref, xp_ref, w1_ref, w3_ref, w2_ref,
                   xs_hbm, sh_ref, zero_scr, sem):
    i = pl.program_id(0)
    tm = xp_ref.shape[0]

    def issue(g, carry):
        row0 = pl.multiple_of(g * SUBLANES, SUBLANES)
        for j in range(SUBLANES):
            for k in range(TOP_K):
                _row_copy(xp_ref, row0 + j, xs_hbm, pos_ref[k, row0 + j], sem).start(
                    priority=k % 2)
        return carry

    lax.fori_loop(0, tm // SUBLANES, issue, 0)
    sh_ref[...] = _swiglu_packed(xp_ref[...], w1_ref, w3_ref, w2_ref)

    for k in range(TOP_K):
        pltpu.make_async_copy(xp_ref, xs_hbm.at[pl.ds(0, tm)], sem).wait()

    @pl.when(i == pl.num_programs(0) - 1)
    def _():
        zero_scr[...] = jnp.zeros_like(zero_scr)

        def per_expert(e, carry):
            start = info_ref[1, e] + info_ref[0, e]
            stop = info_ref[2, e]

            def fill(r, c):
                _row_copy(zero_scr, 0, xs_hbm, r, sem).start()
                return c

            def fill_wait(r, c):
                _row_copy(zero_scr, 0, xs_hbm, 0, sem).wait()
                return c

            lax.fori_loop(start, stop, fill, 0)
            lax.fori_loop(start, stop, fill_wait, 0)
            return carry

        lax.fori_loop(0, N_EXPERTS, per_expert, 0)

        ntiles = info_ref[3, 0]
        tiles_alloc = xs_hbm.shape[0] // MOE_TILE

        def tile_copy(j):
            return pltpu.make_async_copy(
                zero_scr, xs_hbm.at[pl.ds(pl.multiple_of(j * MOE_TILE, MOE_TILE), MOE_TILE)], sem)

        def fill_tile(j, c):
            tile_copy(j).start()
            return c

        def fill_tile_wait(j, c):
            tile_copy(j).wait()
            return c

        lax.fori_loop(ntiles, tiles_alloc, fill_tile, 0)
        lax.fori_loop(ntiles, tiles_alloc, fill_tile_wait, 0)


def _dispatch(info, pos, x1, sh_w1, sh_w3, sh_w2, rows_sorted, tile):
    n_tok, half = x1.shape
    c2 = lambda i, info: (0, 0)
    grid_spec = pltpu.PrefetchScalarGridSpec(
        num_scalar_prefetch=1,
        grid=(n_tok // tile,),
        in_specs=[
            pl.BlockSpec((TOP_K, tile), lambda i, info: (0, i), memory_space=pltpu.SMEM),
            pl.BlockSpec((tile, half), lambda i, info: (i, 0)),
            pl.BlockSpec((D_MODEL, D_EXPERT), c2),
            pl.BlockSpec((D_MODEL, D_EXPERT), c2),
            pl.BlockSpec((D_EXPERT, D_MODEL), c2),
        ],
        out_specs=[
            pl.BlockSpec(memory_space=pl.ANY),
            pl.BlockSpec((tile, D_MODEL), lambda i, info: (i, 0)),
        ],
        scratch_shapes=[pltpu.VMEM((MOE_TILE, half), F32), pltpu.SemaphoreType.DMA],
    )
    return pl.pallas_call(
        _dispatch_body,
        grid_spec=grid_spec,
        out_shape=[
            jax.ShapeDtypeStruct((rows_sorted, half), F32),
            jax.ShapeDtypeStruct((n_tok, D_MODEL), F32),
        ],
        compiler_params=_params(("arbitrary",)),
        name="dispatch",
    )(info, pos, x1, sh_w1, sh_w3, sh_w2)


def _moe_body(texp_ref, info_ref, xs_ref, w1_hbm, w3_hbm, w2_hbm, ys_ref,
              w1f_scr, w3f_scr, w2f_scr, w1b_scr, w3b_scr, w2b_scr, sems):
    t = pl.program_id(0)
    ntiles = info_ref[3, 0]
    tt = jnp.minimum(t, ntiles - 1)
    prev = jnp.maximum(tt - 1, 0)
    expert = texp_ref[0, tt]
    fresh = (t == 0) | (expert != texp_ref[0, prev])
    valid = t < ntiles

    def weight_copies(ex, slot):
        return [
            pltpu.make_async_copy(w1_hbm.at[ex], w1f_scr.at[slot], sems.at[0, slot]),
            pltpu.make_async_copy(w3_hbm.at[ex], w3f_scr.at[slot], sems.at[1, slot]),
            pltpu.make_async_copy(w2_hbm.at[ex], w2f_scr.at[slot], sems.at[2, slot]),
        ]

    @pl.when(t == 0)
    def _():
        for c in weight_copies(expert, info_ref[5, expert]):
            c.start()

    @pl.when(valid & fresh)
    def _():
        slot = info_ref[5, expert]
        for c in weight_copies(expert, slot):
            c.wait()
        w1b_scr[...] = w1f_scr[slot].astype(BF16)
        w3b_scr[...] = w3f_scr[slot].astype(BF16)
        w2b_scr[...] = w2f_scr[slot].astype(BF16)
        nxt = info_ref[4, expert]

        @pl.when(nxt < N_EXPERTS)
        def _():
            for c in weight_copies(nxt, 1 - slot):
                c.start()

    @pl.when(valid)
    def _():
        ys_ref[...] = _swiglu_packed(xs_ref[...], w1b_scr, w3b_scr, w2b_scr)

    @pl.when(jnp.logical_not(valid))
    def _():
        ys_ref[...] = jnp.zeros_like(ys_ref)


def _moe(texp, info, xs, ex_w1, ex_w3, ex_w2, n_tiles_max):
    half = D_MODEL

    def tile_idx(t, texp, info):
        return jnp.minimum(t, info[3, 0] - 1)

    grid_spec = pltpu.PrefetchScalarGridSpec(
        num_scalar_prefetch=2,
        grid=(n_tiles_max,),
        in_specs=[
            pl.BlockSpec((MOE_TILE, half), lambda t, texp, info: (tile_idx(t, texp, info), 0)),
            pl.BlockSpec(memory_space=pl.ANY),
            pl.BlockSpec(memory_space=pl.ANY),
            pl.BlockSpec(memory_space=pl.ANY),
        ],
        out_specs=pl.BlockSpec((MOE_TILE, half), lambda t, texp, info: (t, 0)),
        scratch_shapes=[
            pltpu.VMEM((2, D_MODEL, D_EXPERT), F32),
            pltpu.VMEM((2, D_MODEL, D_EXPERT), F32),
            pltpu.VMEM((2, D_EXPERT, D_MODEL), F32),
            pltpu.VMEM((D_MODEL, D_EXPERT), BF16),
            pltpu.VMEM((D_MODEL, D_EXPERT), BF16),
            pltpu.VMEM((D_EXPERT, D_MODEL), BF16),
            pltpu.SemaphoreType.DMA((3, 2)),
        ],
    )
    return pl.pallas_call(
        _moe_body,
        grid_spec=grid_spec,
        out_shape=jax.ShapeDtypeStruct(xs.shape, F32),
        compiler_params=_params(("arbitrary",)),
        name="moe_experts",
    )(texp, info, xs, ex_w1, ex_w3, ex_w2)


def _combine_body(n_prompt_tiles, pos_ref, ys_hbm, gate_ref, x1_ref, sh_ref, g_ref, b_ref,
                  yp_ref, ysm_ref, buf_scr, y_scr, sem):
    i = pl.program_id(0)
    tm = x1_ref.shape[0]

    def issue(g, carry):
        row0 = pl.multiple_of(g * SUBLANES, SUBLANES)
        for j in range(SUBLANES):
            for k in range(TOP_K):
                pltpu.make_async_copy(
                    ys_hbm.at[pl.ds(pos_ref[k, row0 + j], 1)],
                    buf_scr.at[k, pl.ds(row0 + j, 1)], sem,
                ).start(priority=k % 2)
        return carry

    lax.fori_loop(0, tm // SUBLANES, issue, 0)

    for k in range(TOP_K):
        pltpu.make_async_copy(ys_hbm.at[pl.ds(0, tm)], buf_scr.at[k], sem).wait()

    gates = gate_ref[...]
    acc = sh_ref[...]
    for k in range(TOP_K):
        acc = acc + gates[:, k:k + 1] * buf_scr[k]
    y = _layernorm(ALPHA * x1_ref[...] + acc, g_ref[...], b_ref[...])

    @pl.when(i < n_prompt_tiles)
    def _():
        steps = tm // NSEQ
        for c in range(D_MODEL // LANES):
            y_scr[c] = y[:, c * LANES:(c + 1) * LANES]
        for s in range(NSEQ):
            for c in range(D_MODEL // LANES):
                yp_ref[s, :, c * LANES:(c + 1) * LANES] = y_scr[c, pl.ds(s, steps, stride=NSEQ), :]

    @pl.when(i >= n_prompt_tiles)
    def _():
        ysm_ref[...] = y


def _combine(pos, ys, gate_t, x1, sh_out, ln2_g, ln2_b, tile, n_prompt):
    n_tok = x1.shape[0]
    n_s = n_tok - n_prompt
    assert n_prompt % tile == 0 and n_s % tile == 0
    npt = n_prompt // tile
    steps = tile // NSEQ
    c2 = lambda i: (0, 0)
    return pl.pallas_call(
        functools.partial(_combine_body, npt),
        grid=(n_tok // tile,),
        in_specs=[
            pl.BlockSpec((TOP_K, tile), lambda i: (0, i), memory_space=pltpu.SMEM),
            pl.BlockSpec(memory_space=pl.ANY),
            pl.BlockSpec((tile, LANES), lambda i: (i, 0)),
            pl.BlockSpec((tile, D_MODEL), lambda i: (i, 0)),
            pl.BlockSpec((tile, D_MODEL), lambda i: (i, 0)),
            pl.BlockSpec((1, D_MODEL), c2),
            pl.BlockSpec((1, D_MODEL), c2),
        ],
        out_specs=[
            pl.BlockSpec((NSEQ, steps, D_MODEL), lambda i: (0, jnp.minimum(i, npt - 1), 0)),
            pl.BlockSpec((tile, D_MODEL), lambda i: (jnp.maximum(i - npt, 0), 0)),
        ],
        out_shape=[
            jax.ShapeDtypeStruct((NSEQ, n_prompt // NSEQ, D_MODEL), F32),
            jax.ShapeDtypeStruct((n_s, D_MODEL), F32),
        ],
        scratch_shapes=[
            pltpu.VMEM((TOP_K, tile, D_MODEL), F32),
            pltpu.VMEM((D_MODEL // LANES, tile, LANES), F32),
            pltpu.SemaphoreType.DMA,
        ],
        compiler_params=_params(("arbitrary",)),
        name="combine",
    )(pos, ys, gate_t, x1, sh_out, ln2_g, ln2_b)


def _pick_tile(n, cap, mult):
    best = mult
    for t in range(mult, cap + 1, mult):
        if n % t == 0:
            best = t
    assert n % best == 0
    return best


def kernel(x_prompt, x_sample, state_rglru_h, state_conv, state_s5_re, state_s5_im, meta_tokens, ln_in_g, ln_in_b, w_in, b_in, conv_w, conv_b, rg_wa, rg_ba, rg_wi, rg_bi, rg_lambda, s5_a_re, s5_a_im, s5_b_re, s5_b_im, s5_c_re, s5_c_im, s5_d, s5_log_dt, glu_w, glu_b, proj_a, proj_b, w_o, ln1_g, ln1_b, router_w, router_bias, ex_w1, ex_w3, ex_w2, sh_w1, sh_w3, sh_w2, ln2_g, ln2_b):
    bp, seq, d = x_prompt.shape
    n_s = x_sample.shape[0]
    assert bp == NSEQ and d == D_MODEL and x_sample.shape[1] == 1
    assert w_in.shape[0] == DEPTH
    n_prompt = bp * seq
    n_tok = n_prompt + n_s
    meta_rows = NSEQ * N_META
    n1 = n_tok + 2 * meta_rows
    row = lambda v: v.reshape(1, -1)

    xflat = jnp.concatenate([
        jnp.transpose(x_prompt, (1, 0, 2)).reshape(n_prompt, d),
        x_sample.reshape(n_s, d),
        jnp.repeat(meta_tokens, NSEQ, axis=0),
        jnp.zeros((meta_rows, d), F32),
    ], axis=0)

    tile1 = _pick_tile(n1, 1056, 2 * SUBLANES)
    xa, z16 = _inproj(xflat, row(ln_in_g), row(ln_in_b), w_in[0].astype(BF16), b_in, tile1)

    a_r, a_i, bb_r, bb_i = _s5_prep(s5_a_re[0], s5_a_im[0], s5_log_dt[0], s5_b_re[0], s5_b_im[0])
    bg = S5_BLOCK_GROUPS
    bd_b = jnp.concatenate([
        _block_diag(bb_r.reshape(S5_BLOCKS, bg, S5_CH, S5_N)),
        _block_diag(bb_i.reshape(S5_BLOCKS, bg, S5_CH, S5_N)),
    ], axis=2).astype(BF16)
    c_t = lambda c: jnp.transpose(c[0], (0, 2, 1)).reshape(S5_BLOCKS, bg, S5_N, S5_CH)
    bd_cre = _block_diag(c_t(s5_c_re)).astype(BF16)
    bd_cim = _block_diag(c_t(s5_c_im)).astype(BF16)
    heads_per_blk = RG_HEADS // (D_RNN // MXU_DIM)
    rg_blk = lambda w: _block_diag(
        w[0].reshape(D_RNN // MXU_DIM, heads_per_blk, D_RNN // RG_HEADS, D_RNN // RG_HEADS)
    ).astype(BF16)
    weights = (conv_w[0], conv_b, rg_blk(rg_wa), rg_blk(rg_wi), rg_ba, rg_bi, rg_lambda,
               bd_b, bd_cre, bd_cim, row(s5_d[0]), glu_w[0].astype(BF16), glu_b)

    chunk = n_s + 2 * meta_rows
    assert n_prompt % chunk == 0
    ua, gl, p_h, p_conv, p_s5r, p_s5i, s_h, s_conv, s_s5r, s_s5i = _mixer(
        xa, z16, a_r, a_i, state_rglru_h[0],
        state_conv[0].reshape(n_s, (CONV_W - 1) * D_RNN),
        state_s5_re[0].reshape(n_s, S5_STATE), state_s5_im[0].reshape(n_s, S5_STATE),
        weights, n_prompt, n_s, n_tok, meta_rows, chunk)

    wr_t = jnp.transpose(router_w[0])
    wr_hi = wr_t.astype(BF16)
    wr_lo = (wr_t - wr_hi.astype(F32)).astype(BF16)
    tile3 = _pick_tile(n_tok, POST_TILE, 2 * SUBLANES)
    x1 = _post_mixer(
        ua, gl, z16, xflat, row(ln_in_g), row(ln_in_b), proj_a[0].astype(BF16),
        proj_b[0].astype(BF16), w_o[0].astype(BF16), ln1_g, ln1_b, n_tok, tile3)
    tile4 = _pick_tile(n_tok, TOKEN_TILE, LANES)
    eidx, rank, gate_t, cnt = _router(
        x1, wr_hi, wr_lo, router_bias[0].reshape(N_EXPERTS, 1), tile4)

    rows_max = n_tok * TOP_K + N_EXPERTS * (MOE_TILE - 1)
    n_tiles_max = -(-rows_max // MOE_TILE)
    pos, texp, info = _positions(eidx, rank, cnt, n_tiles_max)
    xs, sh_out = _dispatch(info, pos, x1, sh_w1[0].astype(BF16), sh_w3[0].astype(BF16),
                           sh_w2[0].astype(BF16), n_tiles_max * MOE_TILE, tile4)
    ys = _moe(texp, info, xs, ex_w1[0], ex_w3[0], ex_w2[0], n_tiles_max)
    y_prompt, y_sample = _combine(pos, ys, gate_t, x1, sh_out, ln2_g, ln2_b,
                                  _pick_tile(n_s, FIN_TILE, LANES), n_prompt)

    dt = x_prompt.dtype
    y_sample = y_sample.reshape(n_s, 1, d)
    conv_p = jnp.transpose(p_conv.reshape(CONV_W - 1, bp, D_RNN), (1, 0, 2))
    s5_shape = (S5_GROUPS, S5_N)
    return (y_prompt.astype(dt), y_sample.astype(dt),
            p_h[None], conv_p[None],
            p_s5r.reshape(1, bp, *s5_shape), p_s5i.reshape(1, bp, *s5_shape),
            s_h[None], s_conv.reshape(1, n_s, CONV_W - 1, D_RNN),
            s_s5r.reshape(1, n_s, *s5_shape), s_s5i.reshape(1, n_s, *s5_shape))
```

```python
import functools
import math

import jax
import jax.numpy as jnp
from jax import lax
from jax.experimental import pallas as pl
from jax.experimental.pallas import tpu as pltpu

F32 = jnp.float32
BF16 = jnp.bfloat16
I32 = jnp.int32

D_MODEL = 2048
D_RNN = D_MODEL // 2
D_S5 = D_MODEL // 2
N_IN = 2 * D_RNN + D_S5 + 2 * D_MODEL
RG_HEADS = 8
CONV_W = 4
LRU_C = 8.0
S5_CH = 16
S5_GROUPS = D_S5 // S5_CH
S5_N = 64
S5_STATE = S5_GROUPS * S5_N
N_EXPERTS = 64
TOP_K = 8
N_GROUPS = 8
GROUP_SIZE = N_EXPERTS // N_GROUPS
TOPK_GROUPS = 4
D_EXPERT = 512
ROUTED_SCALE = 2.5
LN_EPS = 1e-5
N_META = 16
DEPTH = 1
ALPHA = (2.0 * DEPTH) ** 0.25

SUBLANES = 8
LANES = 128
MXU_DIM = 256
VMEM_LIMIT = 56 * 1024 * 1024

NSEQ = 4
S5_BLOCK_GROUPS = MXU_DIM // S5_CH
S5_BLOCKS = S5_GROUPS // S5_BLOCK_GROUPS
S5_BLOCK_STATE = S5_BLOCK_GROUPS * S5_N
IN_TILE_N = 1024
POST_TILE = 320
TOKEN_TILE = 640
MOE_TILE = 256
FIN_TILE = 128


def _params(sem, vmem=VMEM_LIMIT):
    return pltpu.CompilerParams(dimension_semantics=sem, vmem_limit_bytes=vmem)


def _dot(a, b):
    return jnp.dot(a, b, preferred_element_type=F32)


def _layernorm(x, g, b):
    mu = jnp.mean(x, axis=-1, keepdims=True)
    xc = x - mu
    var = jnp.mean(xc * xc, axis=-1, keepdims=True)
    return xc * lax.rsqrt(var + LN_EPS) * g + b


def _sigmoid(x):
    return 1.0 / (1.0 + jnp.exp(-x))


def _gelu(x):
    c = math.sqrt(2.0 / math.pi)
    return 0.5 * x * (1.0 + jnp.tanh(c * (x + 0.044715 * (x * x * x))))


def _silu(x):
    return x * _sigmoid(x)


def _softplus(x):
    return jnp.maximum(x, 0.0) + jnp.log1p(jnp.exp(-jnp.abs(x)))


def _neg_expm1(x):
    poly = x * (1.0 + x * (1.0 / 2) * (1.0 + x * (1.0 / 3) * (1.0 + x * (1.0 / 4) * (
        1.0 + x * (1.0 / 5) * (1.0 + x * (1.0 / 6) * (1.0 + x * (1.0 / 7)))))))
    return -jnp.where(x > -0.25, poly, jnp.exp(x) - 1.0)


def _s5_prep_body(are_ref, aim_ref, ldt_ref, bre_ref, bim_ref,
                  abr_ref, abi_ref, bbr_ref, bbi_ref):
    a_re = are_ref[...]
    a_im = aim_ref[...]
    dt = jnp.exp(ldt_ref[...])
    mag = jnp.exp(a_re * dt)
    ab_r = mag * jnp.cos(a_im * dt)
    ab_i = mag * jnp.sin(a_im * dt)
    den = a_re * a_re + a_im * a_im
    nr = ab_r - 1.0
    cr = (nr * a_re + ab_i * a_im) / den
    ci = (ab_i * a_re - nr * a_im) / den
    b_re = bre_ref[...]
    b_im = bim_ref[...]
    abr_ref[...] = ab_r
    abi_ref[...] = ab_i
    bbr_ref[...] = cr * b_re - ci * b_im
    bbi_ref[...] = cr * b_im + ci * b_re


def _s5_prep(a_re, a_im, log_dt, b_re, b_im):
    g, n, c = b_re.shape
    wide = c * n
    bc = lambda v: jnp.broadcast_to(v[:, None, :], (g, c, n)).reshape(g, wide)
    are_x = bc(a_re)
    aim_x = bc(a_im)
    ldt_x = jnp.broadcast_to(log_dt[:, None], (g, wide))
    bre_x = jnp.transpose(b_re, (0, 2, 1)).reshape(g, wide)
    bim_x = jnp.transpose(b_im, (0, 2, 1)).reshape(g, wide)
    shp = jax.ShapeDtypeStruct((g, wide), F32)
    abr, abi, bbr, bbi = pl.pallas_call(
        _s5_prep_body, out_shape=(shp, shp, shp, shp), name="s5_prep",
    )(are_x, aim_x, ldt_x, bre_x, bim_x)
    a_r = abr[:, :n].reshape(1, g * n)
    a_i = abi[:, :n].reshape(1, g * n)
    return a_r, a_i, bbr.reshape(g, c, n), bbi.reshape(g, c, n)


def _block_diag(x):
    k, g, a, b = x.shape
    eye = jnp.eye(g, dtype=x.dtype)
    return jnp.einsum("kgab,gh->kgahb", x, eye).reshape(k, g * a, g * b)


def _inproj_body(x_ref, g_ref, b_ref, w_ref, bias_ref, xa_ref, z_ref, xn_scr):
    j = pl.program_id(1)

    @pl.when(j == 0)
    def _():
        xn_scr[...] = _layernorm(x_ref[...], g_ref[...], b_ref[...]).astype(BF16)

    z = _dot(xn_scr[...], w_ref[...]) + bias_ref[...]
    n_xa = D_RNN // IN_TILE_N

    @pl.when(j < n_xa)
    def _():
        xa_ref[...] = z

    @pl.when(j >= n_xa)
    def _():
        z_ref[...] = z.astype(BF16)


def _inproj(xflat, ln_g, ln_b, w_in_bf, b_in, tile):
    n1 = xflat.shape[0]
    n_xa = D_RNN // IN_TILE_N
    grid = (n1 // tile, N_IN // IN_TILE_N)
    return pl.pallas_call(
        _inproj_body,
        grid=grid,
        in_specs=[
            pl.BlockSpec((tile, D_MODEL), lambda i, j: (i, 0)),
            pl.BlockSpec((1, D_MODEL), lambda i, j: (0, 0)),
            pl.BlockSpec((1, D_MODEL), lambda i, j: (0, 0)),
            pl.BlockSpec((D_MODEL, IN_TILE_N), lambda i, j: (0, j)),
            pl.BlockSpec((1, IN_TILE_N), lambda i, j: (0, j)),
        ],
        out_specs=[
            pl.BlockSpec((tile, IN_TILE_N), lambda i, j: (i, jnp.minimum(j, n_xa - 1))),
            pl.BlockSpec((tile, IN_TILE_N), lambda i, j: (i, jnp.maximum(j - n_xa, 0))),
        ],
        out_shape=[
            jax.ShapeDtypeStruct((n1, D_RNN), F32),
            jax.ShapeDtypeStruct((n1, N_IN - D_RNN), BF16),
        ],
        scratch_shapes=[pltpu.VMEM((tile, D_MODEL), BF16)],
        compiler_params=_params(("arbitrary", "arbitrary")),
        name="in_proj",
    )(xflat, ln_g, ln_b, w_in_bf, b_in)


def _rg_gates(xc, wa_ref, wi_ref, ba, bi, sp):
    xcb = xc.astype(BF16)
    nblk = D_RNN // MXU_DIM
    r_pre = jnp.concatenate(
        [_dot(xcb[:, k * MXU_DIM:(k + 1) * MXU_DIM], wa_ref[k]) for k in range(nblk)], axis=1)
    i_pre = jnp.concatenate(
        [_dot(xcb[:, k * MXU_DIM:(k + 1) * MXU_DIM], wi_ref[k]) for k in range(nblk)], axis=1)
    r = _sigmoid(r_pre + ba)
    i = _sigmoid(i_pre + bi)
    log_a = (-LRU_C * r) * sp
    a = jnp.exp(log_a)
    u = jnp.sqrt(_neg_expm1(2.0 * log_a)) * (i * xc)
    return a, u


def _odd_rows(width):
    return lax.broadcasted_iota(I32, (SUBLANES, width), 0) >= NSEQ


def _mixer_chunk(rows, xa_ref, ya_ref, us_ref, ua_ref, gl_ref, w, s):
    (cw_ref, cb_ref, wa_ref, wi_ref, ba_ref, bi_ref, lam_ref, bdb_ref, cre_ref, cim_ref,
     d_ref, gluw_ref, glub_ref) = w
    (ext_scr, a_scr, u_scr, hs_scr, bu_scr, hst_scr, y_scr, tail_scr, hcar_scr, s5car_scr,
     cst_scr) = s
    emit = ua_ref is not None
    halo = NSEQ * CONV_W
    ngroups = rows // SUBLANES

    ext_scr[pl.ds(0, halo), :] = tail_scr[...]
    ext_scr[pl.ds(halo, rows), :] = xa_ref[...]
    xc = cb_ref[...] + cw_ref[pl.ds(CONV_W - 1, 1), :] * ext_scr[pl.ds(halo, rows), :]
    for j in range(1, CONV_W):
        xc = xc + cw_ref[pl.ds(CONV_W - 1 - j, 1), :] * ext_scr[pl.ds(halo - NSEQ * j, rows), :]
    tail_scr[...] = ext_scr[pl.ds(rows, halo), :]

    sp = _softplus(-lam_ref[...])
    a, u = _rg_gates(xc, wa_ref, wi_ref, ba_ref[...], bi_ref[...], sp)
    a_scr[pl.ds(0, rows), :] = a
    u_scr[pl.ds(0, rows), :] = u
    odd = _odd_rows(D_RNN)

    def rg_body(g, c):
        row = pl.multiple_of(g * SUBLANES, SUBLANES)
        a_v = a_scr[pl.ds(row, SUBLANES), :]
        u_v = u_scr[pl.ds(row, SUBLANES), :]
        hl = u_v + jnp.where(odd, a_v * pltpu.roll(u_v, NSEQ, 0), 0.0)
        p = jnp.where(odd, a_v * pltpu.roll(a_v, NSEQ, 0), a_v)
        hs_scr[pl.ds(row, SUBLANES), :] = hl + p * c
        q = jnp.where(odd, hl, pltpu.roll(hl, NSEQ, 0))
        pp = jnp.where(odd, p, pltpu.roll(p, NSEQ, 0))
        return q + pp * c

    hcar_scr[...] = lax.fori_loop(0, ngroups, rg_body, hcar_scr[...])
    if emit:
        ua_ref[...] = (hs_scr[pl.ds(0, rows), :] * _gelu(ya_ref[...].astype(F32))).astype(BF16)

    odd_s = _odd_rows(S5_BLOCK_STATE)
    for kb in range(S5_BLOCKS):
        lo = kb * S5_BLOCK_STATE
        ub = us_ref[:, kb * MXU_DIM:(kb + 1) * MXU_DIM]
        bu_scr[pl.ds(0, rows), :] = _dot(ub, bdb_ref[kb])
        aor, aoi, pr, pi, a2r, a2i = [cst_scr[i, :, lo:lo + S5_BLOCK_STATE] for i in range(6)]

        def s5_body(g, c, aor=aor, aoi=aoi, pr=pr, pi=pi, a2r=a2r, a2i=a2i):
            cr, ci = c
            row = pl.multiple_of(g * SUBLANES, SUBLANES)
            bur = bu_scr[pl.ds(row, SUBLANES), 0:S5_BLOCK_STATE]
            bui = bu_scr[pl.ds(row, SUBLANES), S5_BLOCK_STATE:2 * S5_BLOCK_STATE]
            sr = pltpu.roll(bur, NSEQ, 0)
            si = pltpu.roll(bui, NSEQ, 0)
            hlr = bur + aor * sr - aoi * si
            hli = bui + aor * si + aoi * sr
            hr = hlr + pr * cr - pi * ci
            hi = hli + pr * ci + pi * cr
            if emit:
                hst_scr[pl.ds(row, SUBLANES), 0:S5_BLOCK_STATE] = hr
                hst_scr[pl.ds(row, SUBLANES), S5_BLOCK_STATE:2 * S5_BLOCK_STATE] = hi
            return (jnp.where(odd_s, hr, pltpu.roll(hr, NSEQ, 0)),
                    jnp.where(odd_s, hi, pltpu.roll(hi, NSEQ, 0)))

        cr, ci = lax.fori_loop(
            0, ngroups, s5_body,
            (s5car_scr[0, :, lo:lo + S5_BLOCK_STATE], s5car_scr[1, :, lo:lo + S5_BLOCK_STATE]))
        s5car_scr[0, :, lo:lo + S5_BLOCK_STATE] = cr
        s5car_scr[1, :, lo:lo + S5_BLOCK_STATE] = ci
        if emit:
            hre = hst_scr[pl.ds(0, rows), 0:S5_BLOCK_STATE].astype(BF16)
            him = hst_scr[pl.ds(0, rows), S5_BLOCK_STATE:2 * S5_BLOCK_STATE].astype(BF16)
            y = _dot(hre, cre_ref[kb]) - _dot(him, cim_ref[kb])
            y = y + d_ref[:, kb * MXU_DIM:(kb + 1) * MXU_DIM] * ub.astype(F32)
            y_scr[pl.ds(0, rows), kb * MXU_DIM:(kb + 1) * MXU_DIM] = y

    if emit:
        g5 = _gelu(y_scr[pl.ds(0, rows), :])
        gate = _sigmoid(_dot(g5.astype(BF16), gluw_ref[...]) + glub_ref[...])
        gl_ref[...] = (g5 * gate).astype(BF16)


def _mixer_body(meta_rows, n_s, xa_ref, ya_ref, us_ref, xam_ref, usm_ref, ar_ref, ai_ref,
                h0_ref, cbuf_ref, s5r0_ref, s5i0_ref, *rest):
    w = rest[:13]
    (ua_ref, gl_ref, hout_ref, convout_ref, s5r_ref, s5i_ref,
     h1_ref, cnew_ref, s5r1_ref, s5i1_ref) = rest[13:23]
    s = rest[23:]
    y_scr, tail_scr, hcar_scr, s5car_scr, cst_scr = s[6], s[7], s[8], s[9], s[10]
    c = pl.program_id(0)
    rows = xa_ref.shape[0]
    nchunks = pl.num_programs(0) - 1

    @pl.when(c == nchunks)
    def _():
        head = lambda r: r.at[pl.ds(0, n_s)]
        _sample_step(head(xa_ref), head(ya_ref), head(us_ref), h0_ref, cbuf_ref, s5r0_ref,
                     s5i0_ref, ar_ref, ai_ref, w, head(ua_ref), head(gl_ref), h1_ref, cnew_ref,
                     s5r1_ref, s5i1_ref, head(y_scr))
        ua_ref[pl.ds(n_s, rows - n_s), :] = jnp.zeros((rows - n_s, D_RNN), BF16)
        gl_ref[pl.ds(n_s, rows - n_s), :] = jnp.zeros((rows - n_s, D_S5), BF16)

    @pl.when(c < nchunks)
    def _():
        _prompt_step(meta_rows, nchunks, xa_ref, ya_ref, us_ref, xam_ref, usm_ref, ar_ref, ai_ref,
                     w, ua_ref, gl_ref, hout_ref, convout_ref, s5r_ref, s5i_ref, s)


def _prompt_step(meta_rows, nchunks, xa_ref, ya_ref, us_ref, xam_ref, usm_ref, ar_ref, ai_ref,
                 w, ua_ref, gl_ref, hout_ref, convout_ref, s5r_ref, s5i_ref, s):
    tail_scr, hcar_scr, s5car_scr, cst_scr = s[7], s[8], s[9], s[10]
    c = pl.program_id(0)
    rows = xa_ref.shape[0]

    @pl.when(c == 0)
    def _():
        odd = _odd_rows(S5_STATE)
        ar = jnp.broadcast_to(ar_ref[...], (SUBLANES, S5_STATE))
        ai = jnp.broadcast_to(ai_ref[...], (SUBLANES, S5_STATE))
        a2r = ar * ar - ai * ai
        a2i = 2.0 * (ar * ai)
        cst_scr[0] = jnp.where(odd, ar, 0.0)
        cst_scr[1] = jnp.where(odd, ai, 0.0)
        cst_scr[2] = jnp.where(odd, a2r, ar)
        cst_scr[3] = jnp.where(odd, a2i, ai)
        cst_scr[4] = a2r
        cst_scr[5] = a2i
        tail_scr[...] = jnp.zeros_like(tail_scr)
        hcar_scr[...] = jnp.zeros_like(hcar_scr)
        s5car_scr[...] = jnp.zeros_like(s5car_scr)
        _mixer_chunk(meta_rows, xam_ref, None, usm_ref, None, None, w, s)

    _mixer_chunk(rows, xa_ref, ya_ref, us_ref, ua_ref, gl_ref, w, s)

    @pl.when(c == nchunks - 1)
    def _():
        hout_ref[...] = hcar_scr[pl.ds(NSEQ, NSEQ), :]
        convout_ref[...] = tail_scr[pl.ds(NSEQ, NSEQ * (CONV_W - 1)), :]
        s5r_ref[...] = s5car_scr[0, pl.ds(NSEQ, NSEQ), :]
        s5i_ref[...] = s5car_scr[1, pl.ds(NSEQ, NSEQ), :]


def _mixer_weight_specs(nidx):
    z = (0,) * nidx if nidx else ()
    c2 = lambda *_: (0, 0)
    c3 = lambda *_: (0, 0, 0)
    nblk = D_RNN // MXU_DIM
    return [
        pl.BlockSpec((CONV_W, D_RNN), c2),
        pl.BlockSpec((1, D_RNN), c2),
        pl.BlockSpec((nblk, MXU_DIM, MXU_DIM), c3),
        pl.BlockSpec((nblk, MXU_DIM, MXU_DIM), c3),
        pl.BlockSpec((1, D_RNN), c2),
        pl.BlockSpec((1, D_RNN), c2),
        pl.BlockSpec((1, D_RNN), c2),
        pl.BlockSpec((S5_BLOCKS, MXU_DIM, 2 * S5_BLOCK_STATE), c3),
        pl.BlockSpec((S5_BLOCKS, S5_BLOCK_STATE, MXU_DIM), c3),
        pl.BlockSpec((S5_BLOCKS, S5_BLOCK_STATE, MXU_DIM), c3),
        pl.BlockSpec((1, D_S5), c2),
        pl.BlockSpec((D_S5, D_S5), c2),
        pl.BlockSpec((1, D_S5), c2),
    ]


def _mixer(xa, z16, a_r, a_i, h0, cbuf, s5r0, s5i0, weights, n_prompt, n_s, meta_row0, meta_rows,
           chunk):
    nchunks = n_prompt // chunk
    n1 = xa.shape[0]
    assert n1 == n_prompt + chunk and n_s <= chunk
    meta_blk = meta_row0 // meta_rows
    halo = NSEQ * CONV_W
    c2 = lambda c: (0, 0)
    in_specs = [
        pl.BlockSpec((chunk, D_RNN), lambda c: (c, 0)),
        pl.BlockSpec((chunk, D_RNN), lambda c: (c, 0)),
        pl.BlockSpec((chunk, D_S5), lambda c: (c, 1)),
        pl.BlockSpec((meta_rows, D_RNN), lambda c: (meta_blk, 0)),
        pl.BlockSpec((meta_rows, D_S5), lambda c: (meta_blk, 1)),
        pl.BlockSpec((1, S5_STATE), c2),
        pl.BlockSpec((1, S5_STATE), c2),
        pl.BlockSpec((n_s, D_RNN), c2),
        pl.BlockSpec((n_s, (CONV_W - 1) * D_RNN), c2),
        pl.BlockSpec((n_s, S5_STATE), c2),
        pl.BlockSpec((n_s, S5_STATE), c2),
    ] + _mixer_weight_specs(1)
    out_specs = [
        pl.BlockSpec((chunk, D_RNN), lambda c: (c, 0)),
        pl.BlockSpec((chunk, D_S5), lambda c: (c, 0)),
        pl.BlockSpec((NSEQ, D_RNN), c2),
        pl.BlockSpec((NSEQ * (CONV_W - 1), D_RNN), c2),
        pl.BlockSpec((NSEQ, S5_STATE), c2),
        pl.BlockSpec((NSEQ, S5_STATE), c2),
        pl.BlockSpec((n_s, D_RNN), c2),
        pl.BlockSpec((n_s, (CONV_W - 1) * D_RNN), c2),
        pl.BlockSpec((n_s, S5_STATE), c2),
        pl.BlockSpec((n_s, S5_STATE), c2),
    ]
    out_shape = [
        jax.ShapeDtypeStruct((n1, D_RNN), BF16),
        jax.ShapeDtypeStruct((n1, D_S5), BF16),
        jax.ShapeDtypeStruct((NSEQ, D_RNN), F32),
        jax.ShapeDtypeStruct((NSEQ * (CONV_W - 1), D_RNN), F32),
        jax.ShapeDtypeStruct((NSEQ, S5_STATE), F32),
        jax.ShapeDtypeStruct((NSEQ, S5_STATE), F32),
        jax.ShapeDtypeStruct((n_s, D_RNN), F32),
        jax.ShapeDtypeStruct((n_s, (CONV_W - 1) * D_RNN), F32),
        jax.ShapeDtypeStruct((n_s, S5_STATE), F32),
        jax.ShapeDtypeStruct((n_s, S5_STATE), F32),
    ]
    scratch = [
        pltpu.VMEM((chunk + halo, D_RNN), F32),
        pltpu.VMEM((chunk, D_RNN), F32),
        pltpu.VMEM((chunk, D_RNN), F32),
        pltpu.VMEM((chunk, D_RNN), F32),
        pltpu.VMEM((chunk, 2 * S5_BLOCK_STATE), F32),
        pltpu.VMEM((chunk, 2 * S5_BLOCK_STATE), F32),
        pltpu.VMEM((chunk, D_S5), F32),
        pltpu.VMEM((halo, D_RNN), F32),
        pltpu.VMEM((SUBLANES, D_RNN), F32),
        pltpu.VMEM((2, SUBLANES, S5_STATE), F32),
        pltpu.VMEM((6, SUBLANES, S5_STATE), F32),
    ]
    return pl.pallas_call(
        functools.partial(_mixer_body, meta_rows, n_s),
        grid=(nchunks + 1,),
        in_specs=in_specs,
        out_specs=out_specs,
        out_shape=out_shape,
        scratch_shapes=scratch,
        compiler_params=_params(("arbitrary",)),
        name="mixer",
    )(xa, z16, z16, xa, z16, a_r, a_i, h0, cbuf, s5r0, s5i0, *weights)


def _sample_step(xa_ref, ya_ref, us_ref, h0_ref, cbuf_ref, s5r0_ref, s5i0_ref, ar_ref, ai_ref,
                 w, ua_ref, gl_ref, h1_ref, cnew_ref, s5r1_ref, s5i1_ref, y_scr):
    (cw_ref, cb_ref, wa_ref, wi_ref, ba_ref, bi_ref, lam_ref, bdb_ref, cre_ref, cim_ref,
     d_ref, gluw_ref, glub_ref) = w
    xa = xa_ref[...]
    xc = cb_ref[...] + cw_ref[pl.ds(CONV_W - 1, 1), :] * xa
    for k in range(CONV_W - 1):
        xc = xc + cw_ref[pl.ds(k, 1), :] * cbuf_ref[:, k * D_RNN:(k + 1) * D_RNN]
    for k in range(CONV_W - 2):
        cnew_ref[:, k * D_RNN:(k + 1) * D_RNN] = cbuf_ref[:, (k + 1) * D_RNN:(k + 2) * D_RNN]
    cnew_ref[:, (CONV_W - 2) * D_RNN:(CONV_W - 1) * D_RNN] = xa

    sp = _softplus(-lam_ref[...])
    a, u = _rg_gates(xc, wa_ref, wi_ref, ba_ref[...], bi_ref[...], sp)
    h1 = a * h0_ref[...] + u
    h1_ref[...] = h1
    ua_ref[...] = (h1 * _gelu(ya_ref[...].astype(F32))).astype(BF16)

    for kb in range(S5_BLOCKS):
        lo = kb * S5_BLOCK_STATE
        ub = us_ref[:, kb * MXU_DIM:(kb + 1) * MXU_DIM]
        bu = _dot(ub, bdb_ref[kb])
        ar = ar_ref[:, lo:lo + S5_BLOCK_STATE]
        ai = ai_ref[:, lo:lo + S5_BLOCK_STATE]
        h0r = s5r0_ref[:, lo:lo + S5_BLOCK_STATE]
        h0i = s5i0_ref[:, lo:lo + S5_BLOCK_STATE]
        hr = bu[:, 0:S5_BLOCK_STATE] + ar * h0r - ai * h0i
        hi = bu[:, S5_BLOCK_STATE:2 * S5_BLOCK_STATE] + ar * h0i + ai * h0r
        s5r1_ref[:, lo:lo + S5_BLOCK_STATE] = hr
        s5i1_ref[:, lo:lo + S5_BLOCK_STATE] = hi
        y = _dot(hr.astype(BF16), cre_ref[kb]) - _dot(hi.astype(BF16), cim_ref[kb])
        y_scr[:, kb * MXU_DIM:(kb + 1) * MXU_DIM] = (
            y + d_ref[:, kb * MXU_DIM:(kb + 1) * MXU_DIM] * ub.astype(F32))

    g5 = _gelu(y_scr[...])
    gate = _sigmoid(_dot(g5.astype(BF16), gluw_ref[...]) + glub_ref[...])
    gl_ref[...] = (g5 * gate).astype(BF16)


def _col_min(x):
    return jnp.min(x, axis=0, keepdims=True)


def _col_max(x):
    return jnp.max(x, axis=0, keepdims=True)


def _col_sum(x):
    return jnp.sum(x, axis=0, keepdims=True)


def _route(sel, scores, tm):
    neg = -jnp.inf
    iota = lax.broadcasted_iota(I32, (GROUP_SIZE, tm), 0)
    sel_b = [sel[g * GROUP_SIZE:(g + 1) * GROUP_SIZE, :] for g in range(N_GROUPS)]
    sc_b = [scores[g * GROUP_SIZE:(g + 1) * GROUP_SIZE, :] for g in range(N_GROUPS)]

    iota_g = lax.broadcasted_iota(I32, (N_GROUPS, tm), 0)
    gs = jnp.zeros((N_GROUPS, tm), F32)
    for g in range(N_GROUPS):
        b = sel_b[g]
        m1 = _col_max(b)
        i1 = _col_min(jnp.where(b == m1, iota, GROUP_SIZE))
        m2 = _col_max(jnp.where(iota == i1, neg, b))
        gs = jnp.where(iota_g == g, m1 + m2, gs)

    keep = jnp.zeros((N_GROUPS, tm), I32)
    work = gs
    for _ in range(TOPK_GROUPS):
        m = _col_max(work)
        idx = _col_min(jnp.where(work == m, iota_g, N_GROUPS))
        hit = iota_g == idx
        keep = jnp.where(hit, 1, keep)
        work = jnp.where(hit, neg, work)

    cand = [jnp.where(keep[g:g + 1, :] > 0, sel_b[g], neg) for g in range(N_GROUPS)]
    ids, vals = [], []
    for _ in range(TOP_K):
        m = _col_max(cand[0])
        for g in range(1, N_GROUPS):
            m = jnp.maximum(m, _col_max(cand[g]))
        idx = _col_min(jnp.where(cand[0] == m, iota, N_EXPERTS))
        for g in range(1, N_GROUPS):
            idx = jnp.minimum(
                idx, _col_min(jnp.where(cand[g] == m, iota + g * GROUP_SIZE, N_EXPERTS)))
        val = jnp.zeros((1, tm), F32)
        for g in range(N_GROUPS):
            hit = (iota + g * GROUP_SIZE) == idx
            val = val + _col_sum(jnp.where(hit, sc_b[g], 0.0))
            cand[g] = jnp.where(hit, neg, cand[g])
        ids.append(idx)
        vals.append(val)
    return ids, vals


def _post_body(ua_ref, gl_ref, ga_ref, gb_ref, x_ref, ling_ref, linb_ref, pa_ref, pb_ref,
               wo_ref, l1g_ref, l1b_ref, x1_ref):
    branch_a = _dot(ua_ref[...], pa_ref[...])
    branch_b = _dot(gl_ref[...], pb_ref[...])
    merged = (_sigmoid(ga_ref[...].astype(F32)) * branch_a
              + _sigmoid(gb_ref[...].astype(F32)) * branch_b)
    o = _dot(merged.astype(BF16), wo_ref[...])
    xn = _layernorm(x_ref[...], ling_ref[...], linb_ref[...])
    x1_ref[...] = _layernorm(ALPHA * xn + o, l1g_ref[...], l1b_ref[...])


def _post_mixer(ua, gl, z16, xflat, ln_in_g, ln_in_b, proj_a, proj_b, w_o, ln1_g, ln1_b,
                n_tok, tile):
    c2 = lambda i: (0, 0)
    in_specs = [
        pl.BlockSpec((tile, D_RNN), lambda i: (i, 0)),
        pl.BlockSpec((tile, D_S5), lambda i: (i, 0)),
        pl.BlockSpec((tile, D_MODEL), lambda i: (i, 1)),
        pl.BlockSpec((tile, D_MODEL), lambda i: (i, 2)),
        pl.BlockSpec((tile, D_MODEL), lambda i: (i, 0)),
        pl.BlockSpec((1, D_MODEL), c2),
        pl.BlockSpec((1, D_MODEL), c2),
        pl.BlockSpec((D_RNN, D_MODEL), c2),
        pl.BlockSpec((D_S5, D_MODEL), c2),
        pl.BlockSpec((D_MODEL, D_MODEL), c2),
        pl.BlockSpec((1, D_MODEL), c2),
        pl.BlockSpec((1, D_MODEL), c2),
    ]
    return pl.pallas_call(
        _post_body,
        grid=(n_tok // tile,),
        in_specs=in_specs,
        out_specs=pl.BlockSpec((tile, D_MODEL), lambda i: (i, 0)),
        out_shape=jax.ShapeDtypeStruct((n_tok, D_MODEL), F32),
        compiler_params=_params(("arbitrary",)),
        name="post_mixer",
    )(ua, gl, z16, z16, xflat, ln_in_g, ln_in_b, proj_a, proj_b, w_o, ln1_g, ln1_b)


def _router_body(x1_ref, wrh_ref, wrl_ref, rb_ref,
                 eidx_ref, rank_ref, gatet_ref, cnt_ref, cnt_scr):
    i = pl.program_id(0)
    tm = x1_ref.shape[0]

    @pl.when(i == 0)
    def _():
        cnt_scr[...] = jnp.zeros_like(cnt_scr)

    x1 = x1_ref[...]

    x_hi = x1.astype(BF16)
    x_lo = (x1 - x_hi.astype(F32)).astype(BF16)
    nt = (((1,), (1,)), ((), ()))
    dg = lambda a, b: lax.dot_general(a, b, nt, preferred_element_type=F32)
    logits = dg(wrh_ref[...], x_hi) + dg(wrh_ref[...], x_lo) + dg(wrl_ref[...], x_hi)
    scores = _sigmoid(logits)
    sel = scores + rb_ref[...]
    ids, vals = _route(sel, scores, tm)

    total = vals[0]
    for v in vals[1:]:
        total = total + v
    iota_k = lax.broadcasted_iota(I32, (TOP_K, tm), 0)
    iota_e = lax.broadcasted_iota(I32, (N_EXPERTS, tm), 0)
    eidx = jnp.zeros((TOP_K, tm), I32)
    gates = jnp.zeros((TOP_K, tm), F32)
    selm = jnp.zeros((N_EXPERTS, tm), F32)
    for k in range(TOP_K):
        eidx = jnp.where(iota_k == k, ids[k], eidx)
        gates = jnp.where(iota_k == k, vals[k] / total * ROUTED_SCALE, gates)
        selm = jnp.where(iota_e == ids[k], 1.0, selm)
    eidx_ref[...] = eidx

    r_i = lax.broadcasted_iota(I32, (tm, tm), 0)
    c_i = lax.broadcasted_iota(I32, (tm, tm), 1)
    upper = jnp.where(r_i < c_i, 1.0, 0.0).astype(BF16)
    rank_all = _dot(selm.astype(BF16), upper) + cnt_scr[...]
    rank = jnp.zeros((TOP_K, tm), F32)
    for k in range(TOP_K):
        rk = _col_sum(jnp.where(iota_e == ids[k], rank_all, 0.0))
        rank = jnp.where(iota_k == k, rk, rank)
    rank_ref[...] = rank.astype(I32)
    cnt_scr[...] = cnt_scr[...] + jnp.sum(selm, axis=1, keepdims=True)
    cnt_ref[...] = cnt_scr[...]

    gpad = jnp.concatenate([gates, jnp.zeros((LANES - TOP_K, tm), F32)], axis=0)
    gatet_ref[...] = gpad.T


def _router(x1, wr_hi, wr_lo, rbias, tile):
    n_tok = x1.shape[0]
    c2 = lambda i: (0, 0)
    return pl.pallas_call(
        _router_body,
        grid=(n_tok // tile,),
        in_specs=[
            pl.BlockSpec((tile, D_MODEL), lambda i: (i, 0)),
            pl.BlockSpec((N_EXPERTS, D_MODEL), c2),
            pl.BlockSpec((N_EXPERTS, D_MODEL), c2),
            pl.BlockSpec((N_EXPERTS, 1), c2),
        ],
        out_specs=[
            pl.BlockSpec((TOP_K, tile), lambda i: (0, i)),
            pl.BlockSpec((TOP_K, tile), lambda i: (0, i)),
            pl.BlockSpec((tile, LANES), lambda i: (i, 0)),
            pl.BlockSpec((N_EXPERTS, 1), c2),
        ],
        out_shape=[
            jax.ShapeDtypeStruct((TOP_K, n_tok), I32),
            jax.ShapeDtypeStruct((TOP_K, n_tok), I32),
            jax.ShapeDtypeStruct((n_tok, LANES), F32),
            jax.ShapeDtypeStruct((N_EXPERTS, 1), F32),
        ],
        scratch_shapes=[pltpu.VMEM((N_EXPERTS, 1), F32)],
        compiler_params=_params(("arbitrary",)),
        name="router",
    )(x1, wr_hi, wr_lo, rbias)


def _positions_body(eidx_ref, rank_ref, cnt_ref, pos_ref, texp_ref, info_ref):
    cnt = cnt_ref[...]
    padded = jnp.floor((cnt + (MOE_TILE - 1)) * (1.0 / MOE_TILE)) * MOE_TILE
    r_i = lax.broadcasted_iota(I32, (N_EXPERTS, N_EXPERTS), 0)
    c_i = lax.broadcasted_iota(I32, (N_EXPERTS, N_EXPERTS), 1)
    eye = r_i == c_i
    as_row = lambda col: jnp.sum(jnp.where(eye, col, 0.0), axis=0, keepdims=True)
    padded_row = as_row(padded)
    base = jnp.sum(jnp.where(c_i < r_i, padded_row, 0.0), axis=1, keepdims=True)
    end = base + padded

    eidx = eidx_ref[...]
    pos = rank_ref[...]
    base_i = base.astype(I32)
    for e in range(N_EXPERTS):
        pos = pos + jnp.where(eidx == e, base_i[e:e + 1, :], 0)
    pos_ref[...] = pos

    ntp = texp_ref.shape[1]
    t_row = lax.broadcasted_iota(I32, (1, ntp), 1).astype(F32) * MOE_TILE
    texp = jnp.sum(jnp.where(end <= t_row, 1, 0), axis=0, keepdims=True)
    texp_ref[...] = jnp.minimum(texp, N_EXPERTS - 1).astype(I32)

    row = lax.broadcasted_iota(I32, (SUBLANES, LANES), 0)
    lane = lax.broadcasted_iota(I32, (SUBLANES, LANES), 1)
    pad_lanes = lambda r: jnp.concatenate(
        [r, jnp.zeros((1, LANES - N_EXPERTS), F32)], axis=1)
    ntiles = jnp.sum(padded, axis=0, keepdims=True) * (1.0 / MOE_TILE)
    used_row = as_row(cnt) > 0.0
    c_f = c_i.astype(F32)
    nxt = jnp.min(jnp.where((c_i > r_i) & used_row, c_f, float(N_EXPERTS)), axis=1, keepdims=True)
    order = jnp.sum(jnp.where((c_i < r_i) & used_row, 1.0, 0.0), axis=1, keepdims=True)
    slot = order - 2.0 * jnp.floor(order * 0.5)
    info = jnp.where(row == 0, pad_lanes(as_row(cnt)), 0.0)
    info = jnp.where(row == 1, pad_lanes(as_row(base)), info)
    info = jnp.where(row == 2, pad_lanes(as_row(end)), info)
    info = jnp.where((row == 3) & (lane == 0), ntiles, info)
    info = jnp.where(row == 4, pad_lanes(as_row(nxt)), info)
    info = jnp.where(row == 5, pad_lanes(as_row(slot)), info)
    info_ref[...] = info.astype(I32)


def _positions(eidx, rank, cnt, n_tiles_max):
    n_tok = eidx.shape[1]
    ntp = -(-n_tiles_max // LANES) * LANES
    return pl.pallas_call(
        _positions_body,
        out_shape=[
            jax.ShapeDtypeStruct((TOP_K, n_tok), I32),
            jax.ShapeDtypeStruct((1, ntp), I32),
            jax.ShapeDtypeStruct((SUBLANES, LANES), I32),
        ],
        compiler_params=pltpu.CompilerParams(vmem_limit_bytes=VMEM_LIMIT),
        name="positions",
    )(eidx, rank, cnt)


def _row_copy(src_hbm, src_row, dst_hbm, dst_row, sem):
    return pltpu.make_async_copy(
        src_hbm.at[pl.ds(src_row, 1)], dst_hbm.at[pl.ds(dst_row, 1)], sem)


def _swiglu_packed(x, w1_ref, w3_ref, w2_ref):
    xb = x.astype(BF16)
    h = (_silu(_dot(xb, w1_ref[...])) * _dot(xb, w3_ref[...])).astype(BF16)
    return _dot(h, w2_ref[...])


def _dispatch_body(info_ref, pos_ref, xp_ref, w1_ref, w3_ref, w2_ref,
                   xs_hbm, sh_ref, zero_scr, sem):
    i = pl.program_id(0)
    tm = xp_ref.shape[0]

    def issue(g, carry):
        row0 = pl.multiple_of(g * SUBLANES, SUBLANES)
        for j in range(SUBLANES):
            for k in range(TOP_K):
                _row_copy(xp_ref, row0 + j, xs_hbm, pos_ref[k, row0 + j], sem).start(
                    priority=k % 2)
        return carry

    lax.fori_loop(0, tm // SUBLANES, issue, 0)
    sh_ref[...] = _swiglu_packed(xp_ref[...], w1_ref, w3_ref, w2_ref)

    for k in range(TOP_K):
        pltpu.make_async_copy(xp_ref, xs_hbm.at[pl.ds(0, tm)], sem).wait()

    @pl.when(i == pl.num_programs(0) - 1)
    def _():
        zero_scr[...] = jnp.zeros_like(zero_scr)

        def per_expert(e, carry):
            start = info_ref[1, e] + info_ref[0, e]
            stop = info_ref[2, e]

            def fill(r, c):
                _row_copy(zero_scr, 0, xs_hbm, r, sem).start()
                return c

            def fill_wait(r, c):
                _row_copy(zero_scr, 0, xs_hbm, 0, sem).wait()
                return c

            lax.fori_loop(start, stop, fill, 0)
            lax.fori_loop(start, stop, fill_wait, 0)
            return carry

        lax.fori_loop(0, N_EXPERTS, per_expert, 0)

        ntiles = info_ref[3, 0]
        tiles_alloc = xs_hbm.shape[0] // MOE_TILE

        def tile_copy(j):
            return pltpu.make_async_copy(
                zero_scr, xs_hbm.at[pl.ds(pl.multiple_of(j * MOE_TILE, MOE_TILE), MOE_TILE)], sem)

        def fill_tile(j, c):
            tile_copy(j).start()
            return c

        def fill_tile_wait(j, c):
            tile_copy(j).wait()
            return c

        lax.fori_loop(ntiles, tiles_alloc, fill_tile, 0)
        lax.fori_loop(ntiles, tiles_alloc, fill_tile_wait, 0)


def _dispatch(info, pos, x1, sh_w1, sh_w3, sh_w2, rows_sorted, tile):
    n_tok, half = x1.shape
    c2 = lambda i, info: (0, 0)
    grid_spec = pltpu.PrefetchScalarGridSpec(
        num_scalar_prefetch=1,
        grid=(n_tok // tile,),
        in_specs=[
            pl.BlockSpec((TOP_K, tile), lambda i, info: (0, i), memory_space=pltpu.SMEM),
            pl.BlockSpec((tile, half), lambda i, info: (i, 0)),
            pl.BlockSpec((D_MODEL, D_EXPERT), c2),
            pl.BlockSpec((D_MODEL, D_EXPERT), c2),
            pl.BlockSpec((D_EXPERT, D_MODEL), c2),
        ],
        out_specs=[
            pl.BlockSpec(memory_space=pl.ANY),
            pl.BlockSpec((tile, D_MODEL), lambda i, info: (i, 0)),
        ],
        scratch_shapes=[pltpu.VMEM((MOE_TILE, half), F32), pltpu.SemaphoreType.DMA],
    )
    return pl.pallas_call(
        _dispatch_body,
        grid_spec=grid_spec,
        out_shape=[
            jax.ShapeDtypeStruct((rows_sorted, half), F32),
            jax.ShapeDtypeStruct((n_tok, D_MODEL), F32),
        ],
        compiler_params=_params(("arbitrary",)),
        name="dispatch",
    )(info, pos, x1, sh_w1, sh_w3, sh_w2)


def _moe_body(texp_ref, info_ref, xs_ref, w1_hbm, w3_hbm, w2_hbm, ys_ref,
              w1f_scr, w3f_scr, w2f_scr, w1b_scr, w3b_scr, w2b_scr, sems):
    t = pl.program_id(0)
    ntiles = info_ref[3, 0]
    tt = jnp.minimum(t, ntiles - 1)
    prev = jnp.maximum(tt - 1, 0)
    expert = texp_ref[0, tt]
    fresh = (t == 0) | (expert != texp_ref[0, prev])
    valid = t < ntiles

    def weight_copies(ex, slot):
        return [
            pltpu.make_async_copy(w1_hbm.at[ex], w1f_scr.at[slot], sems.at[0, slot]),
            pltpu.make_async_copy(w3_hbm.at[ex], w3f_scr.at[slot], sems.at[1, slot]),
            pltpu.make_async_copy(w2_hbm.at[ex], w2f_scr.at[slot], sems.at[2, slot]),
        ]

    @pl.when(t == 0)
    def _():
        for c in weight_copies(expert, info_ref[5, expert]):
            c.start()

    @pl.when(valid & fresh)
    def _():
        slot = info_ref[5, expert]
        for c in weight_copies(expert, slot):
            c.wait()
        w1b_scr[...] = w1f_scr[slot].astype(BF16)
        w3b_scr[...] = w3f_scr[slot].astype(BF16)
        w2b_scr[...] = w2f_scr[slot].astype(BF16)
        nxt = info_ref[4, expert]

        @pl.when(nxt < N_EXPERTS)
        def _():
            for c in weight_copies(nxt, 1 - slot):
                c.start()

    @pl.when(valid)
    def _():
        ys_ref[...] = _swiglu_packed(xs_ref[...], w1b_scr, w3b_scr, w2b_scr)

    @pl.when(jnp.logical_not(valid))
    def _():
        ys_ref[...] = jnp.zeros_like(ys_ref)


def _moe(texp, info, xs, ex_w1, ex_w3, ex_w2, n_tiles_max):
    half = D_MODEL

    def tile_idx(t, texp, info):
        return jnp.minimum(t, info[3, 0] - 1)

    grid_spec = pltpu.PrefetchScalarGridSpec(
        num_scalar_prefetch=2,
        grid=(n_tiles_max,),
        in_specs=[
            pl.BlockSpec((MOE_TILE, half), lambda t, texp, info: (tile_idx(t, texp, info), 0)),
            pl.BlockSpec(memory_space=pl.ANY),
            pl.BlockSpec(memory_space=pl.ANY),
            pl.BlockSpec(memory_space=pl.ANY),
        ],
        out_specs=pl.BlockSpec((MOE_TILE, half), lambda t, texp, info: (t, 0)),
        scratch_shapes=[
            pltpu.VMEM((2, D_MODEL, D_EXPERT), F32),
            pltpu.VMEM((2, D_MODEL, D_EXPERT), F32),
            pltpu.VMEM((2, D_EXPERT, D_MODEL), F32),
            pltpu.VMEM((D_MODEL, D_EXPERT), BF16),
            pltpu.VMEM((D_MODEL, D_EXPERT), BF16),
            pltpu.VMEM((D_EXPERT, D_MODEL), BF16),
            pltpu.SemaphoreType.DMA((3, 2)),
        ],
    )
    return pl.pallas_call(
        _moe_body,
        grid_spec=grid_spec,
        out_shape=jax.ShapeDtypeStruct(xs.shape, F32),
        compiler_params=_params(("arbitrary",)),
        name="moe_experts",
    )(texp, info, xs, ex_w1, ex_w3, ex_w2)


def _combine_body(n_prompt_tiles, pos_ref, ys_hbm, gate_ref, x1_ref, sh_ref, g_ref, b_ref,
                  yp_ref, ysm_ref, buf_scr, y_scr, sem):
    i = pl.program_id(0)
    tm = x1_ref.shape[0]

    def issue(g, carry):
        row0 = pl.multiple_of(g * SUBLANES, SUBLANES)
        for j in range(SUBLANES):
            for k in range(TOP_K):
                pltpu.make_async_copy(
                    ys_hbm.at[pl.ds(pos_ref[k, row0 + j], 1)],
                    buf_scr.at[k, pl.ds(row0 + j, 1)], sem,
                ).start(priority=k % 2)
        return carry

    lax.fori_loop(0, tm // SUBLANES, issue, 0)

    for k in range(TOP_K):
        pltpu.make_async_copy(ys_hbm.at[pl.ds(0, tm)], buf_scr.at[k], sem).wait()

    gates = gate_ref[...]
    acc = sh_ref[...]
    for k in range(TOP_K):
        acc = acc + gates[:, k:k + 1] * buf_scr[k]
    y = _layernorm(ALPHA * x1_ref[...] + acc, g_ref[...], b_ref[...])

    @pl.when(i < n_prompt_tiles)
    def _():
        steps = tm // NSEQ
        for c in range(D_MODEL // LANES):
            y_scr[c] = y[:, c * LANES:(c + 1) * LANES]
        for s in range(NSEQ):
            for c in range(D_MODEL // LANES):
                yp_ref[s, :, c * LANES:(c + 1) * LANES] = y_scr[c, pl.ds(s, steps, stride=NSEQ), :]

    @pl.when(i >= n_prompt_tiles)
    def _():
        ysm_ref[...] = y


def _combine(pos, ys, gate_t, x1, sh_out, ln2_g, ln2_b, tile, n_prompt):
    n_tok = x1.shape[0]
    n_s = n_tok - n_prompt
    assert n_prompt % tile == 0 and n_s % tile == 0
    npt = n_prompt // tile
    steps = tile // NSEQ
    c2 = lambda i: (0, 0)
    return pl.pallas_call(
        functools.partial(_combine_body, npt),
        grid=(n_tok // tile,),
        in_specs=[
            pl.BlockSpec((TOP_K, tile), lambda i: (0, i), memory_space=pltpu.SMEM),
            pl.BlockSpec(memory_space=pl.ANY),
            pl.BlockSpec((tile, LANES), lambda i: (i, 0)),
            pl.BlockSpec((tile, D_MODEL), lambda i: (i, 0)),
            pl.BlockSpec((tile, D_MODEL), lambda i: (i, 0)),
            pl.BlockSpec((1, D_MODEL), c2),
            pl.BlockSpec((1, D_MODEL), c2),
        ],
        out_specs=[
            pl.BlockSpec((NSEQ, steps, D_MODEL), lambda i: (0, jnp.minimum(i, npt - 1), 0)),
            pl.BlockSpec((tile, D_MODEL), lambda i: (jnp.maximum(i - npt, 0), 0)),
        ],
        out_shape=[
            jax.ShapeDtypeStruct((NSEQ, n_prompt // NSEQ, D_MODEL), F32),
            jax.ShapeDtypeStruct((n_s, D_MODEL), F32),
        ],
        scratch_shapes=[
            pltpu.VMEM((TOP_K, tile, D_MODEL), F32),
            pltpu.VMEM((D_MODEL // LANES, tile, LANES), F32),
            pltpu.SemaphoreType.DMA,
        ],
        compiler_params=_params(("arbitrary",)),
        name="combine",
    )(pos, ys, gate_t, x1, sh_out, ln2_g, ln2_b)


def _pick_tile(n, cap, mult):
    best = mult
    for t in range(mult, cap + 1, mult):
        if n % t == 0:
            best = t
    assert n % best == 0
    return best


def kernel(x_prompt, x_sample, state_rglru_h, state_conv, state_s5_re, state_s5_im, meta_tokens, ln_in_g, ln_in_b, w_in, b_in, conv_w, conv_b, rg_wa, rg_ba, rg_wi, rg_bi, rg_lambda, s5_a_re, s5_a_im, s5_b_re, s5_b_im, s5_c_re, s5_c_im, s5_d, s5_log_dt, glu_w, glu_b, proj_a, proj_b, w_o, ln1_g, ln1_b, router_w, router_bias, ex_w1, ex_w3, ex_w2, sh_w1, sh_w3, sh_w2, ln2_g, ln2_b):
    bp, seq, d = x_prompt.shape
    n_s = x_sample.shape[0]
    assert bp == NSEQ and d == D_MODEL and x_sample.shape[1] == 1
    assert w_in.shape[0] == DEPTH
    n_prompt = bp * seq
    n_tok = n_prompt + n_s
    meta_rows = NSEQ * N_META
    n1 = n_tok + 2 * meta_rows
    row = lambda v: v.reshape(1, -1)

    xflat = jnp.concatenate([
        jnp.transpose(x_prompt, (1, 0, 2)).reshape(n_prompt, d),
        x_sample.reshape(n_s, d),
        jnp.repeat(meta_tokens, NSEQ, axis=0),
        jnp.zeros((meta_rows, d), F32),
    ], axis=0)

    tile1 = _pick_tile(n1, 1056, 2 * SUBLANES)
    xa, z16 = _inproj(xflat, row(ln_in_g), row(ln_in_b), w_in[0].astype(BF16), b_in, tile1)

    a_r, a_i, bb_r, bb_i = _s5_prep(s5_a_re[0], s5_a_im[0], s5_log_dt[0], s5_b_re[0], s5_b_im[0])
    bg = S5_BLOCK_GROUPS
    bd_b = jnp.concatenate([
        _block_diag(bb_r.reshape(S5_BLOCKS, bg, S5_CH, S5_N)),
        _block_diag(bb_i.reshape(S5_BLOCKS, bg, S5_CH, S5_N)),
    ], axis=2).astype(BF16)
    c_t = lambda c: jnp.transpose(c[0], (0, 2, 1)).reshape(S5_BLOCKS, bg, S5_N, S5_CH)
    bd_cre = _block_diag(c_t(s5_c_re)).astype(BF16)
    bd_cim = _block_diag(c_t(s5_c_im)).astype(BF16)
    heads_per_blk = RG_HEADS // (D_RNN // MXU_DIM)
    rg_blk = lambda w: _block_diag(
        w[0].reshape(D_RNN // MXU_DIM, heads_per_blk, D_RNN // RG_HEADS, D_RNN // RG_HEADS)
    ).astype(BF16)
    weights = (conv_w[0], conv_b, rg_blk(rg_wa), rg_blk(rg_wi), rg_ba, rg_bi, rg_lambda,
               bd_b, bd_cre, bd_cim, row(s5_d[0]), glu_w[0].astype(BF16), glu_b)

    chunk = n_s + 2 * meta_rows
    assert n_prompt % chunk == 0
    ua, gl, p_h, p_conv, p_s5r, p_s5i, s_h, s_conv, s_s5r, s_s5i = _mixer(
        xa, z16, a_r, a_i, state_rglru_h[0],
        state_conv[0].reshape(n_s, (CONV_W - 1) * D_RNN),
        state_s5_re[0].reshape(n_s, S5_STATE), state_s5_im[0].reshape(n_s, S5_STATE),
        weights, n_prompt, n_s, n_tok, meta_rows, chunk)

    wr_t = jnp.transpose(router_w[0])
    wr_hi = wr_t.astype(BF16)
    wr_lo = (wr_t - wr_hi.astype(F32)).astype(BF16)
    tile3 = _pick_tile(n_tok, POST_TILE, 2 * SUBLANES)
    x1 = _post_mixer(
        ua, gl, z16, xflat, row(ln_in_g), row(ln_in_b), proj_a[0].astype(BF16),
        proj_b[0].astype(BF16), w_o[0].astype(BF16), ln1_g, ln1_b, n_tok, tile3)
    tile4 = _pick_tile(n_tok, TOKEN_TILE, LANES)
    eidx, rank, gate_t, cnt = _router(
        x1, wr_hi, wr_lo, router_bias[0].reshape(N_EXPERTS, 1), tile4)

    rows_max = n_tok * TOP_K + N_EXPERTS * (MOE_TILE - 1)
    n_tiles_max = -(-rows_max // MOE_TILE)
    pos, texp, info = _positions(eidx, rank, cnt, n_tiles_max)
    xs, sh_out = _dispatch(info, pos, x1, sh_w1[0].astype(BF16), sh_w3[0].astype(BF16),
                           sh_w2[0].astype(BF16), n_tiles_max * MOE_TILE, tile4)
    ys = _moe(texp, info, xs, ex_w1[0], ex_w3[0], ex_w2[0], n_tiles_max)
    y_prompt, y_sample = _combine(pos, ys, gate_t, x1, sh_out, ln2_g, ln2_b,
                                  _pick_tile(n_s, FIN_TILE, LANES), n_prompt)

    dt = x_prompt.dtype
    y_sample = y_sample.reshape(n_s, 1, d)
    conv_p = jnp.transpose(p_conv.reshape(CONV_W - 1, bp, D_RNN), (1, 0, 2))
    s5_shape = (S5_GROUPS, S5_N)
    return (y_prompt.astype(dt), y_sample.astype(dt),
            p_h[None], conv_p[None],
            p_s5r.reshape(1, bp, *s5_shape), p_s5i.reshape(1, bp, *s5_shape),
            s_h[None], s_conv.reshape(1, n_s, CONV_W - 1, D_RNN),
            s_s5r.reshape(1, n_s, *s5_shape), s_s5i.reshape(1, n_s, *s5_shape))
```

```python
import functools
import math

import jax
import jax.numpy as jnp
from jax import lax
from jax.experimental import pallas as pl
from jax.experimental.pallas import tpu as pltpu

F32 = jnp.float32
BF16 = jnp.bfloat16
I32 = jnp.int32

D_MODEL = 2048
D_RNN = D_MODEL // 2
D_S5 = D_MODEL // 2
N_IN = 2 * D_RNN + D_S5 + 2 * D_MODEL
RG_HEADS = 8
CONV_W = 4
LRU_C = 8.0
S5_CH = 16
S5_GROUPS = D_S5 // S5_CH
S5_N = 64
S5_STATE = S5_GROUPS * S5_N
N_EXPERTS = 64
TOP_K = 8
N_GROUPS = 8
GROUP_SIZE = N_EXPERTS // N_GROUPS
TOPK_GROUPS = 4
D_EXPERT = 512
ROUTED_SCALE = 2.5
LN_EPS = 1e-5
N_META = 16
DEPTH = 1
ALPHA = (2.0 * DEPTH) ** 0.25

SUBLANES = 8
LANES = 128
MXU_DIM = 256
VMEM_LIMIT = 56 * 1024 * 1024

NSEQ = 4
S5_BLOCK_GROUPS = MXU_DIM // S5_CH
S5_BLOCKS = S5_GROUPS // S5_BLOCK_GROUPS
S5_BLOCK_STATE = S5_BLOCK_GROUPS * S5_N
IN_TILE_N = 1024
POST_TILE = 320
TOKEN_TILE = 640
MOE_TILE = 256
FIN_TILE = 128


def _params(sem, vmem=VMEM_LIMIT):
    return pltpu.CompilerParams(dimension_semantics=sem, vmem_limit_bytes=vmem)


def _dot(a, b):
    return jnp.dot(a, b, preferred_element_type=F32)


def _layernorm(x, g, b):
    mu = jnp.mean(x, axis=-1, keepdims=True)
    xc = x - mu
    var = jnp.mean(xc * xc, axis=-1, keepdims=True)
    return xc * lax.rsqrt(var + LN_EPS) * g + b


def _sigmoid(x):
    return 1.0 / (1.0 + jnp.exp(-x))


def _gelu(x):
    c = math.sqrt(2.0 / math.pi)
    return 0.5 * x * (1.0 + jnp.tanh(c * (x + 0.044715 * (x * x * x))))


def _silu(x):
    return x * _sigmoid(x)


def _softplus(x):
    return jnp.maximum(x, 0.0) + jnp.log1p(jnp.exp(-jnp.abs(x)))


def _neg_expm1(x):
    poly = x * (1.0 + x * (1.0 / 2) * (1.0 + x * (1.0 / 3) * (1.0 + x * (1.0 / 4) * (
        1.0 + x * (1.0 / 5) * (1.0 + x * (1.0 / 6) * (1.0 + x * (1.0 / 7)))))))
    return -jnp.where(x > -0.25, poly, jnp.exp(x) - 1.0)


def _s5_prep_body(are_ref, aim_ref, ldt_ref, bre_ref, bim_ref,
                  abr_ref, abi_ref, bbr_ref, bbi_ref):
    a_re = are_ref[...]
    a_im = aim_ref[...]
    dt = jnp.exp(ldt_ref[...])
    mag = jnp.exp(a_re * dt)
    ab_r = mag * jnp.cos(a_im * dt)
    ab_i = mag * jnp.sin(a_im * dt)
    den = a_re * a_re + a_im * a_im
    nr = ab_r - 1.0
    cr = (nr * a_re + ab_i * a_im) / den
    ci = (ab_i * a_re - nr * a_im) / den
    b_re = bre_ref[...]
    b_im = bim_ref[...]
    abr_ref[...] = ab_r
    abi_ref[...] = ab_i
    bbr_ref[...] = cr * b_re - ci * b_im
    bbi_ref[...] = cr * b_im + ci * b_re


def _s5_prep(a_re, a_im, log_dt, b_re, b_im):
    g, n, c = b_re.shape
    wide = c * n
    bc = lambda v: jnp.broadcast_to(v[:, None, :], (g, c, n)).reshape(g, wide)
    are_x = bc(a_re)
    aim_x = bc(a_im)
    ldt_x = jnp.broadcast_to(log_dt[:, None], (g, wide))
    bre_x = jnp.transpose(b_re, (0, 2, 1)).reshape(g, wide)
    bim_x = jnp.transpose(b_im, (0, 2, 1)).reshape(g, wide)
    shp = jax.ShapeDtypeStruct((g, wide), F32)
    abr, abi, bbr, bbi = pl.pallas_call(
        _s5_prep_body, out_shape=(shp, shp, shp, shp), name="s5_prep",
    )(are_x, aim_x, ldt_x, bre_x, bim_x)
    a_r = abr[:, :n].reshape(1, g * n)
    a_i = abi[:, :n].reshape(1, g * n)
    return a_r, a_i, bbr.reshape(g, c, n), bbi.reshape(g, c, n)


def _block_diag(x):
    k, g, a, b = x.shape
    eye = jnp.eye(g, dtype=x.dtype)
    return jnp.einsum("kgab,gh->kgahb", x, eye).reshape(k, g * a, g * b)


def _inproj_body(x_ref, g_ref, b_ref, w_ref, bias_ref, xa_ref, z_ref, xn_scr):
    j = pl.program_id(1)

    @pl.when(j == 0)
    def _():
        xn_scr[...] = _layernorm(x_ref[...], g_ref[...], b_ref[...]).astype(BF16)

    z = _dot(xn_scr[...], w_ref[...]) + bias_ref[...]
    n_xa = D_RNN // IN_TILE_N

    @pl.when(j < n_xa)
    def _():
        xa_ref[...] = z

    @pl.when(j >= n_xa)
    def _():
        z_ref[...] = z.astype(BF16)


def _inproj(xflat, ln_g, ln_b, w_in_bf, b_in, tile):
    n1 = xflat.shape[0]
    n_xa = D_RNN // IN_TILE_N
    grid = (n1 // tile, N_IN // IN_TILE_N)
    return pl.pallas_call(
        _inproj_body,
        grid=grid,
        in_specs=[
            pl.BlockSpec((tile, D_MODEL), lambda i, j: (i, 0)),
            pl.BlockSpec((1, D_MODEL), lambda i, j: (0, 0)),
            pl.BlockSpec((1, D_MODEL), lambda i, j: (0, 0)),
            pl.BlockSpec((D_MODEL, IN_TILE_N), lambda i, j: (0, j)),
            pl.BlockSpec((1, IN_TILE_N), lambda i, j: (0, j)),
        ],
        out_specs=[
            pl.BlockSpec((tile, IN_TILE_N), lambda i, j: (i, jnp.minimum(j, n_xa - 1))),
            pl.BlockSpec((tile, IN_TILE_N), lambda i, j: (i, jnp.maximum(j - n_xa, 0))),
        ],
        out_shape=[
            jax.ShapeDtypeStruct((n1, D_RNN), F32),
            jax.ShapeDtypeStruct((n1, N_IN - D_RNN), BF16),
        ],
        scratch_shapes=[pltpu.VMEM((tile, D_MODEL), BF16)],
        compiler_params=_params(("arbitrary", "arbitrary")),
        name="in_proj",
    )(xflat, ln_g, ln_b, w_in_bf, b_in)


def _rg_gates(xc, wa_ref, wi_ref, ba, bi, sp):
    xcb = xc.astype(BF16)
    nblk = D_RNN // MXU_DIM
    r_pre = jnp.concatenate(
        [_dot(xcb[:, k * MXU_DIM:(k + 1) * MXU_DIM], wa_ref[k]) for k in range(nblk)], axis=1)
    i_pre = jnp.concatenate(
        [_dot(xcb[:, k * MXU_DIM:(k + 1) * MXU_DIM], wi_ref[k]) for k in range(nblk)], axis=1)
    r = _sigmoid(r_pre + ba)
    i = _sigmoid(i_pre + bi)
    log_a = (-LRU_C * r) * sp
    a = jnp.exp(log_a)
    u = jnp.sqrt(_neg_expm1(2.0 * log_a)) * (i * xc)
    return a, u


def _odd_rows(width):
    return lax.broadcasted_iota(I32, (SUBLANES, width), 0) >= NSEQ


def _mixer_chunk(rows, xa_ref, ya_ref, us_ref, ua_ref, gl_ref, w, s):
    (cw_ref, cb_ref, wa_ref, wi_ref, ba_ref, bi_ref, lam_ref, bdb_ref, cre_ref, cim_ref,
     d_ref, gluw_ref, glub_ref) = w
    (ext_scr, a_scr, u_scr, hs_scr, bu_scr, hst_scr, y_scr, tail_scr, hcar_scr, s5car_scr,
     cst_scr) = s
    emit = ua_ref is not None
    halo = NSEQ * CONV_W
    ngroups = rows // SUBLANES

    ext_scr[pl.ds(0, halo), :] = tail_scr[...]
    ext_scr[pl.ds(halo, rows), :] = xa_ref[...]
    xc = cb_ref[...] + cw_ref[pl.ds(CONV_W - 1, 1), :] * ext_scr[pl.ds(halo, rows), :]
    for j in range(1, CONV_W):
        xc = xc + cw_ref[pl.ds(CONV_W - 1 - j, 1), :] * ext_scr[pl.ds(halo - NSEQ * j, rows), :]
    tail_scr[...] = ext_scr[pl.ds(rows, halo), :]

    sp = _softplus(-lam_ref[...])
    a, u = _rg_gates(xc, wa_ref, wi_ref, ba_ref[...], bi_ref[...], sp)
    a_scr[pl.ds(0, rows), :] = a
    u_scr[pl.ds(0, rows), :] = u
    odd = _odd_rows(D_RNN)

    def rg_body(g, c):
        row = pl.multiple_of(g * SUBLANES, SUBLANES)
        a_v = a_scr[pl.ds(row, SUBLANES), :]
        u_v = u_scr[pl.ds(row, SUBLANES), :]
        hl = u_v + jnp.where(odd, a_v * pltpu.roll(u_v, NSEQ, 0), 0.0)
        p = jnp.where(odd, a_v * pltpu.roll(a_v, NSEQ, 0), a_v)
        hs_scr[pl.ds(row, SUBLANES), :] = hl + p * c
        q = jnp.where(odd, hl, pltpu.roll(hl, NSEQ, 0))
        pp = jnp.where(odd, p, pltpu.roll(p, NSEQ, 0))
        return q + pp * c

    hcar_scr[...] = lax.fori_loop(0, ngroups, rg_body, hcar_scr[...])
    if emit:
        ua_ref[...] = (hs_scr[pl.ds(0, rows), :] * _gelu(ya_ref[...].astype(F32))).astype(BF16)

    odd_s = _odd_rows(S5_BLOCK_STATE)
    for kb in range(S5_BLOCKS):
        lo = kb * S5_BLOCK_STATE
        ub = us_ref[:, kb * MXU_DIM:(kb + 1) * MXU_DIM]
        bu_scr[pl.ds(0, rows), :] = _dot(ub, bdb_ref[kb])
        aor, aoi, pr, pi, a2r, a2i = [cst_scr[i, :, lo:lo + S5_BLOCK_STATE] for i in range(6)]

        def s5_body(g, c, aor=aor, aoi=aoi, pr=pr, pi=pi, a2r=a2r, a2i=a2i):
            cr, ci = c
            row = pl.multiple_of(g * SUBLANES, SUBLANES)
            bur = bu_scr[pl.ds(row, SUBLANES), 0:S5_BLOCK_STATE]
            bui = bu_scr[pl.ds(row, SUBLANES), S5_BLOCK_STATE:2 * S5_BLOCK_STATE]
            sr = pltpu.roll(bur, NSEQ, 0)
            si = pltpu.roll(bui, NSEQ, 0)
            hlr = bur + aor * sr - aoi * si
            hli = bui + aor * si + aoi * sr
            hr = hlr + pr * cr - pi * ci
            hi = hli + pr * ci + pi * cr
            if emit:
                hst_scr[pl.ds(row, SUBLANES), 0:S5_BLOCK_STATE] = hr
                hst_scr[pl.ds(row, SUBLANES), S5_BLOCK_STATE:2 * S5_BLOCK_STATE] = hi
            return (jnp.where(odd_s, hr, pltpu.roll(hr, NSEQ, 0)),
                    jnp.where(odd_s, hi, pltpu.roll(hi, NSEQ, 0)))

        cr, ci = lax.fori_loop(
            0, ngroups, s5_body,
            (s5car_scr[0, :, lo:lo + S5_BLOCK_STATE], s5car_scr[1, :, lo:lo + S5_BLOCK_STATE]))
        s5car_scr[0, :, lo:lo + S5_BLOCK_STATE] = cr
        s5car_scr[1, :, lo:lo + S5_BLOCK_STATE] = ci
        if emit:
            hre = hst_scr[pl.ds(0, rows), 0:S5_BLOCK_STATE].astype(BF16)
            him = hst_scr[pl.ds(0, rows), S5_BLOCK_STATE:2 * S5_BLOCK_STATE].astype(BF16)
            y = _dot(hre, cre_ref[kb]) - _dot(him, cim_ref[kb])
            y = y + d_ref[:, kb * MXU_DIM:(kb + 1) * MXU_DIM] * ub.astype(F32)
            y_scr[pl.ds(0, rows), kb * MXU_DIM:(kb + 1) * MXU_DIM] = y

    if emit:
        g5 = _gelu(y_scr[pl.ds(0, rows), :])
        gate = _sigmoid(_dot(g5.astype(BF16), gluw_ref[...]) + glub_ref[...])
        gl_ref[...] = (g5 * gate).astype(BF16)


def _mixer_body(meta_rows, n_s, xa_ref, ya_ref, us_ref, xam_ref, usm_ref, ar_ref, ai_ref,
                h0_ref, cbuf_ref, s5r0_ref, s5i0_ref, *rest):
    w = rest[:13]
    (ua_ref, gl_ref, hout_ref, convout_ref, s5r_ref, s5i_ref,
     h1_ref, cnew_ref, s5r1_ref, s5i1_ref) = rest[13:23]
    s = rest[23:]
    y_scr, tail_scr, hcar_scr, s5car_scr, cst_scr = s[6], s[7], s[8], s[9], s[10]
    c = pl.program_id(0)
    rows = xa_ref.shape[0]
    nchunks = pl.num_programs(0) - 1

    @pl.when(c == nchunks)
    def _():
        head = lambda r: r.at[pl.ds(0, n_s)]
        _sample_step(head(xa_ref), head(ya_ref), head(us_ref), h0_ref, cbuf_ref, s5r0_ref,
                     s5i0_ref, ar_ref, ai_ref, w, head(ua_ref), head(gl_ref), h1_ref, cnew_ref,
                     s5r1_ref, s5i1_ref, head(y_scr))
        ua_ref[pl.ds(n_s, rows - n_s), :] = jnp.zeros((rows - n_s, D_RNN), BF16)
        gl_ref[pl.ds(n_s, rows - n_s), :] = jnp.zeros((rows - n_s, D_S5), BF16)

    @pl.when(c < nchunks)
    def _():
        _prompt_step(meta_rows, nchunks, xa_ref, ya_ref, us_ref, xam_ref, usm_ref, ar_ref, ai_ref,
                     w, ua_ref, gl_ref, hout_ref, convout_ref, s5r_ref, s5i_ref, s)


def _prompt_step(meta_rows, nchunks, xa_ref, ya_ref, us_ref, xam_ref, usm_ref, ar_ref, ai_ref,
                 w, ua_ref, gl_ref, hout_ref, convout_ref, s5r_ref, s5i_ref, s):
    tail_scr, hcar_scr, s5car_scr, cst_scr = s[7], s[8], s[9], s[10]
    c = pl.program_id(0)
    rows = xa_ref.shape[0]

    @pl.when(c == 0)
    def _():
        odd = _odd_rows(S5_STATE)
        ar = jnp.broadcast_to(ar_ref[...], (SUBLANES, S5_STATE))
        ai = jnp.broadcast_to(ai_ref[...], (SUBLANES, S5_STATE))
        a2r = ar * ar - ai * ai
        a2i = 2.0 * (ar * ai)
        cst_scr[0] = jnp.where(odd, ar, 0.0)
        cst_scr[1] = jnp.where(odd, ai, 0.0)
        cst_scr[2] = jnp.where(odd, a2r, ar)
        cst_scr[3] = jnp.where(odd, a2i, ai)
        cst_scr[4] = a2r
        cst_scr[5] = a2i
        tail_scr[...] = jnp.zeros_like(tail_scr)
        hcar_scr[...] = jnp.zeros_like(hcar_scr)
        s5car_scr[...] = jnp.zeros_like(s5car_scr)
        _mixer_chunk(meta_rows, xam_ref, None, usm_ref, None, None, w, s)

    _mixer_chunk(rows, xa_ref, ya_ref, us_ref, ua_ref, gl_ref, w, s)

    @pl.when(c == nchunks - 1)
    def _():
        hout_ref[...] = hcar_scr[pl.ds(NSEQ, NSEQ), :]
        convout_ref[...] = tail_scr[pl.ds(NSEQ, NSEQ * (CONV_W - 1)), :]
        s5r_ref[...] = s5car_scr[0, pl.ds(NSEQ, NSEQ), :]
        s5i_ref[...] = s5car_scr[1, pl.ds(NSEQ, NSEQ), :]


def _mixer_weight_specs(nidx):
    z = (0,) * nidx if nidx else ()
    c2 = lambda *_: (0, 0)
    c3 = lambda *_: (0, 0, 0)
    nblk = D_RNN // MXU_DIM
    return [
        pl.BlockSpec((CONV_W, D_RNN), c2),
        pl.BlockSpec((1, D_RNN), c2),
        pl.BlockSpec((nblk, MXU_DIM, MXU_DIM), c3),
        pl.BlockSpec((nblk, MXU_DIM, MXU_DIM), c3),
        pl.BlockSpec((1, D_RNN), c2),
        pl.BlockSpec((1, D_RNN), c2),
        pl.BlockSpec((1, D_RNN), c2),
        pl.BlockSpec((S5_BLOCKS, MXU_DIM, 2 * S5_BLOCK_STATE), c3),
        pl.BlockSpec((S5_BLOCKS, S5_BLOCK_STATE, MXU_DIM), c3),
        pl.BlockSpec((S5_BLOCKS, S5_BLOCK_STATE, MXU_DIM), c3),
        pl.BlockSpec((1, D_S5), c2),
        pl.BlockSpec((D_S5, D_S5), c2),
        pl.BlockSpec((1, D_S5), c2),
    ]


def _mixer(xa, z16, a_r, a_i, h0, cbuf, s5r0, s5i0, weights, n_prompt, n_s, meta_row0, meta_rows,
           chunk):
    nchunks = n_prompt // chunk
    n1 = xa.shape[0]
    assert n1 == n_prompt + chunk and n_s <= chunk
    meta_blk = meta_row0 // meta_rows
    halo = NSEQ * CONV_W
    c2 = lambda c: (0, 0)
    in_specs = [
        pl.BlockSpec((chunk, D_RNN), lambda c: (c, 0)),
        pl.BlockSpec((chunk, D_RNN), lambda c: (c, 0)),
        pl.BlockSpec((chunk, D_S5), lambda c: (c, 1)),
        pl.BlockSpec((meta_rows, D_RNN), lambda c: (meta_blk, 0)),
        pl.BlockSpec((meta_rows, D_S5), lambda c: (meta_blk, 1)),
        pl.BlockSpec((1, S5_STATE), c2),
        pl.BlockSpec((1, S5_STATE), c2),
        pl.BlockSpec((n_s, D_RNN), c2),
        pl.BlockSpec((n_s, (CONV_W - 1) * D_RNN), c2),
        pl.BlockSpec((n_s, S5_STATE), c2),
        pl.BlockSpec((n_s, S5_STATE), c2),
    ] + _mixer_weight_specs(1)
    out_specs = [
        pl.BlockSpec((chunk, D_RNN), lambda c: (c, 0)),
        pl.BlockSpec((chunk, D_S5), lambda c: (c, 0)),
        pl.BlockSpec((NSEQ, D_RNN), c2),
        pl.BlockSpec((NSEQ * (CONV_W - 1), D_RNN), c2),
        pl.BlockSpec((NSEQ, S5_STATE), c2),
        pl.BlockSpec((NSEQ, S5_STATE), c2),
        pl.BlockSpec((n_s, D_RNN), c2),
        pl.BlockSpec((n_s, (CONV_W - 1) * D_RNN), c2),
        pl.BlockSpec((n_s, S5_STATE), c2),
        pl.BlockSpec((n_s, S5_STATE), c2),
    ]
    out_shape = [
        jax.ShapeDtypeStruct((n1, D_RNN), BF16),
        jax.ShapeDtypeStruct((n1, D_S5), BF16),
        jax.ShapeDtypeStruct((NSEQ, D_RNN), F32),
        jax.ShapeDtypeStruct((NSEQ * (CONV_W - 1), D_RNN), F32),
        jax.ShapeDtypeStruct((NSEQ, S5_STATE), F32),
        jax.ShapeDtypeStruct((NSEQ, S5_STATE), F32),
        jax.ShapeDtypeStruct((n_s, D_RNN), F32),
        jax.ShapeDtypeStruct((n_s, (CONV_W - 1) * D_RNN), F32),
        jax.ShapeDtypeStruct((n_s, S5_STATE), F32),
        jax.ShapeDtypeStruct((n_s, S5_STATE), F32),
    ]
    scratch = [
        pltpu.VMEM((chunk + halo, D_RNN), F32),
        pltpu.VMEM((chunk, D_RNN), F32),
        pltpu.VMEM((chunk, D_RNN), F32),
        pltpu.VMEM((chunk, D_RNN), F32),
        pltpu.VMEM((chunk, 2 * S5_BLOCK_STATE), F32),
        pltpu.VMEM((chunk, 2 * S5_BLOCK_STATE), F32),
        pltpu.VMEM((chunk, D_S5), F32),
        pltpu.VMEM((halo, D_RNN), F32),
        pltpu.VMEM((SUBLANES, D_RNN), F32),
        pltpu.VMEM((2, SUBLANES, S5_STATE), F32),
        pltpu.VMEM((6, SUBLANES, S5_STATE), F32),
    ]
    return pl.pallas_call(
        functools.partial(_mixer_body, meta_rows, n_s),
        grid=(nchunks + 1,),
        in_specs=in_specs,
        out_specs=out_specs,
        out_shape=out_shape,
        scratch_shapes=scratch,
        compiler_params=_params(("arbitrary",)),
        name="mixer",
    )(xa, z16, z16, xa, z16, a_r, a_i, h0, cbuf, s5r0, s5i0, *weights)


def _sample_step(xa_ref, ya_ref, us_ref, h0_ref, cbuf_ref, s5r0_ref, s5i0_ref, ar_ref, ai_ref,
                 w, ua_ref, gl_ref, h1_ref, cnew_ref, s5r1_ref, s5i1_ref, y_scr):
    (cw_ref, cb_ref, wa_ref, wi_ref, ba_ref, bi_ref, lam_ref, bdb_ref, cre_ref, cim_ref,
     d_ref, gluw_ref, glub_ref) = w
    xa = xa_ref[...]
    xc = cb_ref[...] + cw_ref[pl.ds(CONV_W - 1, 1), :] * xa
    for k in range(CONV_W - 1):
        xc = xc + cw_ref[pl.ds(k, 1), :] * cbuf_ref[:, k * D_RNN:(k + 1) * D_RNN]
    for k in range(CONV_W - 2):
        cnew_ref[:, k * D_RNN:(k + 1) * D_RNN] = cbuf_ref[:, (k + 1) * D_RNN:(k + 2) * D_RNN]
    cnew_ref[:, (CONV_W - 2) * D_RNN:(CONV_W - 1) * D_RNN] = xa

    sp = _softplus(-lam_ref[...])
    a, u = _rg_gates(xc, wa_ref, wi_ref, ba_ref[...], bi_ref[...], sp)
    h1 = a * h0_ref[...] + u
    h1_ref[...] = h1
    ua_ref[...] = (h1 * _gelu(ya_ref[...].astype(F32))).astype(BF16)

    for kb in range(S5_BLOCKS):
        lo = kb * S5_BLOCK_STATE
        ub = us_ref[:, kb * MXU_DIM:(kb + 1) * MXU_DIM]
        bu = _dot(ub, bdb_ref[kb])
        ar = ar_ref[:, lo:lo + S5_BLOCK_STATE]
        ai = ai_ref[:, lo:lo + S5_BLOCK_STATE]
        h0r = s5r0_ref[:, lo:lo + S5_BLOCK_STATE]
        h0i = s5i0_ref[:, lo:lo + S5_BLOCK_STATE]
        hr = bu[:, 0:S5_BLOCK_STATE] + ar * h0r - ai * h0i
        hi = bu[:, S5_BLOCK_STATE:2 * S5_BLOCK_STATE] + ar * h0i + ai * h0r
        s5r1_ref[:, lo:lo + S5_BLOCK_STATE] = hr
        s5i1_ref[:, lo:lo + S5_BLOCK_STATE] = hi
        y = _dot(hr.astype(BF16), cre_ref[kb]) - _dot(hi.astype(BF16), cim_ref[kb])
        y_scr[:, kb * MXU_DIM:(kb + 1) * MXU_DIM] = (
            y + d_ref[:, kb * MXU_DIM:(kb + 1) * MXU_DIM] * ub.astype(F32))

    g5 = _gelu(y_scr[...])
    gate = _sigmoid(_dot(g5.astype(BF16), gluw_ref[...]) + glub_ref[...])
    gl_ref[...] = (g5 * gate).astype(BF16)


def _col_min(x):
    return jnp.min(x, axis=0, keepdims=True)


def _col_max(x):
    return jnp.max(x, axis=0, keepdims=True)


def _col_sum(x):
    return jnp.sum(x, axis=0, keepdims=True)


def _route(sel, scores, tm):
    neg = -jnp.inf
    iota = lax.broadcasted_iota(I32, (GROUP_SIZE, tm), 0)
    sel_b = [sel[g * GROUP_SIZE:(g + 1) * GROUP_SIZE, :] for g in range(N_GROUPS)]
    sc_b = [scores[g * GROUP_SIZE:(g + 1) * GROUP_SIZE, :] for g in range(N_GROUPS)]

    iota_g = lax.broadcasted_iota(I32, (N_GROUPS, tm), 0)
    gs = jnp.zeros((N_GROUPS, tm), F32)
    for g in range(N_GROUPS):
        b = sel_b[g]
        m1 = _col_max(b)
        i1 = _col_min(jnp.where(b == m1, iota, GROUP_SIZE))
        m2 = _col_max(jnp.where(iota == i1, neg, b))
        gs = jnp.where(iota_g == g, m1 + m2, gs)

    keep = jnp.zeros((N_GROUPS, tm), I32)
    work = gs
    for _ in range(TOPK_GROUPS):
        m = _col_max(work)
        idx = _col_min(jnp.where(work == m, iota_g, N_GROUPS))
        hit = iota_g == idx
        keep = jnp.where(hit, 1, keep)
        work = jnp.where(hit, neg, work)

    cand = [jnp.where(keep[g:g + 1, :] > 0, sel_b[g], neg) for g in range(N_GROUPS)]
    ids, vals = [], []
    for _ in range(TOP_K):
        m = _col_max(cand[0])
        for g in range(1, N_GROUPS):
            m = jnp.maximum(m, _col_max(cand[g]))
        idx = _col_min(jnp.where(cand[0] == m, iota, N_EXPERTS))
        for g in range(1, N_GROUPS):
            idx = jnp.minimum(
                idx, _col_min(jnp.where(cand[g] == m, iota + g * GROUP_SIZE, N_EXPERTS)))
        val = jnp.zeros((1, tm), F32)
        for g in range(N_GROUPS):
            hit = (iota + g * GROUP_SIZE) == idx
            val = val + _col_sum(jnp.where(hit, sc_b[g], 0.0))
            cand[g] = jnp.where(hit, neg, cand[g])
        ids.append(idx)
        vals.append(val)
    return ids, vals


def _post_body(ua_ref, gl_ref, ga_ref, gb_ref, x_ref, ling_ref, linb_ref, pa_ref, pb_ref,
               wo_ref, l1g_ref, l1b_ref, x1_ref):
    branch_a = _dot(ua_ref[...], pa_ref[...])
    branch_b = _dot(gl_ref[...], pb_ref[...])
    merged = (_sigmoid(ga_ref[...].astype(F32)) * branch_a
              + _sigmoid(gb_ref[...].astype(F32)) * branch_b)
    o = _dot(merged.astype(BF16), wo_ref[...])
    xn = _layernorm(x_ref[...], ling_ref[...], linb_ref[...])
    x1_ref[...] = _layernorm(ALPHA * xn + o, l1g_ref[...], l1b_ref[...])


def _post_mixer(ua, gl, z16, xflat, ln_in_g, ln_in_b, proj_a, proj_b, w_o, ln1_g, ln1_b,
                n_tok, tile):
    c2 = lambda i: (0, 0)
    in_specs = [
        pl.BlockSpec((tile, D_RNN), lambda i: (i, 0)),
        pl.BlockSpec((tile, D_S5), lambda i: (i, 0)),
        pl.BlockSpec((tile, D_MODEL), lambda i: (i, 1)),
        pl.BlockSpec((tile, D_MODEL), lambda i: (i, 2)),
        pl.BlockSpec((tile, D_MODEL), lambda i: (i, 0)),
        pl.BlockSpec((1, D_MODEL), c2),
        pl.BlockSpec((1, D_MODEL), c2),
        pl.BlockSpec((D_RNN, D_MODEL), c2),
        pl.BlockSpec((D_S5, D_MODEL), c2),
        pl.BlockSpec((D_MODEL, D_MODEL), c2),
        pl.BlockSpec((1, D_MODEL), c2),
        pl.BlockSpec((1, D_MODEL), c2),
    ]
    return pl.pallas_call(
        _post_body,
        grid=(n_tok // tile,),
        in_specs=in_specs,
        out_specs=pl.BlockSpec((tile, D_MODEL), lambda i: (i, 0)),
        out_shape=jax.ShapeDtypeStruct((n_tok, D_MODEL), F32),
        compiler_params=_params(("arbitrary",)),
        name="post_mixer",
    )(ua, gl, z16, z16, xflat, ln_in_g, ln_in_b, proj_a, proj_b, w_o, ln1_g, ln1_b)


def _router_body(x1_ref, wrh_ref, wrl_ref, rb_ref,
                 eidx_ref, rank_ref, gatet_ref, cnt_ref, cnt_scr):
    i = pl.program_id(0)
    tm = x1_ref.shape[0]

    @pl.when(i == 0)
    def _():
        cnt_scr[...] = jnp.zeros_like(cnt_scr)

    x1 = x1_ref[...]

    x_hi = x1.astype(BF16)
    x_lo = (x1 - x_hi.astype(F32)).astype(BF16)
    nt = (((1,), (1,)), ((), ()))
    dg = lambda a, b: lax.dot_general(a, b, nt, preferred_element_type=F32)
    logits = dg(wrh_ref[...], x_hi) + dg(wrh_ref[...], x_lo) + dg(wrl_ref[...], x_hi)
    scores = _sigmoid(logits)
    sel = scores + rb_ref[...]
    ids, vals = _route(sel, scores, tm)

    total = vals[0]
    for v in vals[1:]:
        total = total + v
    iota_k = lax.broadcasted_iota(I32, (TOP_K, tm), 0)
    iota_e = lax.broadcasted_iota(I32, (N_EXPERTS, tm), 0)
    eidx = jnp.zeros((TOP_K, tm), I32)
    gates = jnp.zeros((TOP_K, tm), F32)
    selm = jnp.zeros((N_EXPERTS, tm), F32)
    for k in range(TOP_K):
        eidx = jnp.where(iota_k == k, ids[k], eidx)
        gates = jnp.where(iota_k == k, vals[k] / total * ROUTED_SCALE, gates)
        selm = jnp.where(iota_e == ids[k], 1.0, selm)
    eidx_ref[...] = eidx

    r_i = lax.broadcasted_iota(I32, (tm, tm), 0)
    c_i = lax.broadcasted_iota(I32, (tm, tm), 1)
    upper = jnp.where(r_i < c_i, 1.0, 0.0).astype(BF16)
    rank_all = _dot(selm.astype(BF16), upper) + cnt_scr[...]
    rank = jnp.zeros((TOP_K, tm), F32)
    for k in range(TOP_K):
        rk = _col_sum(jnp.where(iota_e == ids[k], rank_all, 0.0))
        rank = jnp.where(iota_k == k, rk, rank)
    rank_ref[...] = rank.astype(I32)
    cnt_scr[...] = cnt_scr[...] + jnp.sum(selm, axis=1, keepdims=True)
    cnt_ref[...] = cnt_scr[...]

    gpad = jnp.concatenate([gates, jnp.zeros((LANES - TOP_K, tm), F32)], axis=0)
    gatet_ref[...] = gpad.T


def _router(x1, wr_hi, wr_lo, rbias, tile):
    n_tok = x1.shape[0]
    c2 = lambda i: (0, 0)
    return pl.pallas_call(
        _router_body,
        grid=(n_tok // tile,),
        in_specs=[
            pl.BlockSpec((tile, D_MODEL), lambda i: (i, 0)),
            pl.BlockSpec((N_EXPERTS, D_MODEL), c2),
            pl.BlockSpec((N_EXPERTS, D_MODEL), c2),
            pl.BlockSpec((N_EXPERTS, 1), c2),
        ],
        out_specs=[
            pl.BlockSpec((TOP_K, tile), lambda i: (0, i)),
            pl.BlockSpec((TOP_K, tile), lambda i: (0, i)),
            pl.BlockSpec((tile, LANES), lambda i: (i, 0)),
            pl.BlockSpec((N_EXPERTS, 1), c2),
        ],
        out_shape=[
            jax.ShapeDtypeStruct((TOP_K, n_tok), I32),
            jax.ShapeDtypeStruct((TOP_K, n_tok), I32),
            jax.ShapeDtypeStruct((n_tok, LANES), F32),
            jax.ShapeDtypeStruct((N_EXPERTS, 1), F32),
        ],
        scratch_shapes=[pltpu.VMEM((N_EXPERTS, 1), F32)],
        compiler_params=_params(("arbitrary",)),
        name="router",
    )(x1, wr_hi, wr_lo, rbias)


def _positions_body(eidx_ref, rank_ref, cnt_ref, pos_ref, texp_ref, info_ref):
    cnt = cnt_ref[...]
    padded = jnp.floor((cnt + (MOE_TILE - 1)) * (1.0 / MOE_TILE)) * MOE_TILE
    r_i = lax.broadcasted_iota(I32, (N_EXPERTS, N_EXPERTS), 0)
    c_i = lax.broadcasted_iota(I32, (N_EXPERTS, N_EXPERTS), 1)
    eye = r_i == c_i
    as_row = lambda col: jnp.sum(jnp.where(eye, col, 0.0), axis=0, keepdims=True)
    padded_row = as_row(padded)
    base = jnp.sum(jnp.where(c_i < r_i, padded_row, 0.0), axis=1, keepdims=True)
    end = base + padded

    eidx = eidx_ref[...]
    pos = rank_ref[...]
    base_i = base.astype(I32)
    for e in range(N_EXPERTS):
        pos = pos + jnp.where(eidx == e, base_i[e:e + 1, :], 0)
    pos_ref[...] = pos

    ntp = texp_ref.shape[1]
    t_row = lax.broadcasted_iota(I32, (1, ntp), 1).astype(F32) * MOE_TILE
    texp = jnp.sum(jnp.where(end <= t_row, 1, 0), axis=0, keepdims=True)
    texp_ref[...] = jnp.minimum(texp, N_EXPERTS - 1).astype(I32)

    row = lax.broadcasted_iota(I32, (SUBLANES, LANES), 0)
    lane = lax.broadcasted_iota(I32, (SUBLANES, LANES), 1)
    pad_lanes = lambda r: jnp.concatenate(
        [r, jnp.zeros((1, LANES - N_EXPERTS), F32)], axis=1)
    ntiles = jnp.sum(padded, axis=0, keepdims=True) * (1.0 / MOE_TILE)
    used_row = as_row(cnt) > 0.0
    c_f = c_i.astype(F32)
    nxt = jnp.min(jnp.where((c_i > r_i) & used_row, c_f, float(N_EXPERTS)), axis=1, keepdims=True)
    order = jnp.sum(jnp.where((c_i < r_i) & used_row, 1.0, 0.0), axis=1, keepdims=True)
    slot = order - 2.0 * jnp.floor(order * 0.5)
    info = jnp.where(row == 0, pad_lanes(as_row(cnt)), 0.0)
    info = jnp.where(row == 1, pad_lanes(as_row(base)), info)
    info = jnp.where(row == 2, pad_lanes(as_row(end)), info)
    info = jnp.where((row == 3) & (lane == 0), ntiles, info)
    info = jnp.where(row == 4, pad_lanes(as_row(nxt)), info)
    info = jnp.where(row == 5, pad_lanes(as_row(slot)), info)
    info_ref[...] = info.astype(I32)


def _positions(eidx, rank, cnt, n_tiles_max):
    n_tok = eidx.shape[1]
    ntp = -(-n_tiles_max // LANES) * LANES
    return pl.pallas_call(
        _positions_body,
        out_shape=[
            jax.ShapeDtypeStruct((TOP_K, n_tok), I32),
            jax.ShapeDtypeStruct((1, ntp), I32),
            jax.ShapeDtypeStruct((SUBLANES, LANES), I32),
        ],
        compiler_params=pltpu.CompilerParams(vmem_limit_bytes=VMEM_LIMIT),
        name="positions",
    )(eidx, rank, cnt)


def _row_copy(src_hbm, src_row, dst_hbm, dst_row, sem):
    return pltpu.make_async_copy(
        src_hbm.at[pl.ds(src_row, 1)], dst_hbm.at[pl.ds(dst_row, 1)], sem)


def _swiglu_packed(x, w1_ref, w3_ref, w2_ref):
    xb = x.astype(BF16)
    h = (_silu(_dot(xb, w1_ref[...])) * _dot(xb, w3_ref[...])).astype(BF16)
    return _dot(h, w2_ref[...])


def _dispatch_body(info_ref, pos_ref, xp_ref, w1_ref, w3_ref, w2_ref,
                   xs_hbm, sh_ref, zero_scr, sem):
    i = pl.program_id(0)
    tm = xp_ref.shape[0]

    def issue(g, carry):
        row0 = pl.multiple_of(g * SUBLANES, SUBLANES)
        for j in range(SUBLANES):
            for k in range(TOP_K):
                _row_copy(xp_ref, row0 + j, xs_hbm, pos_ref[k, row0 + j], sem).start(
                    priority=k % 2)
        return carry

    lax.fori_loop(0, tm // SUBLANES, issue, 0)
    sh_ref[...] = _swiglu_packed(xp_ref[...], w1_ref, w3_ref, w2_ref)

    for k in range(TOP_K):
        pltpu.make_async_copy(xp_ref, xs_hbm.at[pl.ds(0, tm)], sem).wait()

    @pl.when(i == pl.num_programs(0) - 1)
    def _():
        zero_scr[...] = jnp.zeros_like(zero_scr)

        def per_expert(e, carry):
            start = info_ref[1, e] + info_ref[0, e]
            stop = info_ref[2, e]

            def fill(r, c):
                _row_copy(zero_scr, 0, xs_hbm, r, sem).start()
                return c

            def fill_wait(r, c):
                _row_copy(zero_scr, 0, xs_hbm, 0, sem).wait()
                return c

            lax.fori_loop(start, stop, fill, 0)
            lax.fori_loop(start, stop, fill_wait, 0)
            return carry

        lax.fori_loop(0, N_EXPERTS, per_expert, 0)

        ntiles = info_ref[3, 0]
        tiles_alloc = xs_hbm.shape[0] // MOE_TILE

        def tile_copy(j):
            return pltpu.make_async_copy(
                zero_scr, xs_hbm.at[pl.ds(pl.multiple_of(j * MOE_TILE, MOE_TILE), MOE_TILE)], sem)

        def fill_tile(j, c):
            tile_copy(j).start()
            return c

        def fill_tile_wait(j, c):
            tile_copy(j).wait()
            return c

        lax.fori_loop(ntiles, tiles_alloc, fill_tile, 0)
        lax.fori_loop(ntiles, tiles_alloc, fill_tile_wait, 0)


def _dispatch(info, pos, x1, sh_w1, sh_w3, sh_w2, rows_sorted, tile):
    n_tok, half = x1.shape
    c2 = lambda i, info: (0, 0)
    grid_spec = pltpu.PrefetchScalarGridSpec(
        num_scalar_prefetch=1,
        grid=(n_tok // tile,),
        in_specs=[
            pl.BlockSpec((TOP_K, tile), lambda i, info: (0, i), memory_space=pltpu.SMEM),
            pl.BlockSpec((tile, half), lambda i, info: (i, 0)),
            pl.BlockSpec((D_MODEL, D_EXPERT), c2),
            pl.BlockSpec((D_MODEL, D_EXPERT), c2),
            pl.BlockSpec((D_EXPERT, D_MODEL), c2),
        ],
        out_specs=[
            pl.BlockSpec(memory_space=pl.ANY),
            pl.BlockSpec((tile, D_MODEL), lambda i, info: (i, 0)),
        ],
        scratch_shapes=[pltpu.VMEM((MOE_TILE, half), F32), pltpu.SemaphoreType.DMA],
    )
    return pl.pallas_call(
        _dispatch_body,
        grid_spec=grid_spec,
        out_shape=[
            jax.ShapeDtypeStruct((rows_sorted, half), F32),
            jax.ShapeDtypeStruct((n_tok, D_MODEL), F32),
        ],
        compiler_params=_params(("arbitrary",)),
        name="dispatch",
    )(info, pos, x1, sh_w1, sh_w3, sh_w2)


def _moe_body(texp_ref, info_ref, xs_ref, w1_hbm, w3_hbm, w2_hbm, ys_ref,
              w1f_scr, w3f_scr, w2f_scr, sems):
    t = pl.program_id(0)
    ntiles = info_ref[3, 0]
    tt = jnp.minimum(t, ntiles - 1)
    prev = jnp.maximum(tt - 1, 0)
    expert = texp_ref[0, tt]
    fresh = (t == 0) | (expert != texp_ref[0, prev])
    valid = t < ntiles

    def weight_copies(ex, slot):
        return [
            pltpu.make_async_copy(w1_hbm.at[ex], w1f_scr.at[slot], sems.at[0, slot]),
            pltpu.make_async_copy(w3_hbm.at[ex], w3f_scr.at[slot], sems.at[1, slot]),
            pltpu.make_async_copy(w2_hbm.at[ex], w2f_scr.at[slot], sems.at[2, slot]),
        ]

    @pl.when(t == 0)
    def _():
        for c in weight_copies(expert, info_ref[5, expert]):
            c.start()

    @pl.when(valid & fresh)
    def _():
        slot = info_ref[5, expert]
        for c in weight_copies(expert, slot):
            c.wait()
        nxt = info_ref[4, expert]

        @pl.when(nxt < N_EXPERTS)
        def _():
            for c in weight_copies(nxt, 1 - slot):
                c.start()

    @pl.when(valid)
    def _():
        slot = info_ref[5, expert]
        x = xs_ref[...]
        h = _silu(_dot(x, w1f_scr[slot])) * _dot(x, w3f_scr[slot])
        ys_ref[...] = _dot(h, w2f_scr[slot])

    @pl.when(jnp.logical_not(valid))
    def _():
        ys_ref[...] = jnp.zeros_like(ys_ref)


def _moe(texp, info, xs, ex_w1, ex_w3, ex_w2, n_tiles_max):
    half = D_MODEL

    def tile_idx(t, texp, info):
        return jnp.minimum(t, info[3, 0] - 1)

    grid_spec = pltpu.PrefetchScalarGridSpec(
        num_scalar_prefetch=2,
        grid=(n_tiles_max,),
        in_specs=[
            pl.BlockSpec((MOE_TILE, half), lambda t, texp, info: (tile_idx(t, texp, info), 0)),
            pl.BlockSpec(memory_space=pl.ANY),
            pl.BlockSpec(memory_space=pl.ANY),
            pl.BlockSpec(memory_space=pl.ANY),
        ],
        out_specs=pl.BlockSpec((MOE_TILE, half), lambda t, texp, info: (t, 0)),
        scratch_shapes=[
            pltpu.VMEM((2, D_MODEL, D_EXPERT), F32),
            pltpu.VMEM((2, D_MODEL, D_EXPERT), F32),
            pltpu.VMEM((2, D_EXPERT, D_MODEL), F32),
            pltpu.SemaphoreType.DMA((3, 2)),
        ],
    )
    return pl.pallas_call(
        _moe_body,
        grid_spec=grid_spec,
        out_shape=jax.ShapeDtypeStruct(xs.shape, F32),
        compiler_params=_params(("arbitrary",)),
        name="moe_experts",
    )(texp, info, xs, ex_w1, ex_w3, ex_w2)


def _combine_body(n_prompt_tiles, pos_ref, ys_hbm, gate_ref, x1_ref, sh_ref, g_ref, b_ref,
                  yp_ref, ysm_ref, buf_scr, y_scr, sem):
    i = pl.program_id(0)
    tm = x1_ref.shape[0]

    def issue(g, carry):
        row0 = pl.multiple_of(g * SUBLANES, SUBLANES)
        for j in range(SUBLANES):
            for k in range(TOP_K):
                pltpu.make_async_copy(
                    ys_hbm.at[pl.ds(pos_ref[k, row0 + j], 1)],
                    buf_scr.at[k, pl.ds(row0 + j, 1)], sem,
                ).start(priority=k % 2)
        return carry

    lax.fori_loop(0, tm // SUBLANES, issue, 0)

    for k in range(TOP_K):
        pltpu.make_async_copy(ys_hbm.at[pl.ds(0, tm)], buf_scr.at[k], sem).wait()

    gates = gate_ref[...]
    acc = sh_ref[...]
    for k in range(TOP_K):
        acc = acc + gates[:, k:k + 1] * buf_scr[k]
    y = _layernorm(ALPHA * x1_ref[...] + acc, g_ref[...], b_ref[...])

    @pl.when(i < n_prompt_tiles)
    def _():
        steps = tm // NSEQ
        for c in range(D_MODEL // LANES):
            y_scr[c] = y[:, c * LANES:(c + 1) * LANES]
        for s in range(NSEQ):
            for c in range(D_MODEL // LANES):
                yp_ref[s, :, c * LANES:(c + 1) * LANES] = y_scr[c, pl.ds(s, steps, stride=NSEQ), :]

    @pl.when(i >= n_prompt_tiles)
    def _():
        ysm_ref[...] = y


def _combine(pos, ys, gate_t, x1, sh_out, ln2_g, ln2_b, tile, n_prompt):
    n_tok = x1.shape[0]
    n_s = n_tok - n_prompt
    assert n_prompt % tile == 0 and n_s % tile == 0
    npt = n_prompt // tile
    steps = tile // NSEQ
    c2 = lambda i: (0, 0)
    return pl.pallas_call(
        functools.partial(_combine_body, npt),
        grid=(n_tok // tile,),
        in_specs=[
            pl.BlockSpec((TOP_K, tile), lambda i: (0, i), memory_space=pltpu.SMEM),
            pl.BlockSpec(memory_space=pl.ANY),
            pl.BlockSpec((tile, LANES), lambda i: (i, 0)),
            pl.BlockSpec((tile, D_MODEL), lambda i: (i, 0)),
            pl.BlockSpec((tile, D_MODEL), lambda i: (i, 0)),
            pl.BlockSpec((1, D_MODEL), c2),
            pl.BlockSpec((1, D_MODEL), c2),
        ],
        out_specs=[
            pl.BlockSpec((NSEQ, steps, D_MODEL), lambda i: (0, jnp.minimum(i, npt - 1), 0)),
            pl.BlockSpec((tile, D_MODEL), lambda i: (jnp.maximum(i - npt, 0), 0)),
        ],
        out_shape=[
            jax.ShapeDtypeStruct((NSEQ, n_prompt // NSEQ, D_MODEL), F32),
            jax.ShapeDtypeStruct((n_s, D_MODEL), F32),
        ],
        scratch_shapes=[
            pltpu.VMEM((TOP_K, tile, D_MODEL), F32),
            pltpu.VMEM((D_MODEL // LANES, tile, LANES), F32),
            pltpu.SemaphoreType.DMA,
        ],
        compiler_params=_params(("arbitrary",)),
        name="combine",
    )(pos, ys, gate_t, x1, sh_out, ln2_g, ln2_b)


def _pick_tile(n, cap, mult):
    best = mult
    for t in range(mult, cap + 1, mult):
        if n % t == 0:
            best = t
    assert n % best == 0
    return best


def kernel(x_prompt, x_sample, state_rglru_h, state_conv, state_s5_re, state_s5_im, meta_tokens, ln_in_g, ln_in_b, w_in, b_in, conv_w, conv_b, rg_wa, rg_ba, rg_wi, rg_bi, rg_lambda, s5_a_re, s5_a_im, s5_b_re, s5_b_im, s5_c_re, s5_c_im, s5_d, s5_log_dt, glu_w, glu_b, proj_a, proj_b, w_o, ln1_g, ln1_b, router_w, router_bias, ex_w1, ex_w3, ex_w2, sh_w1, sh_w3, sh_w2, ln2_g, ln2_b):
    bp, seq, d = x_prompt.shape
    n_s = x_sample.shape[0]
    assert bp == NSEQ and d == D_MODEL and x_sample.shape[1] == 1
    assert w_in.shape[0] == DEPTH
    n_prompt = bp * seq
    n_tok = n_prompt + n_s
    meta_rows = NSEQ * N_META
    n1 = n_tok + 2 * meta_rows
    row = lambda v: v.reshape(1, -1)

    xflat = jnp.concatenate([
        jnp.transpose(x_prompt, (1, 0, 2)).reshape(n_prompt, d),
        x_sample.reshape(n_s, d),
        jnp.repeat(meta_tokens, NSEQ, axis=0),
        jnp.zeros((meta_rows, d), F32),
    ], axis=0)

    tile1 = _pick_tile(n1, 1056, 2 * SUBLANES)
    xa, z16 = _inproj(xflat, row(ln_in_g), row(ln_in_b), w_in[0].astype(BF16), b_in, tile1)

    a_r, a_i, bb_r, bb_i = _s5_prep(s5_a_re[0], s5_a_im[0], s5_log_dt[0], s5_b_re[0], s5_b_im[0])
    bg = S5_BLOCK_GROUPS
    bd_b = jnp.concatenate([
        _block_diag(bb_r.reshape(S5_BLOCKS, bg, S5_CH, S5_N)),
        _block_diag(bb_i.reshape(S5_BLOCKS, bg, S5_CH, S5_N)),
    ], axis=2).astype(BF16)
    c_t = lambda c: jnp.transpose(c[0], (0, 2, 1)).reshape(S5_BLOCKS, bg, S5_N, S5_CH)
    bd_cre = _block_diag(c_t(s5_c_re)).astype(BF16)
    bd_cim = _block_diag(c_t(s5_c_im)).astype(BF16)
    heads_per_blk = RG_HEADS // (D_RNN // MXU_DIM)
    rg_blk = lambda w: _block_diag(
        w[0].reshape(D_RNN // MXU_DIM, heads_per_blk, D_RNN // RG_HEADS, D_RNN // RG_HEADS)
    ).astype(BF16)
    weights = (conv_w[0], conv_b, rg_blk(rg_wa), rg_blk(rg_wi), rg_ba, rg_bi, rg_lambda,
               bd_b, bd_cre, bd_cim, row(s5_d[0]), glu_w[0].astype(BF16), glu_b)

    chunk = n_s + 2 * meta_rows
    assert n_prompt % chunk == 0
    ua, gl, p_h, p_conv, p_s5r, p_s5i, s_h, s_conv, s_s5r, s_s5i = _mixer(
        xa, z16, a_r, a_i, state_rglru_h[0],
        state_conv[0].reshape(n_s, (CONV_W - 1) * D_RNN),
        state_s5_re[0].reshape(n_s, S5_STATE), state_s5_im[0].reshape(n_s, S5_STATE),
        weights, n_prompt, n_s, n_tok, meta_rows, chunk)

    wr_t = jnp.transpose(router_w[0])
    wr_hi = wr_t.astype(BF16)
    wr_lo = (wr_t - wr_hi.astype(F32)).astype(BF16)
    tile3 = _pick_tile(n_tok, POST_TILE, 2 * SUBLANES)
    x1 = _post_mixer(
        ua, gl, z16, xflat, row(ln_in_g), row(ln_in_b), proj_a[0].astype(BF16),
        proj_b[0].astype(BF16), w_o[0].astype(BF16), ln1_g, ln1_b, n_tok, tile3)
    tile4 = _pick_tile(n_tok, TOKEN_TILE, LANES)
    eidx, rank, gate_t, cnt = _router(
        x1, wr_hi, wr_lo, router_bias[0].reshape(N_EXPERTS, 1), tile4)

    rows_max = n_tok * TOP_K + N_EXPERTS * (MOE_TILE - 1)
    n_tiles_max = -(-rows_max // MOE_TILE)
    pos, texp, info = _positions(eidx, rank, cnt, n_tiles_max)
    xs, sh_out = _dispatch(info, pos, x1, sh_w1[0].astype(BF16), sh_w3[0].astype(BF16),
                           sh_w2[0].astype(BF16), n_tiles_max * MOE_TILE, tile4)
    ys = _moe(texp, info, xs, ex_w1[0], ex_w3[0], ex_w2[0], n_tiles_max)
    y_prompt, y_sample = _combine(pos, ys, gate_t, x1, sh_out, ln2_g, ln2_b,
                                  _pick_tile(n_s, FIN_TILE, LANES), n_prompt)

    dt = x_prompt.dtype
    y_sample = y_sample.reshape(n_s, 1, d)
    conv_p = jnp.transpose(p_conv.reshape(CONV_W - 1, bp, D_RNN), (1, 0, 2))
    s5_shape = (S5_GROUPS, S5_N)
    return (y_prompt.astype(dt), y_sample.astype(dt),
            p_h[None], conv_p[None],
            p_s5r.reshape(1, bp, *s5_shape), p_s5i.reshape(1, bp, *s5_shape),
            s_h[None], s_conv.reshape(1, n_s, CONV_W - 1, D_RNN),
            s_s5r.reshape(1, n_s, *s5_shape), s_s5i.reshape(1, n_s, *s5_shape))
```

```python
import functools
import math

import jax
import jax.numpy as jnp
from jax import lax
from jax.experimental import pallas as pl
from jax.experimental.pallas import tpu as pltpu

F32 = jnp.float32
BF16 = jnp.bfloat16
I32 = jnp.int32

D_MODEL = 2048
D_RNN = D_MODEL // 2
D_S5 = D_MODEL // 2
N_IN = 2 * D_RNN + D_S5 + 2 * D_MODEL
RG_HEADS = 8
CONV_W = 4
LRU_C = 8.0
S5_CH = 16
S5_GROUPS = D_S5 // S5_CH
S5_N = 64
S5_STATE = S5_GROUPS * S5_N
N_EXPERTS = 64
TOP_K = 8
N_GROUPS = 8
GROUP_SIZE = N_EXPERTS // N_GROUPS
TOPK_GROUPS = 4
D_EXPERT = 512
ROUTED_SCALE = 2.5
LN_EPS = 1e-5
N_META = 16
DEPTH = 1
ALPHA = (2.0 * DEPTH) ** 0.25

SUBLANES = 8
LANES = 128
MXU_DIM = 256
VMEM_LIMIT = 56 * 1024 * 1024

NSEQ = 4
S5_BLOCK_GROUPS = MXU_DIM // S5_CH
S5_BLOCKS = S5_GROUPS // S5_BLOCK_GROUPS
S5_BLOCK_STATE = S5_BLOCK_GROUPS * S5_N
IN_TILE_N = 1024
POST_TILE = 320
TOKEN_TILE = 640
MOE_TILE = 256
FIN_TILE = 128


def _params(sem, vmem=VMEM_LIMIT):
    return pltpu.CompilerParams(dimension_semantics=sem, vmem_limit_bytes=vmem)


def _dot(a, b):
    return jnp.dot(a, b, preferred_element_type=F32)


def _layernorm(x, g, b):
    mu = jnp.mean(x, axis=-1, keepdims=True)
    xc = x - mu
    var = jnp.mean(xc * xc, axis=-1, keepdims=True)
    return xc * lax.rsqrt(var + LN_EPS) * g + b


def _sigmoid(x):
    return 1.0 / (1.0 + jnp.exp(-x))


def _gelu(x):
    c = math.sqrt(2.0 / math.pi)
    return 0.5 * x * (1.0 + jnp.tanh(c * (x + 0.044715 * (x * x * x))))


def _silu(x):
    return x * _sigmoid(x)


def _softplus(x):
    return jnp.maximum(x, 0.0) + jnp.log1p(jnp.exp(-jnp.abs(x)))


def _neg_expm1(x):
    poly = x * (1.0 + x * (1.0 / 2) * (1.0 + x * (1.0 / 3) * (1.0 + x * (1.0 / 4) * (
        1.0 + x * (1.0 / 5) * (1.0 + x * (1.0 / 6) * (1.0 + x * (1.0 / 7)))))))
    return -jnp.where(x > -0.25, poly, jnp.exp(x) - 1.0)


def _s5_prep_body(are_ref, aim_ref, ldt_ref, bre_ref, bim_ref,
                  abr_ref, abi_ref, bbr_ref, bbi_ref):
    a_re = are_ref[...]
    a_im = aim_ref[...]
    dt = jnp.exp(ldt_ref[...])
    mag = jnp.exp(a_re * dt)
    ab_r = mag * jnp.cos(a_im * dt)
    ab_i = mag * jnp.sin(a_im * dt)
    den = a_re * a_re + a_im * a_im
    nr = ab_r - 1.0
    cr = (nr * a_re + ab_i * a_im) / den
    ci = (ab_i * a_re - nr * a_im) / den
    b_re = bre_ref[...]
    b_im = bim_ref[...]
    abr_ref[...] = ab_r
    abi_ref[...] = ab_i
    bbr_ref[...] = cr * b_re - ci * b_im
    bbi_ref[...] = cr * b_im + ci * b_re


def _s5_prep(a_re, a_im, log_dt, b_re, b_im):
    g, n, c = b_re.shape
    wide = c * n
    bc = lambda v: jnp.broadcast_to(v[:, None, :], (g, c, n)).reshape(g, wide)
    are_x = bc(a_re)
    aim_x = bc(a_im)
    ldt_x = jnp.broadcast_to(log_dt[:, None], (g, wide))
    bre_x = jnp.transpose(b_re, (0, 2, 1)).reshape(g, wide)
    bim_x = jnp.transpose(b_im, (0, 2, 1)).reshape(g, wide)
    shp = jax.ShapeDtypeStruct((g, wide), F32)
    abr, abi, bbr, bbi = pl.pallas_call(
        _s5_prep_body, out_shape=(shp, shp, shp, shp), name="s5_prep",
    )(are_x, aim_x, ldt_x, bre_x, bim_x)
    a_r = abr[:, :n].reshape(1, g * n)
    a_i = abi[:, :n].reshape(1, g * n)
    return a_r, a_i, bbr.reshape(g, c, n), bbi.reshape(g, c, n)


def _block_diag(x):
    k, g, a, b = x.shape
    eye = jnp.eye(g, dtype=x.dtype)
    return jnp.einsum("kgab,gh->kgahb", x, eye).reshape(k, g * a, g * b)


def _inproj_body(x_ref, g_ref, b_ref, w_ref, bias_ref, xa_ref, z_ref, xn_scr):
    j = pl.program_id(1)

    @pl.when(j == 0)
    def _():
        xn_scr[...] = _layernorm(x_ref[...], g_ref[...], b_ref[...]).astype(BF16)

    z = _dot(xn_scr[...], w_ref[...]) + bias_ref[...]
    n_xa = D_RNN // IN_TILE_N

    @pl.when(j < n_xa)
    def _():
        xa_ref[...] = z

    @pl.when(j >= n_xa)
    def _():
        z_ref[...] = z.astype(BF16)


def _inproj(xflat, ln_g, ln_b, w_in_bf, b_in, tile):
    n1 = xflat.shape[0]
    n_xa = D_RNN // IN_TILE_N
    grid = (n1 // tile, N_IN // IN_TILE_N)
    return pl.pallas_call(
        _inproj_body,
        grid=grid,
        in_specs=[
            pl.BlockSpec((tile, D_MODEL), lambda i, j: (i, 0)),
            pl.BlockSpec((1, D_MODEL), lambda i, j: (0, 0)),
            pl.BlockSpec((1, D_MODEL), lambda i, j: (0, 0)),
            pl.BlockSpec((D_MODEL, IN_TILE_N), lambda i, j: (0, j)),
            pl.BlockSpec((1, IN_TILE_N), lambda i, j: (0, j)),
        ],
        out_specs=[
            pl.BlockSpec((tile, IN_TILE_N), lambda i, j: (i, jnp.minimum(j, n_xa - 1))),
            pl.BlockSpec((tile, IN_TILE_N), lambda i, j: (i, jnp.maximum(j - n_xa, 0))),
        ],
        out_shape=[
            jax.ShapeDtypeStruct((n1, D_RNN), F32),
            jax.ShapeDtypeStruct((n1, N_IN - D_RNN), BF16),
        ],
        scratch_shapes=[pltpu.VMEM((tile, D_MODEL), BF16)],
        compiler_params=_params(("arbitrary", "arbitrary")),
        name="in_proj",
    )(xflat, ln_g, ln_b, w_in_bf, b_in)


def _rg_gates(xc, wa_ref, wi_ref, ba, bi, sp):
    xcb = xc.astype(BF16)
    nblk = D_RNN // MXU_DIM
    r_pre = jnp.concatenate(
        [_dot(xcb[:, k * MXU_DIM:(k + 1) * MXU_DIM], wa_ref[k]) for k in range(nblk)], axis=1)
    i_pre = jnp.concatenate(
        [_dot(xcb[:, k * MXU_DIM:(k + 1) * MXU_DIM], wi_ref[k]) for k in range(nblk)], axis=1)
    r = _sigmoid(r_pre + ba)
    i = _sigmoid(i_pre + bi)
    log_a = (-LRU_C * r) * sp
    a = jnp.exp(log_a)
    u = jnp.sqrt(_neg_expm1(2.0 * log_a)) * (i * xc)
    return a, u


def _odd_rows(width):
    return lax.broadcasted_iota(I32, (SUBLANES, width), 0) >= NSEQ


def _mixer_chunk(rows, xa_ref, ya_ref, us_ref, ua_ref, gl_ref, w, s):
    (cw_ref, cb_ref, wa_ref, wi_ref, ba_ref, bi_ref, lam_ref, bdb_ref, cre_ref, cim_ref,
     d_ref, gluw_ref, glub_ref) = w
    (ext_scr, a_scr, u_scr, hs_scr, bu_scr, hst_scr, y_scr, tail_scr, hcar_scr, s5car_scr,
     cst_scr) = s
    emit = ua_ref is not None
    halo = NSEQ * CONV_W
    ngroups = rows // SUBLANES

    ext_scr[pl.ds(0, halo), :] = tail_scr[...]
    ext_scr[pl.ds(halo, rows), :] = xa_ref[...]
    xc = cb_ref[...] + cw_ref[pl.ds(CONV_W - 1, 1), :] * ext_scr[pl.ds(halo, rows), :]
    for j in range(1, CONV_W):
        xc = xc + cw_ref[pl.ds(CONV_W - 1 - j, 1), :] * ext_scr[pl.ds(halo - NSEQ * j, rows), :]
    tail_scr[...] = ext_scr[pl.ds(rows, halo), :]

    sp = _softplus(-lam_ref[...])
    a, u = _rg_gates(xc, wa_ref, wi_ref, ba_ref[...], bi_ref[...], sp)
    a_scr[pl.ds(0, rows), :] = a
    u_scr[pl.ds(0, rows), :] = u
    odd = _odd_rows(D_RNN)

    def rg_body(g, c):
        row = pl.multiple_of(g * SUBLANES, SUBLANES)
        a_v = a_scr[pl.ds(row, SUBLANES), :]
        u_v = u_scr[pl.ds(row, SUBLANES), :]
        hl = u_v + jnp.where(odd, a_v * pltpu.roll(u_v, NSEQ, 0), 0.0)
        p = jnp.where(odd, a_v * pltpu.roll(a_v, NSEQ, 0), a_v)
        hs_scr[pl.ds(row, SUBLANES), :] = hl + p * c
        q = jnp.where(odd, hl, pltpu.roll(hl, NSEQ, 0))
        pp = jnp.where(odd, p, pltpu.roll(p, NSEQ, 0))
        return q + pp * c

    hcar_scr[...] = lax.fori_loop(0, ngroups, rg_body, hcar_scr[...])
    if emit:
        ua_ref[...] = (hs_scr[pl.ds(0, rows), :] * _gelu(ya_ref[...].astype(F32))).astype(BF16)

    odd_s = _odd_rows(S5_BLOCK_STATE)
    for kb in range(S5_BLOCKS):
        lo = kb * S5_BLOCK_STATE
        ub = us_ref[:, kb * MXU_DIM:(kb + 1) * MXU_DIM]
        bu_scr[pl.ds(0, rows), :] = _dot(ub, bdb_ref[kb])
        aor, aoi, pr, pi, a2r, a2i = [cst_scr[i, :, lo:lo + S5_BLOCK_STATE] for i in range(6)]

        def s5_body(g, c, aor=aor, aoi=aoi, pr=pr, pi=pi, a2r=a2r, a2i=a2i):
            cr, ci = c
            row = pl.multiple_of(g * SUBLANES, SUBLANES)
            bur = bu_scr[pl.ds(row, SUBLANES), 0:S5_BLOCK_STATE]
            bui = bu_scr[pl.ds(row, SUBLANES), S5_BLOCK_STATE:2 * S5_BLOCK_STATE]
            sr = pltpu.roll(bur, NSEQ, 0)
            si = pltpu.roll(bui, NSEQ, 0)
            hlr = bur + aor * sr - aoi * si
            hli = bui + aor * si + aoi * sr
            hr = hlr + pr * cr - pi * ci
            hi = hli + pr * ci + pi * cr
            if emit:
                hst_scr[pl.ds(row, SUBLANES), 0:S5_BLOCK_STATE] = hr
                hst_scr[pl.ds(row, SUBLANES), S5_BLOCK_STATE:2 * S5_BLOCK_STATE] = hi
            return (jnp.where(odd_s, hr, pltpu.roll(hr, NSEQ, 0)),
                    jnp.where(odd_s, hi, pltpu.roll(hi, NSEQ, 0)))

        cr, ci = lax.fori_loop(
            0, ngroups, s5_body,
            (s5car_scr[0, :, lo:lo + S5_BLOCK_STATE], s5car_scr[1, :, lo:lo + S5_BLOCK_STATE]))
        s5car_scr[0, :, lo:lo + S5_BLOCK_STATE] = cr
        s5car_scr[1, :, lo:lo + S5_BLOCK_STATE] = ci
        if emit:
            hre = hst_scr[pl.ds(0, rows), 0:S5_BLOCK_STATE].astype(BF16)
            him = hst_scr[pl.ds(0, rows), S5_BLOCK_STATE:2 * S5_BLOCK_STATE].astype(BF16)
            y = _dot(hre, cre_ref[kb]) - _dot(him, cim_ref[kb])
            y = y + d_ref[:, kb * MXU_DIM:(kb + 1) * MXU_DIM] * ub.astype(F32)
            y_scr[pl.ds(0, rows), kb * MXU_DIM:(kb + 1) * MXU_DIM] = y

    if emit:
        g5 = _gelu(y_scr[pl.ds(0, rows), :])
        gate = _sigmoid(_dot(g5.astype(BF16), gluw_ref[...]) + glub_ref[...])
        gl_ref[...] = (g5 * gate).astype(BF16)


def _mixer_body(meta_rows, n_s, xa_ref, ya_ref, us_ref, xam_ref, usm_ref, ar_ref, ai_ref,
                h0_ref, cbuf_ref, s5r0_ref, s5i0_ref, *rest):
    w = rest[:13]
    (ua_ref, gl_ref, hout_ref, convout_ref, s5r_ref, s5i_ref,
     h1_ref, cnew_ref, s5r1_ref, s5i1_ref) = rest[13:23]
    s = rest[23:]
    y_scr, tail_scr, hcar_scr, s5car_scr, cst_scr = s[6], s[7], s[8], s[9], s[10]
    c = pl.program_id(0)
    rows = xa_ref.shape[0]
    nchunks = pl.num_programs(0) - 1

    @pl.when(c == nchunks)
    def _():
        head = lambda r: r.at[pl.ds(0, n_s)]
        _sample_step(head(xa_ref), head(ya_ref), head(us_ref), h0_ref, cbuf_ref, s5r0_ref,
                     s5i0_ref, ar_ref, ai_ref, w, head(ua_ref), head(gl_ref), h1_ref, cnew_ref,
                     s5r1_ref, s5i1_ref, head(y_scr))
        ua_ref[pl.ds(n_s, rows - n_s), :] = jnp.zeros((rows - n_s, D_RNN), BF16)
        gl_ref[pl.ds(n_s, rows - n_s), :] = jnp.zeros((rows - n_s, D_S5), BF16)

    @pl.when(c < nchunks)
    def _():
        _prompt_step(meta_rows, nchunks, xa_ref, ya_ref, us_ref, xam_ref, usm_ref, ar_ref, ai_ref,
                     w, ua_ref, gl_ref, hout_ref, convout_ref, s5r_ref, s5i_ref, s)


def _prompt_step(meta_rows, nchunks, xa_ref, ya_ref, us_ref, xam_ref, usm_ref, ar_ref, ai_ref,
                 w, ua_ref, gl_ref, hout_ref, convout_ref, s5r_ref, s5i_ref, s):
    tail_scr, hcar_scr, s5car_scr, cst_scr = s[7], s[8], s[9], s[10]
    c = pl.program_id(0)
    rows = xa_ref.shape[0]

    @pl.when(c == 0)
    def _():
        odd = _odd_rows(S5_STATE)
        ar = jnp.broadcast_to(ar_ref[...], (SUBLANES, S5_STATE))
        ai = jnp.broadcast_to(ai_ref[...], (SUBLANES, S5_STATE))
        a2r = ar * ar - ai * ai
        a2i = 2.0 * (ar * ai)
        cst_scr[0] = jnp.where(odd, ar, 0.0)
        cst_scr[1] = jnp.where(odd, ai, 0.0)
        cst_scr[2] = jnp.where(odd, a2r, ar)
        cst_scr[3] = jnp.where(odd, a2i, ai)
        cst_scr[4] = a2r
        cst_scr[5] = a2i
        tail_scr[...] = jnp.zeros_like(tail_scr)
        hcar_scr[...] = jnp.zeros_like(hcar_scr)
        s5car_scr[...] = jnp.zeros_like(s5car_scr)
        _mixer_chunk(meta_rows, xam_ref, None, usm_ref, None, None, w, s)

    _mixer_chunk(rows, xa_ref, ya_ref, us_ref, ua_ref, gl_ref, w, s)

    @pl.when(c == nchunks - 1)
    def _():
        hout_ref[...] = hcar_scr[pl.ds(NSEQ, NSEQ), :]
        convout_ref[...] = tail_scr[pl.ds(NSEQ, NSEQ * (CONV_W - 1)), :]
        s5r_ref[...] = s5car_scr[0, pl.ds(NSEQ, NSEQ), :]
        s5i_ref[...] = s5car_scr[1, pl.ds(NSEQ, NSEQ), :]


def _mixer_weight_specs():
    c2 = lambda *_: (0, 0)
    c3 = lambda *_: (0, 0, 0)
    nblk = D_RNN // MXU_DIM
    return [
        pl.BlockSpec((CONV_W, D_RNN), c2),
        pl.BlockSpec((1, D_RNN), c2),
        pl.BlockSpec((nblk, MXU_DIM, MXU_DIM), c3),
        pl.BlockSpec((nblk, MXU_DIM, MXU_DIM), c3),
        pl.BlockSpec((1, D_RNN), c2),
        pl.BlockSpec((1, D_RNN), c2),
        pl.BlockSpec((1, D_RNN), c2),
        pl.BlockSpec((S5_BLOCKS, MXU_DIM, 2 * S5_BLOCK_STATE), c3),
        pl.BlockSpec((S5_BLOCKS, S5_BLOCK_STATE, MXU_DIM), c3),
        pl.BlockSpec((S5_BLOCKS, S5_BLOCK_STATE, MXU_DIM), c3),
        pl.BlockSpec((1, D_S5), c2),
        pl.BlockSpec((D_S5, D_S5), c2),
        pl.BlockSpec((1, D_S5), c2),
    ]


def _mixer(xa, z16, a_r, a_i, h0, cbuf, s5r0, s5i0, weights, n_prompt, n_s, meta_row0, meta_rows,
           chunk):
    nchunks = n_prompt // chunk
    n1 = xa.shape[0]
    assert n1 == n_prompt + chunk and n_s <= chunk
    meta_blk = meta_row0 // meta_rows
    halo = NSEQ * CONV_W
    c2 = lambda c: (0, 0)
    in_specs = [
        pl.BlockSpec((chunk, D_RNN), lambda c: (c, 0)),
        pl.BlockSpec((chunk, D_RNN), lambda c: (c, 0)),
        pl.BlockSpec((chunk, D_S5), lambda c: (c, 1)),
        pl.BlockSpec((meta_rows, D_RNN), lambda c: (meta_blk, 0)),
        pl.BlockSpec((meta_rows, D_S5), lambda c: (meta_blk, 1)),
        pl.BlockSpec((1, S5_STATE), c2),
        pl.BlockSpec((1, S5_STATE), c2),
        pl.BlockSpec((n_s, D_RNN), c2),
        pl.BlockSpec((n_s, (CONV_W - 1) * D_RNN), c2),
        pl.BlockSpec((n_s, S5_STATE), c2),
        pl.BlockSpec((n_s, S5_STATE), c2),
    ] + _mixer_weight_specs()
    out_specs = [
        pl.BlockSpec((chunk, D_RNN), lambda c: (c, 0)),
        pl.BlockSpec((chunk, D_S5), lambda c: (c, 0)),
        pl.BlockSpec((NSEQ, D_RNN), c2),
        pl.BlockSpec((NSEQ * (CONV_W - 1), D_RNN), c2),
        pl.BlockSpec((NSEQ, S5_STATE), c2),
        pl.BlockSpec((NSEQ, S5_STATE), c2),
        pl.BlockSpec((n_s, D_RNN), c2),
        pl.BlockSpec((n_s, (CONV_W - 1) * D_RNN), c2),
        pl.BlockSpec((n_s, S5_STATE), c2),
        pl.BlockSpec((n_s, S5_STATE), c2),
    ]
    out_shape = [
        jax.ShapeDtypeStruct((n1, D_RNN), BF16),
        jax.ShapeDtypeStruct((n1, D_S5), BF16),
        jax.ShapeDtypeStruct((NSEQ, D_RNN), F32),
        jax.ShapeDtypeStruct((NSEQ * (CONV_W - 1), D_RNN), F32),
        jax.ShapeDtypeStruct((NSEQ, S5_STATE), F32),
        jax.ShapeDtypeStruct((NSEQ, S5_STATE), F32),
        jax.ShapeDtypeStruct((n_s, D_RNN), F32),
        jax.ShapeDtypeStruct((n_s, (CONV_W - 1) * D_RNN), F32),
        jax.ShapeDtypeStruct((n_s, S5_STATE), F32),
        jax.ShapeDtypeStruct((n_s, S5_STATE), F32),
    ]
    scratch = [
        pltpu.VMEM((chunk + halo, D_RNN), F32),
        pltpu.VMEM((chunk, D_RNN), F32),
        pltpu.VMEM((chunk, D_RNN), F32),
        pltpu.VMEM((chunk, D_RNN), F32),
        pltpu.VMEM((chunk, 2 * S5_BLOCK_STATE), F32),
        pltpu.VMEM((chunk, 2 * S5_BLOCK_STATE), F32),
        pltpu.VMEM((chunk, D_S5), F32),
        pltpu.VMEM((halo, D_RNN), F32),
        pltpu.VMEM((SUBLANES, D_RNN), F32),
        pltpu.VMEM((2, SUBLANES, S5_STATE), F32),
        pltpu.VMEM((6, SUBLANES, S5_STATE), F32),
    ]
    return pl.pallas_call(
        functools.partial(_mixer_body, meta_rows, n_s),
        grid=(nchunks + 1,),
        in_specs=in_specs,
        out_specs=out_specs,
        out_shape=out_shape,
        scratch_shapes=scratch,
        compiler_params=_params(("arbitrary",)),
        name="mixer",
    )(xa, z16, z16, xa, z16, a_r, a_i, h0, cbuf, s5r0, s5i0, *weights)


def _sample_step(xa_ref, ya_ref, us_ref, h0_ref, cbuf_ref, s5r0_ref, s5i0_ref, ar_ref, ai_ref,
                 w, ua_ref, gl_ref, h1_ref, cnew_ref, s5r1_ref, s5i1_ref, y_scr):
    (cw_ref, cb_ref, wa_ref, wi_ref, ba_ref, bi_ref, lam_ref, bdb_ref, cre_ref, cim_ref,
     d_ref, gluw_ref, glub_ref) = w
    xa = xa_ref[...]
    xc = cb_ref[...] + cw_ref[pl.ds(CONV_W - 1, 1), :] * xa
    for k in range(CONV_W - 1):
        xc = xc + cw_ref[pl.ds(k, 1), :] * cbuf_ref[:, k * D_RNN:(k + 1) * D_RNN]
    for k in range(CONV_W - 2):
        cnew_ref[:, k * D_RNN:(k + 1) * D_RNN] = cbuf_ref[:, (k + 1) * D_RNN:(k + 2) * D_RNN]
    cnew_ref[:, (CONV_W - 2) * D_RNN:(CONV_W - 1) * D_RNN] = xa

    sp = _softplus(-lam_ref[...])
    a, u = _rg_gates(xc, wa_ref, wi_ref, ba_ref[...], bi_ref[...], sp)
    h1 = a * h0_ref[...] + u
    h1_ref[...] = h1
    ua_ref[...] = (h1 * _gelu(ya_ref[...].astype(F32))).astype(BF16)

    for kb in range(S5_BLOCKS):
        lo = kb * S5_BLOCK_STATE
        ub = us_ref[:, kb * MXU_DIM:(kb + 1) * MXU_DIM]
        bu = _dot(ub, bdb_ref[kb])
        ar = ar_ref[:, lo:lo + S5_BLOCK_STATE]
        ai = ai_ref[:, lo:lo + S5_BLOCK_STATE]
        h0r = s5r0_ref[:, lo:lo + S5_BLOCK_STATE]
        h0i = s5i0_ref[:, lo:lo + S5_BLOCK_STATE]
        hr = bu[:, 0:S5_BLOCK_STATE] + ar * h0r - ai * h0i
        hi = bu[:, S5_BLOCK_STATE:2 * S5_BLOCK_STATE] + ar * h0i + ai * h0r
        s5r1_ref[:, lo:lo + S5_BLOCK_STATE] = hr
        s5i1_ref[:, lo:lo + S5_BLOCK_STATE] = hi
        y = _dot(hr.astype(BF16), cre_ref[kb]) - _dot(hi.astype(BF16), cim_ref[kb])
        y_scr[:, kb * MXU_DIM:(kb + 1) * MXU_DIM] = (
            y + d_ref[:, kb * MXU_DIM:(kb + 1) * MXU_DIM] * ub.astype(F32))

    g5 = _gelu(y_scr[...])
    gate = _sigmoid(_dot(g5.astype(BF16), gluw_ref[...]) + glub_ref[...])
    gl_ref[...] = (g5 * gate).astype(BF16)


def _col_min(x):
    return jnp.min(x, axis=0, keepdims=True)


def _col_max(x):
    return jnp.max(x, axis=0, keepdims=True)


def _col_sum(x):
    return jnp.sum(x, axis=0, keepdims=True)


def _route(sel, scores, tm):
    neg = -jnp.inf
    iota = lax.broadcasted_iota(I32, (GROUP_SIZE, tm), 0)
    sel_b = [sel[g * GROUP_SIZE:(g + 1) * GROUP_SIZE, :] for g in range(N_GROUPS)]
    sc_b = [scores[g * GROUP_SIZE:(g + 1) * GROUP_SIZE, :] for g in range(N_GROUPS)]

    iota_g = lax.broadcasted_iota(I32, (N_GROUPS, tm), 0)
    gs = jnp.zeros((N_GROUPS, tm), F32)
    for g in range(N_GROUPS):
        b = sel_b[g]
        m1 = _col_max(b)
        i1 = _col_min(jnp.where(b == m1, iota, GROUP_SIZE))
        m2 = _col_max(jnp.where(iota == i1, neg, b))
        gs = jnp.where(iota_g == g, m1 + m2, gs)

    keep = jnp.zeros((N_GROUPS, tm), I32)
    work = gs
    for _ in range(TOPK_GROUPS):
        m = _col_max(work)
        idx = _col_min(jnp.where(work == m, iota_g, N_GROUPS))
        hit = iota_g == idx
        keep = jnp.where(hit, 1, keep)
        work = jnp.where(hit, neg, work)

    cand = [jnp.where(keep[g:g + 1, :] > 0, sel_b[g], neg) for g in range(N_GROUPS)]
    ids, vals = [], []
    for _ in range(TOP_K):
        m = _col_max(cand[0])
        for g in range(1, N_GROUPS):
            m = jnp.maximum(m, _col_max(cand[g]))
        idx = _col_min(jnp.where(cand[0] == m, iota, N_EXPERTS))
        for g in range(1, N_GROUPS):
            idx = jnp.minimum(
                idx, _col_min(jnp.where(cand[g] == m, iota + g * GROUP_SIZE, N_EXPERTS)))
        val = jnp.zeros((1, tm), F32)
        for g in range(N_GROUPS):
            hit = (iota + g * GROUP_SIZE) == idx
            val = val + _col_sum(jnp.where(hit, sc_b[g], 0.0))
            cand[g] = jnp.where(hit, neg, cand[g])
        ids.append(idx)
        vals.append(val)
    return ids, vals


def _post_body(ua_ref, gl_ref, ga_ref, gb_ref, x_ref, ling_ref, linb_ref, pa_ref, pb_ref,
               wo_ref, l1g_ref, l1b_ref, x1_ref):
    branch_a = _dot(ua_ref[...], pa_ref[...])
    branch_b = _dot(gl_ref[...], pb_ref[...])
    merged = (_sigmoid(ga_ref[...].astype(F32)) * branch_a
              + _sigmoid(gb_ref[...].astype(F32)) * branch_b)
    o = _dot(merged.astype(BF16), wo_ref[...])
    xn = _layernorm(x_ref[...], ling_ref[...], linb_ref[...])
    x1_ref[...] = _layernorm(ALPHA * xn + o, l1g_ref[...], l1b_ref[...])


def _post_mixer(ua, gl, z16, xflat, ln_in_g, ln_in_b, proj_a, proj_b, w_o, ln1_g, ln1_b,
                n_tok, tile):
    c2 = lambda i: (0, 0)
    in_specs = [
        pl.BlockSpec((tile, D_RNN), lambda i: (i, 0)),
        pl.BlockSpec((tile, D_S5), lambda i: (i, 0)),
        pl.BlockSpec((tile, D_MODEL), lambda i: (i, 1)),
        pl.BlockSpec((tile, D_MODEL), lambda i: (i, 2)),
        pl.BlockSpec((tile, D_MODEL), lambda i: (i, 0)),
        pl.BlockSpec((1, D_MODEL), c2),
        pl.BlockSpec((1, D_MODEL), c2),
        pl.BlockSpec((D_RNN, D_MODEL), c2),
        pl.BlockSpec((D_S5, D_MODEL), c2),
        pl.BlockSpec((D_MODEL, D_MODEL), c2),
        pl.BlockSpec((1, D_MODEL), c2),
        pl.BlockSpec((1, D_MODEL), c2),
    ]
    return pl.pallas_call(
        _post_body,
        grid=(n_tok // tile,),
        in_specs=in_specs,
        out_specs=pl.BlockSpec((tile, D_MODEL), lambda i: (i, 0)),
        out_shape=jax.ShapeDtypeStruct((n_tok, D_MODEL), F32),
        compiler_params=_params(("arbitrary",)),
        name="post_mixer",
    )(ua, gl, z16, z16, xflat, ln_in_g, ln_in_b, proj_a, proj_b, w_o, ln1_g, ln1_b)


def _router_body(x1_ref, wrh_ref, wrl_ref, rb_ref,
                 eidx_ref, rank_ref, gatet_ref, cnt_ref, cnt_scr):
    i = pl.program_id(0)
    tm = x1_ref.shape[0]

    @pl.when(i == 0)
    def _():
        cnt_scr[...] = jnp.zeros_like(cnt_scr)

    x1 = x1_ref[...]

    x_hi = x1.astype(BF16)
    x_lo = (x1 - x_hi.astype(F32)).astype(BF16)
    nt = (((1,), (1,)), ((), ()))
    dg = lambda a, b: lax.dot_general(a, b, nt, preferred_element_type=F32)
    logits = dg(wrh_ref[...], x_hi) + dg(wrh_ref[...], x_lo) + dg(wrl_ref[...], x_hi)
    scores = _sigmoid(logits)
    sel = scores + rb_ref[...]
    ids, vals = _route(sel, scores, tm)

    total = vals[0]
    for v in vals[1:]:
        total = total + v
    iota_k = lax.broadcasted_iota(I32, (TOP_K, tm), 0)
    iota_e = lax.broadcasted_iota(I32, (N_EXPERTS, tm), 0)
    eidx = jnp.zeros((TOP_K, tm), I32)
    gates = jnp.zeros((TOP_K, tm), F32)
    selm = jnp.zeros((N_EXPERTS, tm), F32)
    for k in range(TOP_K):
        eidx = jnp.where(iota_k == k, ids[k], eidx)
        gates = jnp.where(iota_k == k, vals[k] / total * ROUTED_SCALE, gates)
        selm = jnp.where(iota_e == ids[k], 1.0, selm)
    eidx_ref[...] = eidx

    r_i = lax.broadcasted_iota(I32, (tm, tm), 0)
    c_i = lax.broadcasted_iota(I32, (tm, tm), 1)
    upper = jnp.where(r_i < c_i, 1.0, 0.0).astype(BF16)
    rank_all = _dot(selm.astype(BF16), upper) + cnt_scr[...]
    rank = jnp.zeros((TOP_K, tm), F32)
    for k in range(TOP_K):
        rk = _col_sum(jnp.where(iota_e == ids[k], rank_all, 0.0))
        rank = jnp.where(iota_k == k, rk, rank)
    rank_ref[...] = rank.astype(I32)
    cnt_scr[...] = cnt_scr[...] + jnp.sum(selm, axis=1, keepdims=True)
    cnt_ref[...] = cnt_scr[...]

    gpad = jnp.concatenate([gates, jnp.zeros((LANES - TOP_K, tm), F32)], axis=0)
    gatet_ref[...] = gpad.T


def _router(x1, wr_hi, wr_lo, rbias, tile):
    n_tok = x1.shape[0]
    c2 = lambda i: (0, 0)
    return pl.pallas_call(
        _router_body,
        grid=(n_tok // tile,),
        in_specs=[
            pl.BlockSpec((tile, D_MODEL), lambda i: (i, 0)),
            pl.BlockSpec((N_EXPERTS, D_MODEL), c2),
            pl.BlockSpec((N_EXPERTS, D_MODEL), c2),
            pl.BlockSpec((N_EXPERTS, 1), c2),
        ],
        out_specs=[
            pl.BlockSpec((TOP_K, tile), lambda i: (0, i)),
            pl.BlockSpec((TOP_K, tile), lambda i: (0, i)),
            pl.BlockSpec((tile, LANES), lambda i: (i, 0)),
            pl.BlockSpec((N_EXPERTS, 1), c2),
        ],
        out_shape=[
            jax.ShapeDtypeStruct((TOP_K, n_tok), I32),
            jax.ShapeDtypeStruct((TOP_K, n_tok), I32),
            jax.ShapeDtypeStruct((n_tok, LANES), F32),
            jax.ShapeDtypeStruct((N_EXPERTS, 1), F32),
        ],
        scratch_shapes=[pltpu.VMEM((N_EXPERTS, 1), F32)],
        compiler_params=_params(("arbitrary",)),
        name="router",
    )(x1, wr_hi, wr_lo, rbias)


def _positions_body(eidx_ref, rank_ref, cnt_ref, pos_ref, texp_ref, info_ref):
    cnt = cnt_ref[...]
    padded = jnp.floor((cnt + (MOE_TILE - 1)) * (1.0 / MOE_TILE)) * MOE_TILE
    r_i = lax.broadcasted_iota(I32, (N_EXPERTS, N_EXPERTS), 0)
    c_i = lax.broadcasted_iota(I32, (N_EXPERTS, N_EXPERTS), 1)
    eye = r_i == c_i
    as_row = lambda col: jnp.sum(jnp.where(eye, col, 0.0), axis=0, keepdims=True)
    padded_row = as_row(padded)
    base = jnp.sum(jnp.where(c_i < r_i, padded_row, 0.0), axis=1, keepdims=True)
    end = base + padded

    eidx = eidx_ref[...]
    pos = rank_ref[...]
    base_i = base.astype(I32)
    for e in range(N_EXPERTS):
        pos = pos + jnp.where(eidx == e, base_i[e:e + 1, :], 0)
    pos_ref[...] = pos

    ntp = texp_ref.shape[1]
    t_row = lax.broadcasted_iota(I32, (1, ntp), 1).astype(F32) * MOE_TILE
    texp = jnp.sum(jnp.where(end <= t_row, 1, 0), axis=0, keepdims=True)
    texp_ref[...] = jnp.minimum(texp, N_EXPERTS - 1).astype(I32)

    row = lax.broadcasted_iota(I32, (SUBLANES, LANES), 0)
    lane = lax.broadcasted_iota(I32, (SUBLANES, LANES), 1)
    pad_lanes = lambda r: jnp.concatenate(
        [r, jnp.zeros((1, LANES - N_EXPERTS), F32)], axis=1)
    ntiles = jnp.sum(padded, axis=0, keepdims=True) * (1.0 / MOE_TILE)
    used_row = as_row(cnt) > 0.0
    c_f = c_i.astype(F32)
    nxt = jnp.min(jnp.where((c_i > r_i) & used_row, c_f, float(N_EXPERTS)), axis=1, keepdims=True)
    order = jnp.sum(jnp.where((c_i < r_i) & used_row, 1.0, 0.0), axis=1, keepdims=True)
    slot = order - 2.0 * jnp.floor(order * 0.5)
    info = jnp.where(row == 0, pad_lanes(as_row(cnt)), 0.0)
    info = jnp.where(row == 1, pad_lanes(as_row(base)), info)
    info = jnp.where(row == 2, pad_lanes(as_row(end)), info)
    info = jnp.where((row == 3) & (lane == 0), ntiles, info)
    info = jnp.where(row == 4, pad_lanes(as_row(nxt)), info)
    info = jnp.where(row == 5, pad_lanes(as_row(slot)), info)
    info_ref[...] = info.astype(I32)


def _positions(eidx, rank, cnt, n_tiles_max):
    n_tok = eidx.shape[1]
    ntp = -(-n_tiles_max // LANES) * LANES
    return pl.pallas_call(
        _positions_body,
        out_shape=[
            jax.ShapeDtypeStruct((TOP_K, n_tok), I32),
            jax.ShapeDtypeStruct((1, ntp), I32),
            jax.ShapeDtypeStruct((SUBLANES, LANES), I32),
        ],
        compiler_params=pltpu.CompilerParams(vmem_limit_bytes=VMEM_LIMIT),
        name="positions",
    )(eidx, rank, cnt)


def _row_copy(src_hbm, src_row, dst_hbm, dst_row, sem):
    return pltpu.make_async_copy(
        src_hbm.at[pl.ds(src_row, 1)], dst_hbm.at[pl.ds(dst_row, 1)], sem)


def _swiglu(x, w1_ref, w3_ref, w2_ref):
    h = _silu(_dot(x, w1_ref[...])) * _dot(x, w3_ref[...])
    return _dot(h, w2_ref[...])


def _dispatch_body(info_ref, pos_ref, xp_ref, w1_ref, w3_ref, w2_ref,
                   xs_hbm, sh_ref, zero_scr, sem):
    i = pl.program_id(0)
    tm = xp_ref.shape[0]

    def issue(g, carry):
        row0 = pl.multiple_of(g * SUBLANES, SUBLANES)
        for j in range(SUBLANES):
            for k in range(TOP_K):
                _row_copy(xp_ref, row0 + j, xs_hbm, pos_ref[k, row0 + j], sem).start(
                    priority=k % 2)
        return carry

    lax.fori_loop(0, tm // SUBLANES, issue, 0)
    sh_ref[...] = _swiglu(xp_ref[...], w1_ref, w3_ref, w2_ref)

    for k in range(TOP_K):
        pltpu.make_async_copy(xp_ref, xs_hbm.at[pl.ds(0, tm)], sem).wait()

    @pl.when(i == pl.num_programs(0) - 1)
    def _():
        zero_scr[...] = jnp.zeros_like(zero_scr)

        def per_expert(e, carry):
            start = info_ref[1, e] + info_ref[0, e]
            stop = info_ref[2, e]

            def fill(r, c):
                _row_copy(zero_scr, 0, xs_hbm, r, sem).start()
                return c

            def fill_wait(r, c):
                _row_copy(zero_scr, 0, xs_hbm, 0, sem).wait()
                return c

            lax.fori_loop(start, stop, fill, 0)
            lax.fori_loop(start, stop, fill_wait, 0)
            return carry

        lax.fori_loop(0, N_EXPERTS, per_expert, 0)

        ntiles = info_ref[3, 0]
        tiles_alloc = xs_hbm.shape[0] // MOE_TILE

        def tile_copy(j):
            return pltpu.make_async_copy(
                zero_scr, xs_hbm.at[pl.ds(pl.multiple_of(j * MOE_TILE, MOE_TILE), MOE_TILE)], sem)

        def fill_tile(j, c):
            tile_copy(j).start()
            return c

        def fill_tile_wait(j, c):
            tile_copy(j).wait()
            return c

        lax.fori_loop(ntiles, tiles_alloc, fill_tile, 0)
        lax.fori_loop(ntiles, tiles_alloc, fill_tile_wait, 0)


def _dispatch(info, pos, x1, sh_w1, sh_w3, sh_w2, rows_sorted, tile):
    n_tok, half = x1.shape
    c2 = lambda i, info: (0, 0)
    grid_spec = pltpu.PrefetchScalarGridSpec(
        num_scalar_prefetch=1,
        grid=(n_tok // tile,),
        in_specs=[
            pl.BlockSpec((TOP_K, tile), lambda i, info: (0, i), memory_space=pltpu.SMEM),
            pl.BlockSpec((tile, half), lambda i, info: (i, 0)),
            pl.BlockSpec((D_MODEL, D_EXPERT), c2),
            pl.BlockSpec((D_MODEL, D_EXPERT), c2),
            pl.BlockSpec((D_EXPERT, D_MODEL), c2),
        ],
        out_specs=[
            pl.BlockSpec(memory_space=pl.ANY),
            pl.BlockSpec((tile, D_MODEL), lambda i, info: (i, 0)),
        ],
        scratch_shapes=[pltpu.VMEM((MOE_TILE, half), F32), pltpu.SemaphoreType.DMA],
    )
    return pl.pallas_call(
        _dispatch_body,
        grid_spec=grid_spec,
        out_shape=[
            jax.ShapeDtypeStruct((rows_sorted, half), F32),
            jax.ShapeDtypeStruct((n_tok, D_MODEL), F32),
        ],
        compiler_params=_params(("arbitrary",)),
        name="dispatch",
    )(info, pos, x1, sh_w1, sh_w3, sh_w2)


def _moe_body(texp_ref, info_ref, xs_ref, w1_hbm, w3_hbm, w2_hbm, ys_ref,
              w1f_scr, w3f_scr, w2f_scr, sems):
    t = pl.program_id(0)
    ntiles = info_ref[3, 0]
    tt = jnp.minimum(t, ntiles - 1)
    prev = jnp.maximum(tt - 1, 0)
    expert = texp_ref[0, tt]
    fresh = (t == 0) | (expert != texp_ref[0, prev])
    valid = t < ntiles

    def weight_copies(ex, slot):
        return [
            pltpu.make_async_copy(w1_hbm.at[ex], w1f_scr.at[slot], sems.at[0, slot]),
            pltpu.make_async_copy(w3_hbm.at[ex], w3f_scr.at[slot], sems.at[1, slot]),
            pltpu.make_async_copy(w2_hbm.at[ex], w2f_scr.at[slot], sems.at[2, slot]),
        ]

    @pl.when(t == 0)
    def _():
        for c in weight_copies(expert, info_ref[5, expert]):
            c.start()

    @pl.when(valid & fresh)
    def _():
        slot = info_ref[5, expert]
        for c in weight_copies(expert, slot):
            c.wait()
        nxt = info_ref[4, expert]

        @pl.when(nxt < N_EXPERTS)
        def _():
            for c in weight_copies(nxt, 1 - slot):
                c.start()

    @pl.when(valid)
    def _():
        slot = info_ref[5, expert]
        x = xs_ref[...]
        h = _silu(_dot(x, w1f_scr[slot])) * _dot(x, w3f_scr[slot])
        ys_ref[...] = _dot(h, w2f_scr[slot])

    @pl.when(jnp.logical_not(valid))
    def _():
        ys_ref[...] = jnp.zeros_like(ys_ref)


def _moe(texp, info, xs, ex_w1, ex_w3, ex_w2, n_tiles_max):
    half = D_MODEL

    def tile_idx(t, texp, info):
        return jnp.minimum(t, info[3, 0] - 1)

    grid_spec = pltpu.PrefetchScalarGridSpec(
        num_scalar_prefetch=2,
        grid=(n_tiles_max,),
        in_specs=[
            pl.BlockSpec((MOE_TILE, half), lambda t, texp, info: (tile_idx(t, texp, info), 0)),
            pl.BlockSpec(memory_space=pl.ANY),
            pl.BlockSpec(memory_space=pl.ANY),
            pl.BlockSpec(memory_space=pl.ANY),
        ],
        out_specs=pl.BlockSpec((MOE_TILE, half), lambda t, texp, info: (t, 0)),
        scratch_shapes=[
            pltpu.VMEM((2, D_MODEL, D_EXPERT), F32),
            pltpu.VMEM((2, D_MODEL, D_EXPERT), F32),
            pltpu.VMEM((2, D_EXPERT, D_MODEL), F32),
            pltpu.SemaphoreType.DMA((3, 2)),
        ],
    )
    return pl.pallas_call(
        _moe_body,
        grid_spec=grid_spec,
        out_shape=jax.ShapeDtypeStruct(xs.shape, F32),
        compiler_params=_params(("arbitrary",)),
        name="moe_experts",
    )(texp, info, xs, ex_w1, ex_w3, ex_w2)


def _combine_body(n_prompt_tiles, pos_ref, ys_hbm, gate_ref, x1_ref, sh_ref, g_ref, b_ref,
                  yp_ref, ysm_ref, buf_scr, y_scr, sem):
    i = pl.program_id(0)
    tm = x1_ref.shape[0]

    def issue(g, carry):
        row0 = pl.multiple_of(g * SUBLANES, SUBLANES)
        for j in range(SUBLANES):
            for k in range(TOP_K):
                pltpu.make_async_copy(
                    ys_hbm.at[pl.ds(pos_ref[k, row0 + j], 1)],
                    buf_scr.at[k, pl.ds(row0 + j, 1)], sem,
                ).start(priority=k % 2)
        return carry

    lax.fori_loop(0, tm // SUBLANES, issue, 0)

    for k in range(TOP_K):
        pltpu.make_async_copy(ys_hbm.at[pl.ds(0, tm)], buf_scr.at[k], sem).wait()

    gates = gate_ref[...]
    acc = sh_ref[...]
    for k in range(TOP_K):
        acc = acc + gates[:, k:k + 1] * buf_scr[k]
    y = _layernorm(ALPHA * x1_ref[...] + acc, g_ref[...], b_ref[...])

    @pl.when(i < n_prompt_tiles)
    def _():
        steps = tm // NSEQ
        for c in range(D_MODEL // LANES):
            y_scr[c] = y[:, c * LANES:(c + 1) * LANES]
        for s in range(NSEQ):
            for c in range(D_MODEL // LANES):
                yp_ref[s, :, c * LANES:(c + 1) * LANES] = y_scr[c, pl.ds(s, steps, stride=NSEQ), :]

    @pl.when(i >= n_prompt_tiles)
    def _():
        ysm_ref[...] = y


def _combine(pos, ys, gate_t, x1, sh_out, ln2_g, ln2_b, tile, n_prompt):
    n_tok = x1.shape[0]
    n_s = n_tok - n_prompt
    assert n_prompt % tile == 0 and n_s % tile == 0
    npt = n_prompt // tile
    steps = tile // NSEQ
    c2 = lambda i: (0, 0)
    return pl.pallas_call(
        functools.partial(_combine_body, npt),
        grid=(n_tok // tile,),
        in_specs=[
            pl.BlockSpec((TOP_K, tile), lambda i: (0, i), memory_space=pltpu.SMEM),
            pl.BlockSpec(memory_space=pl.ANY),
            pl.BlockSpec((tile, LANES), lambda i: (i, 0)),
            pl.BlockSpec((tile, D_MODEL), lambda i: (i, 0)),
            pl.BlockSpec((tile, D_MODEL), lambda i: (i, 0)),
            pl.BlockSpec((1, D_MODEL), c2),
            pl.BlockSpec((1, D_MODEL), c2),
        ],
        out_specs=[
            pl.BlockSpec((NSEQ, steps, D_MODEL), lambda i: (0, jnp.minimum(i, npt - 1), 0)),
            pl.BlockSpec((tile, D_MODEL), lambda i: (jnp.maximum(i - npt, 0), 0)),
        ],
        out_shape=[
            jax.ShapeDtypeStruct((NSEQ, n_prompt // NSEQ, D_MODEL), F32),
            jax.ShapeDtypeStruct((n_s, D_MODEL), F32),
        ],
        scratch_shapes=[
            pltpu.VMEM((TOP_K, tile, D_MODEL), F32),
            pltpu.VMEM((D_MODEL // LANES, tile, LANES), F32),
            pltpu.SemaphoreType.DMA,
        ],
        compiler_params=_params(("arbitrary",)),
        name="combine",
    )(pos, ys, gate_t, x1, sh_out, ln2_g, ln2_b)


def _pick_tile(n, cap, mult):
    best = mult
    for t in range(mult, cap + 1, mult):
        if n % t == 0:
            best = t
    assert n % best == 0
    return best


def kernel(x_prompt, x_sample, state_rglru_h, state_conv, state_s5_re, state_s5_im, meta_tokens, ln_in_g, ln_in_b, w_in, b_in, conv_w, conv_b, rg_wa, rg_ba, rg_wi, rg_bi, rg_lambda, s5_a_re, s5_a_im, s5_b_re, s5_b_im, s5_c_re, s5_c_im, s5_d, s5_log_dt, glu_w, glu_b, proj_a, proj_b, w_o, ln1_g, ln1_b, router_w, router_bias, ex_w1, ex_w3, ex_w2, sh_w1, sh_w3, sh_w2, ln2_g, ln2_b):
    bp, seq, d = x_prompt.shape
    n_s = x_sample.shape[0]
    assert bp == NSEQ and d == D_MODEL and x_sample.shape[1] == 1
    assert w_in.shape[0] == DEPTH
    n_prompt = bp * seq
    n_tok = n_prompt + n_s
    meta_rows = NSEQ * N_META
    n1 = n_tok + 2 * meta_rows
    row = lambda v: v.reshape(1, -1)

    xflat = jnp.concatenate([
        jnp.transpose(x_prompt, (1, 0, 2)).reshape(n_prompt, d),
        x_sample.reshape(n_s, d),
        jnp.repeat(meta_tokens, NSEQ, axis=0),
        jnp.zeros((meta_rows, d), F32),
    ], axis=0)

    tile1 = _pick_tile(n1, 1056, 2 * SUBLANES)
    xa, z16 = _inproj(xflat, row(ln_in_g), row(ln_in_b), w_in[0].astype(BF16), b_in, tile1)

    a_r, a_i, bb_r, bb_i = _s5_prep(s5_a_re[0], s5_a_im[0], s5_log_dt[0], s5_b_re[0], s5_b_im[0])
    bg = S5_BLOCK_GROUPS
    bd_b = jnp.concatenate([
        _block_diag(bb_r.reshape(S5_BLOCKS, bg, S5_CH, S5_N)),
        _block_diag(bb_i.reshape(S5_BLOCKS, bg, S5_CH, S5_N)),
    ], axis=2).astype(BF16)
    c_t = lambda c: jnp.transpose(c[0], (0, 2, 1)).reshape(S5_BLOCKS, bg, S5_N, S5_CH)
    bd_cre = _block_diag(c_t(s5_c_re)).astype(BF16)
    bd_cim = _block_diag(c_t(s5_c_im)).astype(BF16)
    heads_per_blk = RG_HEADS // (D_RNN // MXU_DIM)
    rg_blk = lambda w: _block_diag(
        w[0].reshape(D_RNN // MXU_DIM, heads_per_blk, D_RNN // RG_HEADS, D_RNN // RG_HEADS)
    ).astype(BF16)
    weights = (conv_w[0], conv_b, rg_blk(rg_wa), rg_blk(rg_wi), rg_ba, rg_bi, rg_lambda,
               bd_b, bd_cre, bd_cim, row(s5_d[0]), glu_w[0].astype(BF16), glu_b)

    chunk = n_s + 2 * meta_rows
    assert n_prompt % chunk == 0
    ua, gl, p_h, p_conv, p_s5r, p_s5i, s_h, s_conv, s_s5r, s_s5i = _mixer(
        xa, z16, a_r, a_i, state_rglru_h[0],
        state_conv[0].reshape(n_s, (CONV_W - 1) * D_RNN),
        state_s5_re[0].reshape(n_s, S5_STATE), state_s5_im[0].reshape(n_s, S5_STATE),
        weights, n_prompt, n_s, n_tok, meta_rows, chunk)

    wr_t = jnp.transpose(router_w[0])
    wr_hi = wr_t.astype(BF16)
    wr_lo = (wr_t - wr_hi.astype(F32)).astype(BF16)
    tile3 = _pick_tile(n_tok, POST_TILE, 2 * SUBLANES)
    x1 = _post_mixer(
        ua, gl, z16, xflat, row(ln_in_g), row(ln_in_b), proj_a[0].astype(BF16),
        proj_b[0].astype(BF16), w_o[0].astype(BF16), ln1_g, ln1_b, n_tok, tile3)
    tile4 = _pick_tile(n_tok, TOKEN_TILE, LANES)
    eidx, rank, gate_t, cnt = _router(
        x1, wr_hi, wr_lo, router_bias[0].reshape(N_EXPERTS, 1), tile4)

    rows_max = n_tok * TOP_K + N_EXPERTS * (MOE_TILE - 1)
    n_tiles_max = -(-rows_max // MOE_TILE)
    pos, texp, info = _positions(eidx, rank, cnt, n_tiles_max)
    xs, sh_out = _dispatch(info, pos, x1, sh_w1[0], sh_w3[0], sh_w2[0],
                           n_tiles_max * MOE_TILE, tile4)
    ys = _moe(texp, info, xs, ex_w1[0], ex_w3[0], ex_w2[0], n_tiles_max)
    y_prompt, y_sample = _combine(pos, ys, gate_t, x1, sh_out, ln2_g, ln2_b,
                                  _pick_tile(n_s, FIN_TILE, LANES), n_prompt)

    dt = x_prompt.dtype
    y_sample = y_sample.reshape(n_s, 1, d)
    conv_p = jnp.transpose(p_conv.reshape(CONV_W - 1, bp, D_RNN), (1, 0, 2))
    s5_shape = (S5_GROUPS, S5_N)
    return (y_prompt.astype(dt), y_sample.astype(dt),
            p_h[None], conv_p[None],
            p_s5r.reshape(1, bp, *s5_shape), p_s5i.reshape(1, bp, *s5_shape),
            s_h[None], s_conv.reshape(1, n_s, CONV_W - 1, D_RNN),
            s_s5r.reshape(1, n_s, *s5_shape), s_s5i.reshape(1, n_s, *s5_shape))
```
